```python
import math
import jax, jax.numpy as jnp
from jax import lax
import numpy as np

D_MODEL = 1024
BATCH = 4
SEQ = 4096
DEPTH = 2
DEC_BATCH = 32
DEC_SEQ = 4
PAST_LEN = 8192
PAGE_SIZE = 128

HEAD_DIM = 128
N_HEADS = 4
BRANCH_W = N_HEADS * HEAD_DIM
N_BRANCH = 3
CONV_W = 4
GDN_CHUNK = 64
RET_CHUNK = 64
MOBA_BLOCK = 256
MOBA_TOPK = 3
MOBA_QBLOCK = 64
ROPE_THETA = 500000.0
ROPE_DIMS = HEAD_DIM // 4
RET_THETA = 10000.0
EPS = 1e-6
F32 = jnp.float32
IN_SIZES = (3 * BRANCH_W, BRANCH_W, N_HEADS, N_HEADS,
            BRANCH_W, BRANCH_W, BRANCH_W, BRANCH_W,
            BRANCH_W, BRANCH_W, BRANCH_W, BRANCH_W,
            N_BRANCH * D_MODEL)
IN_W = 12 * BRANCH_W + 2 * N_HEADS + N_BRANCH * D_MODEL

kernel_name = 'hybrid_gdn_moba_retention_step'


def rmsnorm(x, g):
    xf = x.astype(F32)
    y = xf * lax.rsqrt(jnp.mean(xf * xf, axis=-1, keepdims=True) + EPS)
    return (y * g.astype(F32)).astype(x.dtype)


def l2norm(x):
    xf = x.astype(F32)
    return xf * lax.rsqrt(jnp.sum(xf * xf, axis=-1, keepdims=True) + EPS)


def rope(x, pos, n_rot, theta):
    half = n_rot // 2
    inv = theta ** (-jnp.arange(half, dtype=F32) / half)
    ang = pos.astype(F32)[:, None] * inv[None, :]
    cos = jnp.cos(ang)[:, None, :]
    sin = jnp.sin(ang)[:, None, :]
    xr = x[..., :n_rot].astype(F32)
    x1, x2 = xr[..., :half], xr[..., half:]
    rot = jnp.concatenate([x1 * cos - x2 * sin, x1 * sin + x2 * cos], axis=-1)
    return jnp.concatenate([rot.astype(x.dtype), x[..., n_rot:]], axis=-1)


def causal_conv_silu(u, buf, w):
    T = u.shape[1]
    up = jnp.concatenate([buf.astype(u.dtype), u], axis=1)
    acc = up[:, 0:T] * w[0]
    for j in range(1, CONV_W):
        acc = acc + up[:, j:j + T] * w[j]
    return jax.nn.silu(acc), up[:, T:]


def _to_chunks(a, n, c):
    b, _, h = a.shape[:3]
    return a.reshape((b, n, c, h) + a.shape[3:]).swapaxes(2, 3)


def _from_chunks(o):
    b, n, h, c, d = o.shape
    return o.swapaxes(2, 3).reshape(b, n * c, h, d)


def gated_delta_chunked(q, k, v, g, beta, s0):
    B, T, H, dk = q.shape
    dv = v.shape[-1]
    C = math.gcd(T, GDN_CHUNK)
    N = T // C
    qc = _to_chunks(q.astype(F32), N, C)
    kc = _to_chunks(k.astype(F32), N, C)
    vc = _to_chunks(v.astype(F32), N, C)
    gc = jnp.cumsum(_to_chunks(g.astype(F32), N, C), axis=-1)
    bc = _to_chunks(beta.astype(F32), N, C)[..., None]
    tri = jnp.tril(jnp.ones((C, C), dtype=bool))
    strict = jnp.tril(jnp.ones((C, C), dtype=bool), -1)
    diff = gc[..., :, None] - gc[..., None, :]
    decay = jnp.where(tri, jnp.exp(jnp.where(tri, diff, 0.0)), 0.0)
    kb = kc * bc
    m = jnp.where(strict, jnp.einsum('bnhid,bnhjd->bnhij', kb, kc) * decay, 0.0)
    rhs = jnp.concatenate([vc * bc, kb * jnp.exp(gc)[..., None]], axis=-1)
    sol = lax.linalg.triangular_solve(m + jnp.eye(C, dtype=F32), rhs, left_side=True,
                                      lower=True, unit_diagonal=True)
    u, w = sol[..., :dv], sol[..., dv:]
    attn = jnp.einsum('bnhid,bnhjd->bnhij', qc, kc) * decay
    q_dec = qc * jnp.exp(gc)[..., None]
    k_dec = kc * jnp.exp(gc[..., -1:] - gc)[..., None]
    g_last = jnp.exp(gc[..., -1])

    def step(S, xs):
        u_n, w_n, attn_n, qd_n, kd_n, gl_n = xs
        v_new = u_n - jnp.einsum('bhcd,bhde->bhce', w_n, S)
        o = jnp.einsum('bhcd,bhde->bhce', qd_n, S) + jnp.einsum('bhij,bhje->bhie', attn_n, v_new)
        S = S * gl_n[..., None, None] + jnp.einsum('bhcd,bhce->bhde', kd_n, v_new)
        return S, o

    xs = tuple(a.swapaxes(0, 1) for a in (u, w, attn, q_dec, k_dec, g_last))
    S, o = lax.scan(step, s0.astype(F32), xs)
    return _from_chunks(o.swapaxes(0, 1)), S


def retention_chunked(q, k, v, s0):
    B, T, H, dk = q.shape
    C = math.gcd(T, RET_CHUNK)
    N = T // C
    log_g = jnp.log1p(-(2.0 ** (-5.0 - jnp.arange(H, dtype=F32))))
    idx = jnp.arange(C, dtype=F32)
    tri = jnp.tril(jnp.ones((C, C), dtype=bool))
    dmat = jnp.where(tri, jnp.exp(log_g[:, None, None] * jnp.where(tri, idx[:, None] - idx[None, :], 0.0)), 0.0)
    q_dec = jnp.exp(log_g[:, None] * (idx + 1.0))[..., None]
    k_dec = jnp.exp(log_g[:, None] * (C - 1.0 - idx))[..., None]
    c_dec = jnp.exp(log_g * C)[:, None, None]
    qc = _to_chunks(q.astype(F32), N, C)
    kc = _to_chunks(k.astype(F32), N, C)
    vc = _to_chunks(v.astype(F32), N, C)
    o_intra = jnp.einsum('bnhij,bnhje->bnhie', jnp.einsum('bnhid,bnhjd->bnhij', qc, kc) * dmat, vc)
    kv = jnp.einsum('bnhcd,bnhce->bnhde', kc * k_dec, vc)

    def step(S, xs):
        q_n, kv_n = xs
        o = jnp.einsum('bhcd,bhde->bhce', q_n * q_dec, S)
        return S * c_dec + kv_n, o

    S, o_inter = lax.scan(step, s0.astype(F32), (qc.swapaxes(0, 1), kv.swapaxes(0, 1)))
    return _from_chunks(o_intra + o_inter.swapaxes(0, 1)), S


def moba_attention(q, k_parts, v_parts):
    B, Tq, H, d = q.shape
    L = sum(p.shape[1] for p in k_parts)
    q0 = L - Tq
    nb = max(-(-L // MOBA_BLOCK), MOBA_TOPK)
    pad = nb * MOBA_BLOCK - L
    zpad = jnp.zeros((B, pad, H, d), q.dtype)
    k_all = jnp.concatenate([p.astype(q.dtype) for p in k_parts] + [zpad], axis=1)
    v_all = jnp.concatenate([p.astype(q.dtype) for p in v_parts] + [zpad], axis=1)
    k_blk = k_all.reshape(B, nb, MOBA_BLOCK, H, d)
    v_blk = v_all.reshape(B, nb, MOBA_BLOCK, H, d)
    k_mean = jnp.mean(k_blk.astype(F32), axis=2)
    qb_len = math.gcd(Tq, MOBA_QBLOCK)
    nq = Tq // qb_len
    bi = jnp.arange(B)[:, None, None, None]
    hi = jnp.arange(H)[None, :, None, None]
    blk_ids = jnp.arange(nb)
    n_sel = MOBA_TOPK * MOBA_BLOCK
    scale = d ** -0.5

    def one_block(args):
        qblk, pos = args
        qf = qblk.astype(F32) * scale
        own = pos // MOBA_BLOCK
        gate = jnp.einsum('bqhd,bnhd->bhqn', qf, k_mean)
        gate = jnp.where(blk_ids[None, :] < own[:, None], gate, -jnp.inf)
        _, sel = lax.top_k(gate, MOBA_TOPK)
        sel_ok = sel < own[None, None, :, None]
        ks = k_blk[bi, sel, :, hi, :]
        vs = v_blk[bi, sel, :, hi, :]
        s_sel = jnp.einsum('bqhd,bhqkjd->bhqkj', qf, ks)
        s_sel = jnp.where(sel_ok[..., None], s_sel, -jnp.inf).reshape(B, H, qb_len, n_sel)
        own_idx = own[:, None] * MOBA_BLOCK + jnp.arange(MOBA_BLOCK)[None, :]
        ko = jnp.take(k_all, own_idx, axis=1)
        vo = jnp.take(v_all, own_idx, axis=1)
        s_own = jnp.einsum('bqhd,bqjhd->bhqj', qf, ko)
        s_own = jnp.where((own_idx <= pos[:, None])[None, None], s_own, -jnp.inf)
        probs = jax.nn.softmax(jnp.concatenate([s_sel, s_own], axis=-1), axis=-1)
        p_sel = probs[..., :n_sel].reshape(B, H, qb_len, MOBA_TOPK, MOBA_BLOCK)
        out = (jnp.einsum('bhqkj,bhqkjd->bqhd', p_sel, vs.astype(F32))
               + jnp.einsum('bhqj,bqjhd->bqhd', probs[..., n_sel:], vo.astype(F32)))
        return out.astype(q.dtype)

    q_blocks = q.reshape(B, nq, qb_len, H, d).swapaxes(0, 1)
    pos_blocks = (q0 + jnp.arange(Tq, dtype=jnp.int32)).reshape(nq, qb_len)
    o = lax.map(one_block, (q_blocks, pos_blocks))
    return o.swapaxes(0, 1).reshape(B, Tq, H, d)


def trunk_layer(x, c, pos0, gdn_s0, conv_buf, ret_s0, k_past, v_past,
                norm_in, w_ada, b_ada, w_in, conv_w, a_log, dt_bias, norm_a, norm_c, w_branch, w_out):
    B, T, _ = x.shape
    pos = pos0 + jnp.arange(T, dtype=jnp.int32)
    mod = jax.nn.silu(c) @ w_ada + b_ada
    shift, scale, gate = jnp.split(mod[:, None, :], 3, axis=-1)
    h = rmsnorm(x, norm_in) * (1.0 + scale) + shift
    proj = h @ w_in
    (qkv_a, z_a, beta_in, alpha_in, q_b, k_b, v_b, z_b,
     q_c, k_c, v_c, z_c, merge_in) = jnp.split(proj, np.cumsum(IN_SIZES)[:-1].tolist(), axis=-1)

    def heads(t):
        return t.reshape(B, T, N_HEADS, HEAD_DIM)

    conv_out, conv_new = causal_conv_silu(qkv_a, conv_buf, conv_w)
    q_a, k_a, v_a = [heads(t) for t in jnp.split(conv_out, 3, axis=-1)]
    beta = jax.nn.sigmoid(beta_in.astype(F32))
    g = -jnp.exp(a_log.astype(F32)) * jax.nn.softplus(alpha_in.astype(F32) + dt_bias.astype(F32))
    o_a, gdn_new = gated_delta_chunked(l2norm(q_a) * HEAD_DIM ** -0.5, l2norm(k_a), v_a, g, beta, gdn_s0)
    y_a = rmsnorm(o_a, norm_a).reshape(B, T, BRANCH_W).astype(x.dtype) * jax.nn.silu(z_a)

    qb = rope(heads(q_b), pos, ROPE_DIMS, ROPE_THETA)
    kb = rope(heads(k_b), pos, ROPE_DIMS, ROPE_THETA)
    vb = heads(v_b)
    if k_past is None:
        k_parts, v_parts = [kb], [vb]
    else:
        k_parts, v_parts = [k_past, kb], [v_past, vb]
    o_b = moba_attention(qb, k_parts, v_parts)
    y_b = o_b.reshape(B, T, BRANCH_W) * jax.nn.silu(z_b)

    qc = rope(heads(q_c), pos, HEAD_DIM, RET_THETA)
    kc = rope(heads(k_c), pos, HEAD_DIM, RET_THETA) * HEAD_DIM ** -0.5
    o_c, ret_new = retention_chunked(qc, kc, heads(v_c), ret_s0)
    y_c = rmsnorm(o_c, norm_c).reshape(B, T, BRANCH_W).astype(x.dtype) * jax.nn.silu(z_c)

    branches = jnp.stack([y_a, y_b, y_c], axis=2)
    per_branch = jnp.einsum('btnw,nwd->btnd', branches, w_branch)
    merge = jax.nn.sigmoid(merge_in.reshape(B, T, N_BRANCH, D_MODEL).astype(F32)).astype(x.dtype)
    mixed = jnp.sum(merge * per_branch, axis=2) @ w_out
    x_out = x + gate * mixed
    return (x_out, kb, vb, gdn_new.astype(gdn_s0.dtype), conv_new.astype(conv_buf.dtype),
            ret_new.astype(ret_s0.dtype))


def setup_inputs(seed: int = 0) -> dict:
    key = jax.random.key(seed)
    k = jax.random.split(key, 24)
    nrm = jax.random.normal
    n_pages = PAST_LEN // PAGE_SIZE
    n_used = DEC_BATCH * n_pages
    n_pool = n_used + (n_used + 3) // 4
    page_table = jax.random.permutation(k[0], n_pool)[:n_used].reshape(DEC_BATCH, n_pages).astype(jnp.int32)
    dt = jnp.exp(jax.random.uniform(k[16], (DEPTH, N_HEADS), F32, math.log(1e-3), math.log(0.1)))
    return {
        'x_prompt': nrm(k[1], (BATCH, SEQ, D_MODEL), F32),
        'x_sample': nrm(k[2], (DEC_BATCH, DEC_SEQ, D_MODEL), F32),
        'cache_k': nrm(k[3], (DEPTH, n_pool, PAGE_SIZE, N_HEADS, HEAD_DIM), F32),
        'cache_v': nrm(k[4], (DEPTH, n_pool, PAGE_SIZE, N_HEADS, HEAD_DIM), F32),
        'state_gdn': 0.5 * nrm(k[5], (DEPTH, DEC_BATCH, N_HEADS, HEAD_DIM, HEAD_DIM), F32),
        'state_conv': nrm(k[6], (DEPTH, DEC_BATCH, CONV_W - 1, 3 * BRANCH_W), F32),
        'state_ret': nrm(k[7], (DEPTH, DEC_BATCH, N_HEADS, HEAD_DIM, HEAD_DIM), F32),
        'page_table': page_table,
        'c_prompt': nrm(k[8], (BATCH, D_MODEL), F32),
        'c_sample': nrm(k[9], (DEC_BATCH, D_MODEL), F32),
        'norm_in': 1.0 + 0.02 * nrm(k[10], (DEPTH, D_MODEL), F32),
        'w_ada': nrm(k[11], (DEPTH, D_MODEL, 3 * D_MODEL), F32) * D_MODEL ** -0.5,
        'b_ada': 0.02 * nrm(k[12], (DEPTH, 3 * D_MODEL), F32),
        'w_in': nrm(k[13], (DEPTH, D_MODEL, IN_W), F32) * D_MODEL ** -0.5,
        'conv_w': nrm(k[14], (DEPTH, CONV_W, 3 * BRANCH_W), F32) * CONV_W ** -0.5,
        'a_log': jnp.log(jax.random.uniform(k[15], (DEPTH, N_HEADS), F32, 1.0, 16.0)),
        'dt_bias': jnp.log(jnp.expm1(dt)),
        'norm_a': 1.0 + 0.02 * nrm(k[17], (DEPTH, HEAD_DIM), F32),
        'norm_c': 1.0 + 0.02 * nrm(k[18], (DEPTH, HEAD_DIM), F32),
        'w_branch': nrm(k[19], (DEPTH, N_BRANCH, BRANCH_W, D_MODEL), F32) * BRANCH_W ** -0.5,
        'w_out': nrm(k[20], (DEPTH, D_MODEL, D_MODEL), F32) * D_MODEL ** -0.5,
        'norm_f': 1.0 + 0.02 * nrm(k[21], (D_MODEL,), F32),
    }


def reference(x_prompt, x_sample, cache_k, cache_v, state_gdn, state_conv, state_ret, page_table,
              c_prompt, c_sample, norm_in, w_ada, b_ada, w_in, conv_w, a_log, dt_bias, norm_a, norm_c,
              w_branch, w_out, norm_f):
    n_b = x_prompt.shape[0]
    n_db = x_sample.shape[0]
    n_pages = page_table.shape[1]
    past_len = n_pages * PAGE_SIZE
    xp, xs = x_prompt, x_sample
    kp_l, vp_l, ks_l, vs_l = [], [], [], []
    gp_l, gs_l, cp_l, cs_l, rp_l, rs_l = [], [], [], [], [], []
    for l in range(DEPTH):
        lp = (norm_in[l], w_ada[l], b_ada[l], w_in[l], conv_w[l], a_log[l], dt_bias[l],
              norm_a[l], norm_c[l], w_branch[l], w_out[l])
        xp, k_new, v_new, g_new, cv_new, r_new = trunk_layer(
            xp, c_prompt, 0,
            jnp.zeros((n_b, N_HEADS, HEAD_DIM, HEAD_DIM), state_gdn.dtype),
            jnp.zeros((n_b, CONV_W - 1, 3 * BRANCH_W), state_conv.dtype),
            jnp.zeros((n_b, N_HEADS, HEAD_DIM, HEAD_DIM), state_ret.dtype),
            None, None, *lp)
        kp_l.append(k_new); vp_l.append(v_new); gp_l.append(g_new); cp_l.append(cv_new); rp_l.append(r_new)
        k_past = cache_k[l][page_table].reshape(n_db, past_len, N_HEADS, HEAD_DIM)
        v_past = cache_v[l][page_table].reshape(n_db, past_len, N_HEADS, HEAD_DIM)
        xs, k_new, v_new, g_new, cv_new, r_new = trunk_layer(
            xs, c_sample, past_len, state_gdn[l], state_conv[l], state_ret[l], k_past, v_past, *lp)
        ks_l.append(k_new); vs_l.append(v_new); gs_l.append(g_new); cs_l.append(cv_new); rs_l.append(r_new)
    y_prompt = rmsnorm(xp, norm_f)
    y_sample = rmsnorm(xs, norm_f)
    k_prompt = jnp.stack(kp_l)
    v_prompt = jnp.stack(vp_l)
    k_sample = jnp.stack(ks_l)
    v_sample = jnp.stack(vs_l)
    gdn_prompt = jnp.stack(gp_l)
    gdn_sample = jnp.stack(gs_l)
    conv_prompt = jnp.stack(cp_l)
    conv_sample = jnp.stack(cs_l)
    ret_prompt = jnp.stack(rp_l)
    ret_sample = jnp.stack(rs_l)
    return (y_prompt, y_sample, k_prompt, v_prompt, k_sample, v_sample, gdn_prompt, gdn_sample,
            conv_prompt, conv_sample, ret_prompt, ret_sample)
```

```python
import functools
import math

import jax
import jax.numpy as jnp
from jax import lax
from jax.experimental import pallas as pl
from jax.experimental.pallas import tpu as pltpu

F32 = jnp.float32
BF16 = jnp.bfloat16
HI = lax.Precision.HIGHEST

HEAD_DIM = 128
N_HEADS = 4
BRANCH_W = N_HEADS * HEAD_DIM
N_BRANCH = 3
CONV_W = 4
GDN_CHUNK = 64
RET_CHUNK = 64
MOBA_BLOCK = 256
MOBA_TOPK = 3
ROPE_THETA = 500000.0
ROPE_DIMS = HEAD_DIM // 4
RET_THETA = 10000.0
EPS = 1e-6
NEG = -1e30
SUBLANES = 8
QK_SCALE = HEAD_DIM ** -0.5

COL_QKV_A, COL_Z_A, COL_Q_B, COL_K_B, COL_V_B, COL_Z_B = 0, 3, 4, 5, 6, 7
COL_Q_C, COL_K_C, COL_V_C, COL_Z_C, COL_MERGE = 8, 9, 10, 11, 12
MAIN_W = 18 * BRANCH_W
BA_W = 128

NT = (((1,), (1,)), ((), ()))
TN = (((0,), (0,)), ((), ()))

VMEM_LIMIT = 48 * 1024 * 1024


def _params(*sem):
    return pltpu.CompilerParams(dimension_semantics=sem, vmem_limit_bytes=VMEM_LIMIT)


def _silu(x):
    return x * jax.nn.sigmoid(x)


def _bdot(a, b):
    return jnp.dot(a.astype(BF16), b.astype(BF16), preferred_element_type=F32)


def _bdot_g(a, b, dims):
    return lax.dot_general(a.astype(BF16), b.astype(BF16), dims, preferred_element_type=F32)


def _hdot(a, b):
    return jnp.dot(a, b, precision=HI, preferred_element_type=F32)


def _mod_kernel(c_ref, w_ref, b_ref, o_ref):
    o_ref[...] = _bdot(_silu(c_ref[...]), w_ref[...]) + b_ref[...]


def _mod_call(c_all, w_ada, b_ada):
    depth, d, d3 = w_ada.shape
    rows = c_all.shape[0]
    tn = d
    return pl.pallas_call(
        _mod_kernel,
        grid=(depth, d3 // tn),
        in_specs=[pl.BlockSpec((rows, d), lambda l, j: (0, 0)),
                  pl.BlockSpec((None, d, tn), lambda l, j: (l, 0, j)),
                  pl.BlockSpec((None, 1, tn), lambda l, j: (l, 0, j))],
        out_specs=pl.BlockSpec((None, rows, tn), lambda l, j: (l, 0, j)),
        out_shape=jax.ShapeDtypeStruct((depth, rows, d3), F32),
        compiler_params=_params("arbitrary", "arbitrary"),
        name="mod",
    )(c_all, w_ada, b_ada.reshape(depth, 1, d3))


def _inproj_kernel(x_ref, sc_ref, sh_ref, g_ref, w_ref, wba_ref, o_ref, ba_ref, h_scr):
    @pl.when(pl.program_id(1) == 0)
    def _():
        x = x_ref[...]
        y = x * lax.rsqrt(jnp.mean(x * x, axis=-1, keepdims=True) + EPS) * g_ref[...]
        h = (y * (1.0 + sc_ref[...]) + sh_ref[...]).astype(BF16)
        h_scr[...] = h
        ba_ref[...] = jnp.dot(h, wba_ref[...], preferred_element_type=F32)

    o_ref[...] = jnp.dot(h_scr[...], w_ref[...], preferred_element_type=F32)


def _inproj_call(x2d, scale, shift, norm_g, w_main, w_ba, *, tm, rows_per_mod):
    m, d = x2d.shape
    tn = 1024
    if scale.ndim == 3:
        mod_spec = pl.BlockSpec((None, 1, d), lambda i, j: ((i * tm) // rows_per_mod, 0, 0))
    else:
        mod_spec = pl.BlockSpec((tm, d), lambda i, j: (i, 0))
    return pl.pallas_call(
        _inproj_kernel,
        grid=(m // tm, MAIN_W // tn),
        in_specs=[pl.BlockSpec((tm, d), lambda i, j: (i, 0)),
                  mod_spec, mod_spec,
                  pl.BlockSpec((1, d), lambda i, j: (0, 0)),
                  pl.BlockSpec((d, tn), lambda i, j: (0, j)),
                  pl.BlockSpec((d, BA_W), lambda i, j: (0, 0))],
        out_specs=[pl.BlockSpec((tm, tn), lambda i, j: (i, j)),
                   pl.BlockSpec((tm, BA_W), lambda i, j: (i, 0))],
        out_shape=[jax.ShapeDtypeStruct((m, MAIN_W), F32),
                   jax.ShapeDtypeStruct((m, BA_W), F32)],
        scratch_shapes=[pltpu.VMEM((tm, d), BF16)],
        compiler_params=_params("arbitrary", "arbitrary"),
        name="in_proj",
    )(x2d, scale, shift, norm_g.reshape(1, d), w_main, w_ba)


def _rope_b_kernel(q_ref, k_ref, c_ref, s1_ref, s2_ref, qo_ref, ko_ref):
    c, s1, s2 = c_ref[...], s1_ref[...], s2_ref[...]
    half = ROPE_DIMS // 2
    for h in range(N_HEADS):
        sl = slice(h * HEAD_DIM, (h + 1) * HEAD_DIM)
        q = q_ref[:, sl]
        k = k_ref[:, sl]
        qr = q * c + pltpu.roll(q, half, 1) * s1 + pltpu.roll(q, HEAD_DIM - half, 1) * s2
        kr = k * c + pltpu.roll(k, half, 1) * s1 + pltpu.roll(k, HEAD_DIM - half, 1) * s2
        qo_ref[:, sl] = qr * QK_SCALE
        ko_ref[:, sl] = kr


def _rope_b_call(proj, tabs, *, tr, t_rows):
    m = proj.shape[0]
    nt = t_rows // tr
    tab_spec = pl.BlockSpec((tr, HEAD_DIM), lambda i: (i % nt, 0))
    return pl.pallas_call(
        _rope_b_kernel,
        grid=(m // tr,),
        in_specs=[pl.BlockSpec((tr, BRANCH_W), lambda i: (i, COL_Q_B)),
                  pl.BlockSpec((tr, BRANCH_W), lambda i: (i, COL_K_B)),
                  tab_spec, tab_spec, tab_spec],
        out_specs=[pl.BlockSpec((tr, BRANCH_W), lambda i: (i, 0)),
                   pl.BlockSpec((tr, BRANCH_W), lambda i: (i, 0))],
        out_shape=[jax.ShapeDtypeStruct((m, BRANCH_W), F32),
                   jax.ShapeDtypeStruct((m, BRANCH_W), F32)],
        compiler_params=_params("arbitrary"),
        name="rope_b",
    )(proj, proj, *tabs)


def _gdn_kernel(qkv_ref, hist_ref, ba_ref, z_ref, s0_ref, cw_ref, alog_ref, dtb_ref, na_ref,
                y_ref, sout_ref, s_scr, tail_scr, *, C, n_valid):
    c_idx = pl.program_id(1)

    @pl.when(c_idx == 0)
    def _():
        s_scr[...] = s0_ref[...]
        tail_scr[...] = hist_ref[...]

    u = qkv_ref[...]
    prev = tail_scr[...]
    w = cw_ref[...]
    acc = u * w[CONV_W - 1:CONV_W, :]
    row8 = lax.broadcasted_iota(jnp.int32, (SUBLANES, u.shape[1]), 0)
    for s in range(1, CONV_W):
        rolled = pltpu.roll(u, s, 0)
        first = jnp.where(row8 < s, pltpu.roll(prev, s, 0), rolled[:SUBLANES])
        shifted = first if C == SUBLANES else jnp.concatenate([first, rolled[SUBLANES:]], axis=0)
        acc = acc + shifted * w[CONV_W - 1 - s:CONV_W - s, :]
    tail_scr[...] = u[C - SUBLANES:, :]
    conv = _silu(acc)

    ba = ba_ref[...]
    ri = lax.broadcasted_iota(jnp.int32, (C, C), 0)
    ci = lax.broadcasted_iota(jnp.int32, (C, C), 1)
    tri = ri >= ci
    strict = ri > ci
    ltri = tri.astype(F32)
    utri = (ri <= ci).astype(F32)
    eye = (ri == ci).astype(F32)
    ones = jnp.ones((C, C), F32)
    valid = lax.broadcasted_iota(jnp.int32, (C, 1), 0) < n_valid
    n_levels = max(1, math.ceil(math.log2(n_valid)))

    for h in range(N_HEADS):
        sl = slice(h * HEAD_DIM, (h + 1) * HEAD_DIM)
        q = conv[:, h * HEAD_DIM:(h + 1) * HEAD_DIM]
        k = conv[:, BRANCH_W + h * HEAD_DIM:BRANCH_W + (h + 1) * HEAD_DIM]
        v = conv[:, 2 * BRANCH_W + h * HEAD_DIM:2 * BRANCH_W + (h + 1) * HEAD_DIM]
        qn = q * lax.rsqrt(jnp.sum(q * q, axis=-1, keepdims=True) + EPS) * QK_SCALE
        kn = k * lax.rsqrt(jnp.sum(k * k, axis=-1, keepdims=True) + EPS)
        beta = jax.nn.sigmoid(ba[:, h:h + 1])
        xg = ba[:, N_HEADS + h:N_HEADS + h + 1] + dtb_ref[0:1, h:h + 1]
        softplus = jnp.maximum(xg, 0.0) + jnp.log1p(jnp.exp(-jnp.abs(xg)))
        g = -jnp.exp(alog_ref[0:1, h:h + 1]) * softplus
        if n_valid < C:
            beta = jnp.where(valid, beta, 0.0)
            g = jnp.where(valid, g, 0.0)
        g_b = jnp.broadcast_to(g, (C, HEAD_DIM))
        beta_b = jnp.broadcast_to(beta, (C, HEAD_DIM))

        gc_b = _hdot(ltri, g_b)
        gc_row = _hdot(ones, g_b[:, :C] * utri)
        diff = gc_b[:, :C] - gc_row
        decay = jnp.where(tri, jnp.exp(jnp.where(tri, diff, 0.0)), 0.0)
        kb = kn * beta_b
        a = -jnp.where(strict, _bdot_g(kb, kn, NT) * decay, 0.0)
        inv = eye + a
        apow = a
        for _ in range(n_levels - 1):
            apow = _hdot(apow, apow)
            inv = inv + _hdot(inv, apow)
        eg = jnp.exp(gc_b)
        u_s = _hdot(inv, v * beta_b)
        w_s = _hdot(inv, kb * eg)
        attn = _bdot_g(qn, kn, NT) * decay
        g_last = gc_b[C - 1:C, :]
        k_dec = kn * jnp.exp(g_last - gc_b)

        st = s_scr[h]
        v_new = u_s - _bdot(w_s, st)
        o = _bdot(qn * eg, st) + _bdot(attn, v_new)
        s_scr[h] = st * jnp.exp(g_last) + _bdot_g(k_dec, v_new, TN)

        on = o * lax.rsqrt(jnp.mean(o * o, axis=-1, keepdims=True) + EPS) * na_ref[...]
        y_ref[:, sl] = on * _silu(z_ref[:, sl])

    @pl.when(c_idx == pl.num_programs(1) - 1)
    def _():
        sout_ref[...] = s_scr[...]


def _gdn_call(proj, ba, hist, s0, conv_w, alog, dtb, norm_a, *, n_b, t_rows, C, n_valid):
    n_c = t_rows // C
    row = lambda b, c: b * n_c + c
    vec_spec = pl.BlockSpec((1, HEAD_DIM), lambda b, c: (0, 0))
    state_spec = pl.BlockSpec((None, N_HEADS, HEAD_DIM, HEAD_DIM), lambda b, c: (b, 0, 0, 0))
    return pl.pallas_call(
        functools.partial(_gdn_kernel, C=C, n_valid=n_valid),
        grid=(n_b, n_c),
        in_specs=[pl.BlockSpec((C, 3 * BRANCH_W), lambda b, c: (row(b, c), 0)),
                  pl.BlockSpec((None, SUBLANES, 3 * BRANCH_W), lambda b, c: (b, 0, 0)),
                  pl.BlockSpec((C, BA_W), lambda b, c: (row(b, c), 0)),
                  pl.BlockSpec((C, BRANCH_W), lambda b, c: (row(b, c), COL_Z_A)),
                  state_spec,
                  pl.BlockSpec((CONV_W, 3 * BRANCH_W), lambda b, c: (0, 0)),
                  vec_spec, vec_spec, vec_spec],
        out_specs=[pl.BlockSpec((C, BRANCH_W), lambda b, c: (row(b, c), 0)), state_spec],
        out_shape=[jax.ShapeDtypeStruct((n_b * t_rows, BRANCH_W), F32),
                   jax.ShapeDtypeStruct((n_b, N_HEADS, HEAD_DIM, HEAD_DIM), F32)],
        scratch_shapes=[pltpu.VMEM((N_HEADS, HEAD_DIM, HEAD_DIM), F32),
                        pltpu.VMEM((SUBLANES, 3 * BRANCH_W), F32)],
        compiler_params=_params("arbitrary", "arbitrary"),
        name="gdn",
    )(proj, hist, ba, proj, s0, conv_w, alog, dtb, norm_a)


def _ret_kernel(q_ref, k_ref, v_ref, z_ref, cos_ref, sin_ref, s0_ref, dmat_ref, qdec_ref, kdec_ref,
                cdec_ref, nc_ref, y_ref, sout_ref, s_scr):
    c_idx = pl.program_id(1)

    @pl.when(c_idx == 0)
    def _():
        s_scr[...] = s0_ref[...]

    cos, sin = cos_ref[...], sin_ref[...]
    for h in range(N_HEADS):
        sl = slice(h * HEAD_DIM, (h + 1) * HEAD_DIM)
        q = q_ref[:, sl]
        k = k_ref[:, sl]
        v = v_ref[:, sl]
        qr = q * cos + pltpu.roll(q, HEAD_DIM // 2, 1) * sin
        kr = (k * cos + pltpu.roll(k, HEAD_DIM // 2, 1) * sin) * QK_SCALE
        o_intra = _bdot(_bdot_g(qr, kr, NT) * dmat_ref[h], v)
        kv = _bdot_g(kr * kdec_ref[h], v, TN)
        st = s_scr[h]
        o = o_intra + _bdot(qr * qdec_ref[h], st)
        s_scr[h] = st * cdec_ref[h] + kv
        on = o * lax.rsqrt(jnp.mean(o * o, axis=-1, keepdims=True) + EPS) * nc_ref[...]
        y_ref[:, sl] = on * _silu(z_ref[:, sl])

    @pl.when(c_idx == pl.num_programs(1) - 1)
    def _():
        sout_ref[...] = s_scr[...]


def _ret_tables(C, n_valid):
    log_g = jnp.log1p(-(2.0 ** (-5.0 - jnp.arange(N_HEADS, dtype=F32))))
    idx = jnp.arange(C, dtype=F32)
    tri = jnp.tril(jnp.ones((C, C), dtype=bool))
    dmat = jnp.where(tri, jnp.exp(log_g[:, None, None] * jnp.where(tri, idx[:, None] - idx[None, :], 0.0)), 0.0)
    q_dec = jnp.exp(log_g[:, None] * (idx + 1.0))[..., None]
    k_dec = jnp.exp(log_g[:, None] * (n_valid - 1.0 - idx))[..., None]
    c_dec = jnp.exp(log_g * n_valid)[:, None, None]
    bc = lambda t, r: jnp.broadcast_to(t, (N_HEADS, r, HEAD_DIM))
    return dmat, bc(q_dec, C), bc(k_dec, C), bc(c_dec, 1)


def _ret_call(proj, cos2, sin2, s0, norm_c, *, n_b, t_rows, C, n_valid):
    n_c = t_rows // C
    row = lambda b, c: b * n_c + c
    dmat, q_dec, k_dec, c_dec = _ret_tables(C, n_valid)
    col_spec = lambda col: pl.BlockSpec((C, BRANCH_W), lambda b, c: (row(b, c), col))
    tab_spec = pl.BlockSpec((C, HEAD_DIM), lambda b, c: (c, 0))
    state_spec = pl.BlockSpec((None, N_HEADS, HEAD_DIM, HEAD_DIM), lambda b, c: (b, 0, 0, 0))
    const_spec = lambda r, w: pl.BlockSpec((N_HEADS, r, w), lambda b, c: (0, 0, 0))
    return pl.pallas_call(
        _ret_kernel,
        grid=(n_b, n_c),
        in_specs=[col_spec(COL_Q_C), col_spec(COL_K_C), col_spec(COL_V_C), col_spec(COL_Z_C),
                  tab_spec, tab_spec, state_spec,
                  const_spec(C, C), const_spec(C, HEAD_DIM), const_spec(C, HEAD_DIM), const_spec(1, HEAD_DIM),
                  pl.BlockSpec((1, HEAD_DIM), lambda b, c: (0, 0))],
        out_specs=[pl.BlockSpec((C, BRANCH_W), lambda b, c: (row(b, c), 0)), state_spec],
        out_shape=[jax.ShapeDtypeStruct((n_b * t_rows, BRANCH_W), F32),
                   jax.ShapeDtypeStruct((n_b, N_HEADS, HEAD_DIM, HEAD_DIM), F32)],
        scratch_shapes=[pltpu.VMEM((N_HEADS, HEAD_DIM, HEAD_DIM), F32)],
        compiler_params=_params("arbitrary", "arbitrary"),
        name="ret",
    )(proj, proj, proj, proj, cos2, sin2, s0, dmat, q_dec, k_dec, c_dec, norm_c)


def _topk_mask(gate, valid):
    n = gate.shape[1]
    col = lax.broadcasted_iota(jnp.int32, gate.shape, 1)
    gm = jnp.where(valid, gate, -jnp.inf)
    rank = jnp.zeros(gate.shape, jnp.int32)
    for m in range(n):
        gmm = gm[:, m:m + 1]
        beats = (gmm > gm) | ((gmm == gm) & (col > m))
        rank = rank + jnp.where(beats, 1, 0)
    return valid & (rank < MOBA_TOPK)


def _moba_p_kernel(q_ref, k_ref, v_ref, z_ref, y_ref, kmean_scr, bias_scr, *, n_blk):
    qi = pl.program_id(2)
    blk = MOBA_BLOCK

    @pl.when(qi == 0)
    def _():
        for n in range(n_blk):
            kmean_scr[n:n + 1, :] = jnp.sum(k_ref[n * blk:(n + 1) * blk, :], axis=0, keepdims=True) * (1.0 / blk)

    q = q_ref[...]
    gate = lax.dot_general(q, kmean_scr[...], NT, precision=HI, preferred_element_type=F32)
    col = lax.broadcasted_iota(jnp.int32, gate.shape, 1)
    bias = jnp.where(_topk_mask(gate, col < qi), 0.0, NEG)
    for n in range(n_blk):
        bias_scr[n] = jnp.broadcast_to(bias[:, n:n + 1], (blk, HEAD_DIM))

    qb = q.astype(BF16)
    own = pl.multiple_of(qi * blk, blk)
    s = lax.dot_general(qb, k_ref[pl.ds(own, blk), :].astype(BF16), NT, preferred_element_type=F32)
    ri = lax.broadcasted_iota(jnp.int32, (blk, blk), 0)
    ci = lax.broadcasted_iota(jnp.int32, (blk, blk), 1)
    s = jnp.where(ci <= ri, s, NEG)
    m0 = jnp.max(s, axis=1, keepdims=True)
    p = jnp.exp(s - m0)
    l0 = jnp.sum(p, axis=1, keepdims=True)
    acc0 = _bdot(p, v_ref[pl.ds(own, blk), :])

    def body(n, carry):
        m, l, acc = carry
        start = pl.multiple_of(n * blk, blk)
        s = lax.dot_general(qb, k_ref[pl.ds(start, blk), :].astype(BF16), NT, preferred_element_type=F32)
        b = bias_scr[n]
        s = s + jnp.concatenate([b] * (blk // HEAD_DIM), axis=1)
        m_new = jnp.maximum(m, jnp.max(s, axis=1, keepdims=True))
        p = jnp.exp(s - m_new)
        alpha = jnp.exp(m - m_new)
        l = alpha * l + jnp.sum(p, axis=1, keepdims=True)
        acc = alpha * acc + _bdot(p, v_ref[pl.ds(start, blk), :])
        return m_new, l, acc

    _, l, acc = lax.fori_loop(0, qi, body, (m0, l0, acc0))
    y_ref[...] = (acc / l) * _silu(z_ref[...])


def _moba_p_call(q_rot, k_rot, proj, *, n_b, t_rows):
    n_blk = t_rows // MOBA_BLOCK
    assert t_rows % MOBA_BLOCK == 0 and n_blk >= MOBA_TOPK
    blk = MOBA_BLOCK
    return pl.pallas_call(
        functools.partial(_moba_p_kernel, n_blk=n_blk),
        grid=(n_b, N_HEADS, n_blk),
        in_specs=[pl.BlockSpec((blk, HEAD_DIM), lambda b, h, i: (b * n_blk + i, h)),
                  pl.BlockSpec((t_rows, HEAD_DIM), lambda b, h, i: (b, h)),
                  pl.BlockSpec((t_rows, HEAD_DIM), lambda b, h, i: (b, COL_V_B * N_HEADS + h)),
                  pl.BlockSpec((blk, HEAD_DIM), lambda b, h, i: (b * n_blk + i, COL_Z_B * N_HEADS + h))],
        out_specs=pl.BlockSpec((blk, HEAD_DIM), lambda b, h, i: (b * n_blk + i, h)),
        out_shape=jax.ShapeDtypeStruct((n_b * t_rows, BRANCH_W), F32),
        scratch_shapes=[pltpu.VMEM((n_blk, HEAD_DIM), F32),
                        pltpu.VMEM((n_blk, blk, HEAD_DIM), F32)],
        compiler_params=_params("arbitrary", "arbitrary", "arbitrary"),
        name="moba_p",
    )(q_rot, k_rot, proj, proj)


def _head_diag(x, row_head):
    out = jnp.zeros((x.shape[0], HEAD_DIM), F32)
    for h in range(N_HEADS):
        out = jnp.where(row_head == h, x[:, h * HEAD_DIM:(h + 1) * HEAD_DIM], out)
    return out


def _moba_s_kernel(pt_ref, q_ref, kn_ref, vn_ref, z_ref, kp_ref, vp_ref, y_ref,
                   qbd_scr, ksum_scr, m_scr, l_scr, o_scr, *, n_valid, pages_per_blk):
    del pt_ref
    p_idx = pl.program_id(1)
    n_pages = pl.num_programs(1)
    rows = N_HEADS * SUBLANES
    row_head = lax.broadcasted_iota(jnp.int32, (rows, HEAD_DIM), 0) // SUBLANES

    @pl.when(p_idx == 0)
    def _():
        q32 = jnp.concatenate([q_ref[...]] * N_HEADS, axis=0)
        rh = lax.broadcasted_iota(jnp.int32, q32.shape, 0) // SUBLANES
        ch = lax.broadcasted_iota(jnp.int32, q32.shape, 1) // HEAD_DIM
        qbd_scr[...] = jnp.where(rh == ch, q32, 0.0)

    qbd = qbd_scr[...]
    kp = kp_ref[...]
    s = _bdot_g(qbd, kp, NT)
    m = jnp.max(s, axis=1, keepdims=True)
    e = jnp.exp(s - m)
    m_scr[p_idx] = jnp.broadcast_to(m, (rows, HEAD_DIM))
    l_scr[p_idx] = jnp.broadcast_to(jnp.sum(e, axis=1, keepdims=True), (rows, HEAD_DIM))
    o_scr[p_idx] = _head_diag(_bdot(e, vp_ref[...]), row_head)
    ksum = jnp.sum(kp, axis=0, keepdims=True)
    blk_row = pl.ds(p_idx // pages_per_blk, 1)

    @pl.when(p_idx % pages_per_blk == 0)
    def _():
        ksum_scr[blk_row, :] = ksum

    @pl.when(p_idx % pages_per_blk != 0)
    def _():
        ksum_scr[blk_row, :] = ksum_scr[blk_row, :] + ksum

    @pl.when(p_idx == n_pages - 1)
    def _():
        n_pg = m_scr.shape[0]
        n_blk = n_pg // pages_per_blk
        kmean = ksum_scr[...] * (1.0 / MOBA_BLOCK)
        gate = lax.dot_general(qbd, kmean, NT, precision=HI, preferred_element_type=F32)
        sel_f = jnp.where(_topk_mask(gate, jnp.ones(gate.shape, jnp.bool_)), 1.0, 0.0)
        selw = [jnp.broadcast_to(sel_f[:, n:n + 1], (rows, HEAD_DIM)) > 0.5 for n in range(n_blk)]

        s_own = _bdot_g(qbd, kn_ref[...], NT)
        rq = lax.broadcasted_iota(jnp.int32, s_own.shape, 0) % SUBLANES
        cj = lax.broadcasted_iota(jnp.int32, s_own.shape, 1)
        own_ok = (cj <= rq) & (cj < n_valid)
        s_own = jnp.where(own_ok, s_own, NEG)
        mx = jnp.broadcast_to(jnp.max(s_own, axis=1, keepdims=True), (rows, HEAD_DIM))
        for pg in range(n_pg):
            mx = jnp.maximum(mx, jnp.where(selw[pg // pages_per_blk], m_scr[pg], NEG))
        e_own = jnp.where(own_ok, jnp.exp(s_own - mx[:, :SUBLANES]), 0.0)
        l_tot = jnp.broadcast_to(jnp.sum(e_own, axis=1, keepdims=True), (rows, HEAD_DIM))
        o_tot = _head_diag(_bdot(e_own, vn_ref[...]), row_head)
        for pg in range(n_pg):
            wgt = jnp.where(selw[pg // pages_per_blk], jnp.exp(jnp.minimum(m_scr[pg] - mx, 0.0)), 0.0)
            l_tot = l_tot + wgt * l_scr[pg]
            o_tot = o_tot + wgt * o_scr[pg]
        o = o_tot / l_tot
        o_b = jnp.concatenate([o[h * SUBLANES:(h + 1) * SUBLANES] for h in range(N_HEADS)], axis=1)
        y_ref[...] = o_b * _silu(z_ref[...])


def _moba_s_call(page_table, q_rot, k_rot, proj, cache_k, cache_v, layer, *, n_valid):
    n_b, n_pages = page_table.shape
    depth, n_pool, page = cache_k.shape[:3]
    assert MOBA_BLOCK % page == 0
    pages_per_blk = MOBA_BLOCK // page
    assert n_pages % pages_per_blk == 0 and n_pages // pages_per_blk >= MOBA_TOPK
    ck = cache_k.reshape(depth, n_pool, page, BRANCH_W)
    cv = cache_v.reshape(depth, n_pool, page, BRANCH_W)
    rows = N_HEADS * SUBLANES
    row_spec = lambda col: pl.BlockSpec((SUBLANES, BRANCH_W), lambda b, p, pt: (b, col))
    page_spec = pl.BlockSpec((None, None, page, BRANCH_W), lambda b, p, pt: (layer, pt[b, p], 0, 0))
    grid_spec = pltpu.PrefetchScalarGridSpec(
        num_scalar_prefetch=1,
        grid=(n_b, n_pages),
        in_specs=[row_spec(0), row_spec(0), row_spec(COL_V_B), row_spec(COL_Z_B), page_spec, page_spec],
        out_specs=pl.BlockSpec((SUBLANES, BRANCH_W), lambda b, p, pt: (b, 0)),
        scratch_shapes=[pltpu.VMEM((rows, BRANCH_W), F32),
                        pltpu.VMEM((n_pages // pages_per_blk, BRANCH_W), F32),
                        pltpu.VMEM((n_pages, rows, HEAD_DIM), F32),
                        pltpu.VMEM((n_pages, rows, HEAD_DIM), F32),
                        pltpu.VMEM((n_pages, rows, HEAD_DIM), F32)])
    return pl.pallas_call(
        functools.partial(_moba_s_kernel, n_valid=n_valid, pages_per_blk=pages_per_blk),
        grid_spec=grid_spec,
        out_shape=jax.ShapeDtypeStruct((n_b * SUBLANES, BRANCH_W), F32),
        compiler_params=_params("arbitrary", "arbitrary"),
        name="moba_s",
    )(page_table, q_rot, k_rot, proj, proj, ck, cv)


def _merge_kernel(ya_ref, yb_ref, yc_ref, mg_ref, x_ref, gate_ref, wb_ref, wo_ref, nf_ref, *out_refs, final):
    d = x_ref.shape[1]
    mixed = None
    for n, y_ref in enumerate((ya_ref, yb_ref, yc_ref)):
        per_branch = _bdot(y_ref[...], wb_ref[n])
        term = jax.nn.sigmoid(mg_ref[:, n * d:(n + 1) * d]) * per_branch
        mixed = term if mixed is None else mixed + term
    x_out = x_ref[...] + gate_ref[...] * _bdot(mixed, wo_ref[...])
    out_refs[0][...] = x_out
    if final:
        out_refs[1][...] = (x_out * lax.rsqrt(jnp.mean(x_out * x_out, axis=-1, keepdims=True) + EPS)
                            * nf_ref[...])


def _merge_call(y_a, y_b, y_c, proj, x2d, gate, w_branch, w_out, norm_f, *, tm, rows_per_mod, final):
    m, d = x2d.shape
    if gate.ndim == 3:
        gate_spec = pl.BlockSpec((None, 1, d), lambda i: ((i * tm) // rows_per_mod, 0, 0))
    else:
        gate_spec = pl.BlockSpec((tm, d), lambda i: (i, 0))
    y_spec = pl.BlockSpec((tm, BRANCH_W), lambda i: (i, 0))
    x_spec = pl.BlockSpec((tm, d), lambda i: (i, 0))
    n_out = 2 if final else 1
    outs = pl.pallas_call(
        functools.partial(_merge_kernel, final=final),
        grid=(m // tm,),
        in_specs=[y_spec, y_spec, y_spec,
                  pl.BlockSpec((tm, N_BRANCH * d), lambda i: (i, (COL_MERGE * BRANCH_W) // (N_BRANCH * d))),
                  x_spec, gate_spec,
                  pl.BlockSpec((N_BRANCH, BRANCH_W, d), lambda i: (0, 0, 0)),
                  pl.BlockSpec((d, d), lambda i: (0, 0)),
                  pl.BlockSpec((1, d), lambda i: (0, 0))],
        out_specs=[x_spec] * n_out,
        out_shape=[jax.ShapeDtypeStruct((m, d), F32)] * n_out,
        compiler_params=_params("arbitrary"),
        name="merge",
    )(y_a, y_b, y_c, proj, x2d, gate, w_branch, w_out, norm_f.reshape(1, d))
    return outs


def _rope_tables(pos):
    t = pos.shape[0]
    posf = pos.astype(F32)

    def cos_sin(n_rot, theta):
        half = n_rot // 2
        inv = theta ** (-jnp.arange(half, dtype=F32) / half)
        ang = posf[:, None] * inv[None, :]
        return jnp.cos(ang), jnp.sin(ang)

    cb, sb = cos_sin(ROPE_DIMS, ROPE_THETA)
    hb = ROPE_DIMS // 2
    tab_b = (jnp.concatenate([cb, cb, jnp.ones((t, HEAD_DIM - ROPE_DIMS), F32)], axis=1),
             jnp.concatenate([jnp.zeros((t, hb), F32), sb, jnp.zeros((t, HEAD_DIM - ROPE_DIMS), F32)], axis=1),
             jnp.concatenate([-sb, jnp.zeros((t, HEAD_DIM - hb), F32)], axis=1))
    cc, sc = cos_sin(HEAD_DIM, RET_THETA)
    tab_c = (jnp.concatenate([cc, cc], axis=1), jnp.concatenate([-sc, sc], axis=1))
    return tab_b, tab_c


def _pad_rows(a2d, n_b, t, t_pad):
    w = a2d.shape[1]
    return jnp.pad(a2d.reshape(n_b, t, w), ((0, 0), (0, t_pad - t), (0, 0))).reshape(n_b * t_pad, w)


def _vec128(v):
    return jnp.pad(v.astype(F32), (0, HEAD_DIM - v.shape[0])).reshape(1, HEAD_DIM)


def kernel(x_prompt, x_sample, cache_k, cache_v, state_gdn, state_conv, state_ret, page_table, c_prompt, c_sample,
           norm_in, w_ada, b_ada, w_in, conv_w, a_log, dt_bias, norm_a, norm_c, w_branch, w_out, norm_f):
    n_b, seq, d = x_prompt.shape
    n_db, dec_seq, _ = x_sample.shape
    depth = w_in.shape[0]
    n_pages = page_table.shape[1]
    past_len = n_pages * cache_k.shape[2]
    assert d == 2 * BRANCH_W and dec_seq <= SUBLANES and dec_seq >= CONV_W - 1
    assert seq % GDN_CHUNK == 0 and seq % RET_CHUNK == 0
    t_pad = SUBLANES

    ba0 = 4 * BRANCH_W
    w_main = jnp.concatenate([w_in[:, :, :ba0], w_in[:, :, ba0 + 2 * N_HEADS:]], axis=2).astype(BF16)
    w_ba = jnp.pad(w_in[:, :, ba0:ba0 + 2 * N_HEADS], ((0, 0), (0, 0), (0, BA_W - 2 * N_HEADS))).astype(BF16)
    w_ada_b = w_ada.astype(BF16)
    w_branch_b = w_branch.astype(BF16)
    w_out_b = w_out.astype(BF16)

    n_c = n_b + n_db
    c_rows = -(-n_c // SUBLANES) * SUBLANES
    c_all = jnp.pad(jnp.concatenate([c_prompt, c_sample], axis=0), ((0, c_rows - n_c), (0, 0)))
    mods = _mod_call(c_all, w_ada_b, b_ada)

    tab_b_p, tab_c_p = _rope_tables(jnp.arange(seq, dtype=jnp.int32))
    tab_b_s, tab_c_s = _rope_tables(past_len + jnp.arange(t_pad, dtype=jnp.int32))

    zeros_state = jnp.zeros((n_b, N_HEADS, HEAD_DIM, HEAD_DIM), F32)
    zeros_hist = jnp.zeros((n_b, SUBLANES, 3 * BRANCH_W), F32)

    xp = x_prompt.reshape(n_b * seq, d)
    xs = x_sample.reshape(n_db * dec_seq, d)
    outs = {k: [] for k in ("kp", "vp", "ks", "vs", "gp", "gs", "cp", "cs", "rp", "rs")}
    y_p = y_s = None
    for l in range(depth):
        final = l == depth - 1
        alog, dtb = _vec128(a_log[l]), _vec128(dt_bias[l])
        na, nc = norm_a[l].reshape(1, HEAD_DIM), norm_c[l].reshape(1, HEAD_DIM)
        shift, scale, gate = jnp.split(mods[l], 3, axis=-1)

        mod_p = [t[:n_b].reshape(n_b, 1, d) for t in (scale, shift, gate)]
        proj, ba = _inproj_call(xp, mod_p[0], mod_p[1], norm_in[l], w_main[l], w_ba[l],
                                tm=min(seq, 1024), rows_per_mod=seq)
        q_rot, k_rot = _rope_b_call(proj, tab_b_p, tr=min(seq, 512), t_rows=seq)
        y_a, gdn_new = _gdn_call(proj, ba, zeros_hist, zeros_state, conv_w[l], alog, dtb, na,
                                 n_b=n_b, t_rows=seq, C=GDN_CHUNK, n_valid=GDN_CHUNK)
        y_c, ret_new = _ret_call(proj, tab_c_p[0], tab_c_p[1], zeros_state, nc,
                                 n_b=n_b, t_rows=seq, C=RET_CHUNK, n_valid=RET_CHUNK)
        y_b = _moba_p_call(q_rot, k_rot, proj, n_b=n_b, t_rows=seq)
        res = _merge_call(y_a, y_b, y_c, proj, xp, mod_p[2], w_branch_b[l], w_out_b[l], norm_f,
                          tm=min(seq, 256), rows_per_mod=seq, final=final)
        xp = res[0]
        if final:
            y_p = res[1]
        proj3 = proj.reshape(n_b, seq, MAIN_W)
        outs["kp"].append(k_rot.reshape(n_b, seq, N_HEADS, HEAD_DIM))
        outs["vp"].append(proj3[:, :, COL_V_B * BRANCH_W:(COL_V_B + 1) * BRANCH_W].reshape(n_b, seq, N_HEADS, HEAD_DIM))
        outs["gp"].append(gdn_new)
        outs["cp"].append(proj3[:, seq - (CONV_W - 1):, :3 * BRANCH_W])
        outs["rp"].append(ret_new)

        mod_s = [jnp.repeat(t[n_b:n_c], dec_seq, axis=0) for t in (scale, shift, gate)]
        proj_s, ba_s = _inproj_call(xs, mod_s[0], mod_s[1], norm_in[l], w_main[l], w_ba[l],
                                    tm=n_db * dec_seq, rows_per_mod=dec_seq)
        proj_sp = _pad_rows(proj_s, n_db, dec_seq, t_pad)
        ba_sp = _pad_rows(ba_s, n_db, dec_seq, t_pad)
        hist = jnp.pad(state_conv[l], ((0, 0), (SUBLANES - (CONV_W - 1), 0), (0, 0)))
        q_rot_s, k_rot_s = _rope_b_call(proj_sp, tab_b_s, tr=t_pad, t_rows=t_pad)
        y_a_s, gdn_new_s = _gdn_call(proj_sp, ba_sp, hist, state_gdn[l], conv_w[l], alog, dtb, na,
                                     n_b=n_db, t_rows=t_pad, C=t_pad, n_valid=dec_seq)
        y_c_s, ret_new_s = _ret_call(proj_sp, tab_c_s[0], tab_c_s[1], state_ret[l], nc,
                                     n_b=n_db, t_rows=t_pad, C=t_pad, n_valid=dec_seq)
        y_b_s = _moba_s_call(page_table, q_rot_s, k_rot_s, proj_sp, cache_k, cache_v, l, n_valid=dec_seq)
        unpad = lambda y: y.reshape(n_db, t_pad, BRANCH_W)[:, :dec_seq].reshape(n_db * dec_seq, BRANCH_W)
        res_s = _merge_call(unpad(y_a_s), unpad(y_b_s), unpad(y_c_s), proj_s, xs, mod_s[2],
                            w_branch_b[l], w_out_b[l], norm_f, tm=n_db * dec_seq, rows_per_mod=dec_seq, final=final)
        xs = res_s[0]
        if final:
            y_s = res_s[1]
        proj_s3 = proj_s.reshape(n_db, dec_seq, MAIN_W)
        outs["ks"].append(k_rot_s.reshape(n_db, t_pad, N_HEADS, HEAD_DIM)[:, :dec_seq])
        outs["vs"].append(proj_s3[:, :, COL_V_B * BRANCH_W:(COL_V_B + 1) * BRANCH_W]
                          .reshape(n_db, dec_seq, N_HEADS, HEAD_DIM))
        outs["gs"].append(gdn_new_s)
        outs["cs"].append(proj_s3[:, dec_seq - (CONV_W - 1):, :3 * BRANCH_W])
        outs["rs"].append(ret_new_s)

    st = {k: jnp.stack(v) for k, v in outs.items()}
    return (y_p.reshape(n_b, seq, d), y_s.reshape(n_db, dec_seq, d),
            st["kp"], st["vp"], st["ks"], st["vs"], st["gp"], st["gs"],
            st["cp"], st["cs"], st["rp"], st["rs"])
```

```python
import functools
import math

import jax
import jax.numpy as jnp
from jax import lax
from jax.experimental import pallas as pl
from jax.experimental.pallas import tpu as pltpu

F32 = jnp.float32
BF16 = jnp.bfloat16
HI = lax.Precision.HIGHEST

HEAD_DIM = 128
N_HEADS = 4
BRANCH_W = N_HEADS * HEAD_DIM
N_BRANCH = 3
CONV_W = 4
GDN_CHUNK = 64
RET_CHUNK = 64
MOBA_BLOCK = 256
MOBA_TOPK = 3
ROPE_THETA = 500000.0
ROPE_DIMS = HEAD_DIM // 4
RET_THETA = 10000.0
EPS = 1e-6
NEG = -1e30
SUBLANES = 8
QK_SCALE = HEAD_DIM ** -0.5

COL_QKV_A, COL_Z_A, COL_Q_B, COL_K_B, COL_V_B, COL_Z_B = 0, 3, 4, 5, 6, 7
COL_Q_C, COL_K_C, COL_V_C, COL_Z_C, COL_MERGE = 8, 9, 10, 11, 12
MAIN_W = 18 * BRANCH_W
BA_W = 128

NT = (((1,), (1,)), ((), ()))
TN = (((0,), (0,)), ((), ()))

VMEM_LIMIT = 48 * 1024 * 1024


def _params(*sem):
    return pltpu.CompilerParams(dimension_semantics=sem, vmem_limit_bytes=VMEM_LIMIT)


def _silu(x):
    return x * jax.nn.sigmoid(x)


def _bdot(a, b):
    return jnp.dot(a.astype(BF16), b.astype(BF16), preferred_element_type=F32)


def _bdot_g(a, b, dims):
    return lax.dot_general(a.astype(BF16), b.astype(BF16), dims, preferred_element_type=F32)


def _hdot(a, b):
    return jnp.dot(a, b, precision=HI, preferred_element_type=F32)


def _mod_kernel(c_ref, w_ref, b_ref, o_ref):
    o_ref[...] = _bdot(_silu(c_ref[...]), w_ref[...]) + b_ref[...]


def _mod_call(c_all, w_ada, b_ada):
    depth, d, d3 = w_ada.shape
    rows = c_all.shape[0]
    tn = d
    return pl.pallas_call(
        _mod_kernel,
        grid=(depth, d3 // tn),
        in_specs=[pl.BlockSpec((rows, d), lambda l, j: (0, 0)),
                  pl.BlockSpec((None, d, tn), lambda l, j: (l, 0, j)),
                  pl.BlockSpec((None, 1, tn), lambda l, j: (l, 0, j))],
        out_specs=pl.BlockSpec((None, rows, tn), lambda l, j: (l, 0, j)),
        out_shape=jax.ShapeDtypeStruct((depth, rows, d3), F32),
        compiler_params=_params("arbitrary", "arbitrary"),
        name="mod",
    )(c_all, w_ada, b_ada.reshape(depth, 1, d3))


def _inproj_kernel(x_ref, sc_ref, sh_ref, g_ref, w_ref, wba_ref, o_ref, ba_ref, h_scr):
    @pl.when(pl.program_id(1) == 0)
    def _():
        x = x_ref[...]
        y = x * lax.rsqrt(jnp.mean(x * x, axis=-1, keepdims=True) + EPS) * g_ref[...]
        h = (y * (1.0 + sc_ref[...]) + sh_ref[...]).astype(BF16)
        h_scr[...] = h
        ba_ref[...] = jnp.dot(h, wba_ref[...], preferred_element_type=F32)

    o_ref[...] = jnp.dot(h_scr[...], w_ref[...], preferred_element_type=F32)


def _inproj_call(x2d, scale, shift, norm_g, w_main, w_ba, *, tm, rows_per_mod):
    m, d = x2d.shape
    tn = 1024
    if scale.ndim == 3:
        mod_spec = pl.BlockSpec((None, 1, d), lambda i, j: ((i * tm) // rows_per_mod, 0, 0))
    else:
        mod_spec = pl.BlockSpec((tm, d), lambda i, j: (i, 0))
    return pl.pallas_call(
        _inproj_kernel,
        grid=(m // tm, MAIN_W // tn),
        in_specs=[pl.BlockSpec((tm, d), lambda i, j: (i, 0)),
                  mod_spec, mod_spec,
                  pl.BlockSpec((1, d), lambda i, j: (0, 0)),
                  pl.BlockSpec((d, tn), lambda i, j: (0, j)),
                  pl.BlockSpec((d, BA_W), lambda i, j: (0, 0))],
        out_specs=[pl.BlockSpec((tm, tn), lambda i, j: (i, j)),
                   pl.BlockSpec((tm, BA_W), lambda i, j: (i, 0))],
        out_shape=[jax.ShapeDtypeStruct((m, MAIN_W), F32),
                   jax.ShapeDtypeStruct((m, BA_W), F32)],
        scratch_shapes=[pltpu.VMEM((tm, d), BF16)],
        compiler_params=_params("arbitrary", "arbitrary"),
        name="in_proj",
    )(x2d, scale, shift, norm_g.reshape(1, d), w_main, w_ba)


def _rope_b_kernel(q_ref, k_ref, c_ref, s1_ref, s2_ref, qo_ref, ko_ref):
    c, s1, s2 = c_ref[...], s1_ref[...], s2_ref[...]
    half = ROPE_DIMS // 2
    for h in range(N_HEADS):
        sl = slice(h * HEAD_DIM, (h + 1) * HEAD_DIM)
        q = q_ref[:, sl]
        k = k_ref[:, sl]
        qr = q * c + pltpu.roll(q, half, 1) * s1 + pltpu.roll(q, HEAD_DIM - half, 1) * s2
        kr = k * c + pltpu.roll(k, half, 1) * s1 + pltpu.roll(k, HEAD_DIM - half, 1) * s2
        qo_ref[:, sl] = qr * QK_SCALE
        ko_ref[:, sl] = kr


def _rope_b_call(proj, tabs, *, tr, t_rows):
    m = proj.shape[0]
    nt = t_rows // tr
    tab_spec = pl.BlockSpec((tr, HEAD_DIM), lambda i: (i % nt, 0))
    return pl.pallas_call(
        _rope_b_kernel,
        grid=(m // tr,),
        in_specs=[pl.BlockSpec((tr, BRANCH_W), lambda i: (i, COL_Q_B)),
                  pl.BlockSpec((tr, BRANCH_W), lambda i: (i, COL_K_B)),
                  tab_spec, tab_spec, tab_spec],
        out_specs=[pl.BlockSpec((tr, BRANCH_W), lambda i: (i, 0)),
                   pl.BlockSpec((tr, BRANCH_W), lambda i: (i, 0))],
        out_shape=[jax.ShapeDtypeStruct((m, BRANCH_W), F32),
                   jax.ShapeDtypeStruct((m, BRANCH_W), F32)],
        compiler_params=_params("arbitrary"),
        name="rope_b",
    )(proj, proj, *tabs)


def _gdn_kernel(qkv_ref, hist_ref, ba_ref, z_ref, s0_ref, cw_ref, alog_ref, dtb_ref, na_ref,
                y_ref, sout_ref, s_scr, tail_scr, *, C, n_valid):
    c_idx = pl.program_id(1)

    @pl.when(c_idx == 0)
    def _():
        s_scr[...] = s0_ref[...]
        tail_scr[...] = hist_ref[...]

    u = qkv_ref[...]
    prev = tail_scr[...]
    w = cw_ref[...]
    acc = u * w[CONV_W - 1:CONV_W, :]
    row8 = lax.broadcasted_iota(jnp.int32, (SUBLANES, u.shape[1]), 0)
    for s in range(1, CONV_W):
        rolled = pltpu.roll(u, s, 0)
        first = jnp.where(row8 < s, pltpu.roll(prev, s, 0), rolled[:SUBLANES])
        shifted = first if C == SUBLANES else jnp.concatenate([first, rolled[SUBLANES:]], axis=0)
        acc = acc + shifted * w[CONV_W - 1 - s:CONV_W - s, :]
    tail_scr[...] = u[C - SUBLANES:, :]
    conv = _silu(acc)

    ba = ba_ref[...]
    ri = lax.broadcasted_iota(jnp.int32, (C, C), 0)
    ci = lax.broadcasted_iota(jnp.int32, (C, C), 1)
    tri = ri >= ci
    strict = ri > ci
    ltri = tri.astype(F32)
    utri = (ri <= ci).astype(F32)
    eye = (ri == ci).astype(F32)
    ones = jnp.ones((C, C), F32)
    valid = lax.broadcasted_iota(jnp.int32, (C, 1), 0) < n_valid
    n_levels = max(1, math.ceil(math.log2(n_valid)))

    for h in range(N_HEADS):
        sl = slice(h * HEAD_DIM, (h + 1) * HEAD_DIM)
        q = conv[:, h * HEAD_DIM:(h + 1) * HEAD_DIM]
        k = conv[:, BRANCH_W + h * HEAD_DIM:BRANCH_W + (h + 1) * HEAD_DIM]
        v = conv[:, 2 * BRANCH_W + h * HEAD_DIM:2 * BRANCH_W + (h + 1) * HEAD_DIM]
        qn = q * lax.rsqrt(jnp.sum(q * q, axis=-1, keepdims=True) + EPS) * QK_SCALE
        kn = k * lax.rsqrt(jnp.sum(k * k, axis=-1, keepdims=True) + EPS)
        beta = jax.nn.sigmoid(ba[:, h:h + 1])
        xg = ba[:, N_HEADS + h:N_HEADS + h + 1] + dtb_ref[0:1, h:h + 1]
        softplus = jnp.maximum(xg, 0.0) + jnp.log1p(jnp.exp(-jnp.abs(xg)))
        g = -jnp.exp(alog_ref[0:1, h:h + 1]) * softplus
        if n_valid < C:
            beta = jnp.where(valid, beta, 0.0)
            g = jnp.where(valid, g, 0.0)
        g_b = jnp.broadcast_to(g, (C, HEAD_DIM))
        beta_b = jnp.broadcast_to(beta, (C, HEAD_DIM))

        gc_b = _hdot(ltri, g_b)
        gc_row = _hdot(ones, g_b[:, :C] * utri)
        diff = gc_b[:, :C] - gc_row
        decay = jnp.where(tri, jnp.exp(jnp.where(tri, diff, 0.0)), 0.0)
        kb = kn * beta_b
        a = -jnp.where(strict, _bdot_g(kb, kn, NT) * decay, 0.0)
        inv = eye + a
        apow = a
        for _ in range(n_levels - 1):
            apow = _hdot(apow, apow)
            inv = inv + _hdot(inv, apow)
        eg = jnp.exp(gc_b)
        u_s = _hdot(inv, v * beta_b)
        w_s = _hdot(inv, kb * eg)
        attn = _bdot_g(qn, kn, NT) * decay
        g_last = gc_b[C - 1:C, :]
        k_dec = kn * jnp.exp(g_last - gc_b)

        st = s_scr[h]
        v_new = u_s - _bdot(w_s, st)
        o = _bdot(qn * eg, st) + _bdot(attn, v_new)
        s_scr[h] = st * jnp.exp(g_last) + _bdot_g(k_dec, v_new, TN)

        on = o * lax.rsqrt(jnp.mean(o * o, axis=-1, keepdims=True) + EPS) * na_ref[...]
        y_ref[:, sl] = on * _silu(z_ref[:, sl])

    @pl.when(c_idx == pl.num_programs(1) - 1)
    def _():
        sout_ref[...] = s_scr[...]


def _gdn_call(proj, ba, hist, s0, conv_w, alog, dtb, norm_a, *, n_b, t_rows, C, n_valid):
    n_c = t_rows // C
    row = lambda b, c: b * n_c + c
    vec_spec = pl.BlockSpec((1, HEAD_DIM), lambda b, c: (0, 0))
    state_spec = pl.BlockSpec((None, N_HEADS, HEAD_DIM, HEAD_DIM), lambda b, c: (b, 0, 0, 0))
    return pl.pallas_call(
        functools.partial(_gdn_kernel, C=C, n_valid=n_valid),
        grid=(n_b, n_c),
        in_specs=[pl.BlockSpec((C, 3 * BRANCH_W), lambda b, c: (row(b, c), 0)),
                  pl.BlockSpec((None, SUBLANES, 3 * BRANCH_W), lambda b, c: (b, 0, 0)),
                  pl.BlockSpec((C, BA_W), lambda b, c: (row(b, c), 0)),
                  pl.BlockSpec((C, BRANCH_W), lambda b, c: (row(b, c), COL_Z_A)),
                  state_spec,
                  pl.BlockSpec((CONV_W, 3 * BRANCH_W), lambda b, c: (0, 0)),
                  vec_spec, vec_spec, vec_spec],
        out_specs=[pl.BlockSpec((C, BRANCH_W), lambda b, c: (row(b, c), 0)), state_spec],
        out_shape=[jax.ShapeDtypeStruct((n_b * t_rows, BRANCH_W), F32),
                   jax.ShapeDtypeStruct((n_b, N_HEADS, HEAD_DIM, HEAD_DIM), F32)],
        scratch_shapes=[pltpu.VMEM((N_HEADS, HEAD_DIM, HEAD_DIM), F32),
                        pltpu.VMEM((SUBLANES, 3 * BRANCH_W), F32)],
        compiler_params=_params("arbitrary", "arbitrary"),
        name="gdn",
    )(proj, hist, ba, proj, s0, conv_w, alog, dtb, norm_a)


def _ret_kernel(q_ref, k_ref, v_ref, z_ref, cos_ref, sin_ref, s0_ref, dmat_ref, qdec_ref, kdec_ref,
                cdec_ref, nc_ref, y_ref, sout_ref, s_scr):
    c_idx = pl.program_id(1)

    @pl.when(c_idx == 0)
    def _():
        s_scr[...] = s0_ref[...]

    cos, sin = cos_ref[...], sin_ref[...]
    for h in range(N_HEADS):
        sl = slice(h * HEAD_DIM, (h + 1) * HEAD_DIM)
        q = q_ref[:, sl]
        k = k_ref[:, sl]
        v = v_ref[:, sl]
        qr = q * cos + pltpu.roll(q, HEAD_DIM // 2, 1) * sin
        kr = (k * cos + pltpu.roll(k, HEAD_DIM // 2, 1) * sin) * QK_SCALE
        o_intra = _bdot(_bdot_g(qr, kr, NT) * dmat_ref[h], v)
        kv = _bdot_g(kr * kdec_ref[h], v, TN)
        st = s_scr[h]
        o = o_intra + _bdot(qr * qdec_ref[h], st)
        s_scr[h] = st * cdec_ref[h] + kv
        on = o * lax.rsqrt(jnp.mean(o * o, axis=-1, keepdims=True) + EPS) * nc_ref[...]
        y_ref[:, sl] = on * _silu(z_ref[:, sl])

    @pl.when(c_idx == pl.num_programs(1) - 1)
    def _():
        sout_ref[...] = s_scr[...]


def _ret_tables(C, n_valid):
    log_g = jnp.log1p(-(2.0 ** (-5.0 - jnp.arange(N_HEADS, dtype=F32))))
    idx = jnp.arange(C, dtype=F32)
    tri = jnp.tril(jnp.ones((C, C), dtype=bool))
    dmat = jnp.where(tri, jnp.exp(log_g[:, None, None] * jnp.where(tri, idx[:, None] - idx[None, :], 0.0)), 0.0)
    q_dec = jnp.exp(log_g[:, None] * (idx + 1.0))[..., None]
    k_dec = jnp.exp(log_g[:, None] * (n_valid - 1.0 - idx))[..., None]
    c_dec = jnp.exp(log_g * n_valid)[:, None, None]
    bc = lambda t, r: jnp.broadcast_to(t, (N_HEADS, r, HEAD_DIM))
    return dmat, bc(q_dec, C), bc(k_dec, C), bc(c_dec, 1)


def _ret_call(proj, cos2, sin2, s0, norm_c, *, n_b, t_rows, C, n_valid):
    n_c = t_rows // C
    row = lambda b, c: b * n_c + c
    dmat, q_dec, k_dec, c_dec = _ret_tables(C, n_valid)
    col_spec = lambda col: pl.BlockSpec((C, BRANCH_W), lambda b, c: (row(b, c), col))
    tab_spec = pl.BlockSpec((C, HEAD_DIM), lambda b, c: (c, 0))
    state_spec = pl.BlockSpec((None, N_HEADS, HEAD_DIM, HEAD_DIM), lambda b, c: (b, 0, 0, 0))
    const_spec = lambda r, w: pl.BlockSpec((N_HEADS, r, w), lambda b, c: (0, 0, 0))
    return pl.pallas_call(
        _ret_kernel,
        grid=(n_b, n_c),
        in_specs=[col_spec(COL_Q_C), col_spec(COL_K_C), col_spec(COL_V_C), col_spec(COL_Z_C),
                  tab_spec, tab_spec, state_spec,
                  const_spec(C, C), const_spec(C, HEAD_DIM), const_spec(C, HEAD_DIM), const_spec(1, HEAD_DIM),
                  pl.BlockSpec((1, HEAD_DIM), lambda b, c: (0, 0))],
        out_specs=[pl.BlockSpec((C, BRANCH_W), lambda b, c: (row(b, c), 0)), state_spec],
        out_shape=[jax.ShapeDtypeStruct((n_b * t_rows, BRANCH_W), F32),
                   jax.ShapeDtypeStruct((n_b, N_HEADS, HEAD_DIM, HEAD_DIM), F32)],
        scratch_shapes=[pltpu.VMEM((N_HEADS, HEAD_DIM, HEAD_DIM), F32)],
        compiler_params=_params("arbitrary", "arbitrary"),
        name="ret",
    )(proj, proj, proj, proj, cos2, sin2, s0, dmat, q_dec, k_dec, c_dec, norm_c)


def _topk_mask(gate, valid, axis):
    idx = lax.broadcasted_iota(jnp.int32, gate.shape, axis)
    gm = jnp.where(valid, gate, -jnp.inf)
    rank = jnp.zeros(gate.shape, jnp.int32)
    for m in range(gate.shape[axis]):
        gmm = gm[m:m + 1, :] if axis == 0 else gm[:, m:m + 1]
        beats = (gmm > gm) | ((gmm == gm) & (idx > m))
        rank = rank + jnp.where(beats, 1, 0)
    return valid & (rank < MOBA_TOPK)


def _moba_p_kernel(q_ref, k_ref, v_ref, z_ref, y_ref, kmean_scr, kb_scr, vt_scr, bias_scr, *, n_blk):
    qi = pl.program_id(2)
    blk = MOBA_BLOCK

    @pl.when(qi == 0)
    def _():
        for n in range(n_blk):
            kn = k_ref[n * blk:(n + 1) * blk, :]
            kmean_scr[n:n + 1, :] = jnp.sum(kn, axis=0, keepdims=True) * (1.0 / blk)
            kb_scr[n] = kn.astype(BF16)
            vt_scr[n] = v_ref[n * blk:(n + 1) * blk, :].T.astype(BF16)

    q = q_ref[...]
    gate = lax.dot_general(kmean_scr[...], q, NT, precision=HI, preferred_element_type=F32)
    blk_id = lax.broadcasted_iota(jnp.int32, gate.shape, 0)
    bias_scr[...] = jnp.where(_topk_mask(gate, blk_id < qi, 0), 0.0, NEG)

    qb = q.astype(BF16)
    s = lax.dot_general(kb_scr[qi], qb, NT, preferred_element_type=F32)
    key_i = lax.broadcasted_iota(jnp.int32, (blk, blk), 0)
    qry_i = lax.broadcasted_iota(jnp.int32, (blk, blk), 1)
    s = jnp.where(key_i <= qry_i, s, NEG)
    m0 = jnp.max(s, axis=0, keepdims=True)
    p = jnp.exp(s - m0)
    l0 = jnp.sum(p, axis=0, keepdims=True)
    acc0 = jnp.dot(vt_scr[qi], p.astype(BF16), preferred_element_type=F32)

    def body(n, carry):
        m, l, acc = carry
        s = lax.dot_general(kb_scr[n], qb, NT, preferred_element_type=F32) + bias_scr[pl.ds(n, 1), :]
        m_new = jnp.maximum(m, jnp.max(s, axis=0, keepdims=True))
        p = jnp.exp(s - m_new)
        alpha = jnp.exp(m - m_new)
        l = alpha * l + jnp.sum(p, axis=0, keepdims=True)
        acc = alpha * acc + jnp.dot(vt_scr[n], p.astype(BF16), preferred_element_type=F32)
        return m_new, l, acc

    _, l, acc = lax.fori_loop(0, qi, body, (m0, l0, acc0))
    y_ref[...] = (acc / l).T * _silu(z_ref[...])


def _moba_p_call(q_rot, k_rot, proj, *, n_b, t_rows):
    n_blk = t_rows // MOBA_BLOCK
    assert t_rows % MOBA_BLOCK == 0 and n_blk >= MOBA_TOPK
    blk = MOBA_BLOCK
    return pl.pallas_call(
        functools.partial(_moba_p_kernel, n_blk=n_blk),
        grid=(n_b, N_HEADS, n_blk),
        in_specs=[pl.BlockSpec((blk, HEAD_DIM), lambda b, h, i: (b * n_blk + i, h)),
                  pl.BlockSpec((t_rows, HEAD_DIM), lambda b, h, i: (b, h)),
                  pl.BlockSpec((t_rows, HEAD_DIM), lambda b, h, i: (b, COL_V_B * N_HEADS + h)),
                  pl.BlockSpec((blk, HEAD_DIM), lambda b, h, i: (b * n_blk + i, COL_Z_B * N_HEADS + h))],
        out_specs=pl.BlockSpec((blk, HEAD_DIM), lambda b, h, i: (b * n_blk + i, h)),
        out_shape=jax.ShapeDtypeStruct((n_b * t_rows, BRANCH_W), F32),
        scratch_shapes=[pltpu.VMEM((n_blk, HEAD_DIM), F32),
                        pltpu.VMEM((n_blk, blk, HEAD_DIM), BF16),
                        pltpu.VMEM((n_blk, HEAD_DIM, blk), BF16),
                        pltpu.VMEM((n_blk, blk), F32)],
        compiler_params=_params("arbitrary", "arbitrary", "arbitrary"),
        name="moba_p",
    )(q_rot, k_rot, proj, proj)


PAGES_PER_STEP = 4


def _moba_s_kernel(pt_ref, q_ref, kn_ref, vn_ref, z_ref, *refs, n_valid, pages_per_blk, pps):
    del pt_ref
    k_refs, v_refs = refs[:pps], refs[pps:2 * pps]
    y_ref, q_scr, ksum_scr, m_scr, l_scr, o_scr = refs[2 * pps:]
    step = pl.program_id(1)
    rows = N_HEADS * SUBLANES
    page_rows = k_refs[0].shape[0]
    heads = [slice(h * HEAD_DIM, (h + 1) * HEAD_DIM) for h in range(N_HEADS)]
    hrows = [slice(h * SUBLANES, (h + 1) * SUBLANES) for h in range(N_HEADS)]

    @pl.when(step == 0)
    def _():
        q = q_ref[...]
        q_scr[...] = jnp.concatenate([q[:, sl] for sl in heads], axis=0)

    q32 = q_scr[...]
    qb = q32.astype(BF16)
    row_head = lax.broadcasted_iota(jnp.int32, (rows, page_rows), 0) // SUBLANES
    col_head = lax.broadcasted_iota(jnp.int32, (rows, page_rows), 1) % N_HEADS
    head_bias = jnp.where(row_head == col_head, 0.0, NEG)

    blks_per_step = pps // pages_per_blk
    for bl in range(blks_per_step):
        ksum = None
        for j in range(bl * pages_per_blk, (bl + 1) * pages_per_blk):
            pg = step * pps + j
            kp = k_refs[j][...]
            s = _bdot_g(qb, kp, NT) + head_bias
            m = jnp.max(s, axis=1, keepdims=True)
            e = jnp.exp(s - m)
            m_scr[pg] = jnp.broadcast_to(m, (rows, HEAD_DIM))
            l_scr[pg] = jnp.broadcast_to(jnp.sum(e, axis=1, keepdims=True), (rows, HEAD_DIM))
            o_scr[pg] = _bdot(e, v_refs[j][...])
            part = jnp.sum(kp.reshape(page_rows // SUBLANES, SUBLANES, HEAD_DIM), axis=0)
            ksum = part if ksum is None else ksum + part
        ksum_scr[step * blks_per_step + bl] = ksum

    @pl.when(step == pl.num_programs(1) - 1)
    def _():
        n_pg = m_scr.shape[0]
        n_blk = ksum_scr.shape[0]
        kflat = ksum_scr[...].reshape(n_blk * SUBLANES, HEAD_DIM) * (1.0 / MOBA_BLOCK)
        g_all = lax.dot_general(q32, kflat, NT, precision=HI, preferred_element_type=F32)
        rh = lax.broadcasted_iota(jnp.int32, g_all.shape, 0) // SUBLANES
        ch = lax.broadcasted_iota(jnp.int32, g_all.shape, 1) % N_HEADS
        pool = (lax.broadcasted_iota(jnp.int32, (n_blk * SUBLANES, n_blk), 0) // SUBLANES
                == lax.broadcasted_iota(jnp.int32, (n_blk * SUBLANES, n_blk), 1)).astype(F32)
        gate = _hdot(jnp.where(rh == ch, g_all, 0.0), pool)
        sel_f = jnp.where(_topk_mask(gate, jnp.ones(gate.shape, jnp.bool_), 1), 1.0, 0.0)
        selw = [jnp.broadcast_to(sel_f[:, n:n + 1], (rows, HEAD_DIM)) > 0.5 for n in range(n_blk)]

        kn, vn = kn_ref[...], vn_ref[...]
        s_own = jnp.concatenate([_bdot_g(q32[hr], kn[:, sl], NT) for hr, sl in zip(hrows, heads)], axis=0)
        rq = lax.broadcasted_iota(jnp.int32, s_own.shape, 0) % SUBLANES
        cj = lax.broadcasted_iota(jnp.int32, s_own.shape, 1)
        own_ok = (cj <= rq) & (cj < n_valid)
        s_own = jnp.where(own_ok, s_own, NEG)
        mx = jnp.broadcast_to(jnp.max(s_own, axis=1, keepdims=True), (rows, HEAD_DIM))
        for pg in range(n_pg):
            mx = jnp.maximum(mx, jnp.where(selw[pg // pages_per_blk], m_scr[pg], NEG))
        e_own = jnp.where(own_ok, jnp.exp(s_own - mx[:, :SUBLANES]), 0.0)
        l_tot = jnp.broadcast_to(jnp.sum(e_own, axis=1, keepdims=True), (rows, HEAD_DIM))
        o_tot = jnp.concatenate([_bdot(e_own[hr], vn[:, sl]) for hr, sl in zip(hrows, heads)], axis=0)
        for pg in range(n_pg):
            wgt = jnp.where(selw[pg // pages_per_blk], jnp.exp(jnp.minimum(m_scr[pg] - mx, 0.0)), 0.0)
            l_tot = l_tot + wgt * l_scr[pg]
            o_tot = o_tot + wgt * o_scr[pg]
        o = o_tot / l_tot
        y_ref[...] = jnp.concatenate([o[hr] for hr in hrows], axis=1) * _silu(z_ref[...])


def _moba_s_call(page_table, q_rot, k_rot, proj, cache_k, cache_v, layer, *, n_valid):
    n_b, n_pages = page_table.shape
    depth, n_pool, page = cache_k.shape[:3]
    assert MOBA_BLOCK % page == 0
    pages_per_blk = MOBA_BLOCK // page
    pps = PAGES_PER_STEP
    assert pps % pages_per_blk == 0 and n_pages % pps == 0 and n_pages // pages_per_blk >= MOBA_TOPK
    assert (page * N_HEADS) % SUBLANES == 0 and SUBLANES % N_HEADS == 0
    ck = cache_k.reshape(depth, n_pool, page * N_HEADS, HEAD_DIM)
    cv = cache_v.reshape(depth, n_pool, page * N_HEADS, HEAD_DIM)
    rows = N_HEADS * SUBLANES
    row_spec = lambda col: pl.BlockSpec((SUBLANES, BRANCH_W), lambda b, p, pt: (b, col))

    def page_spec(j):
        return pl.BlockSpec((None, None, page * N_HEADS, HEAD_DIM),
                            lambda b, p, pt: (layer, pt[b, p * pps + j], 0, 0))

    page_specs = [page_spec(j) for j in range(pps)]
    grid_spec = pltpu.PrefetchScalarGridSpec(
        num_scalar_prefetch=1,
        grid=(n_b, n_pages // pps),
        in_specs=[row_spec(0), row_spec(0), row_spec(COL_V_B), row_spec(COL_Z_B)] + page_specs + page_specs,
        out_specs=pl.BlockSpec((SUBLANES, BRANCH_W), lambda b, p, pt: (b, 0)),
        scratch_shapes=[pltpu.VMEM((rows, HEAD_DIM), F32),
                        pltpu.VMEM((n_pages // pages_per_blk, SUBLANES, HEAD_DIM), F32),
                        pltpu.VMEM((n_pages, rows, HEAD_DIM), F32),
                        pltpu.VMEM((n_pages, rows, HEAD_DIM), F32),
                        pltpu.VMEM((n_pages, rows, HEAD_DIM), F32)])
    return pl.pallas_call(
        functools.partial(_moba_s_kernel, n_valid=n_valid, pages_per_blk=pages_per_blk, pps=pps),
        grid_spec=grid_spec,
        out_shape=jax.ShapeDtypeStruct((n_b * SUBLANES, BRANCH_W), F32),
        compiler_params=_params("arbitrary", "arbitrary"),
        name="moba_s",
    )(page_table, q_rot, k_rot, proj, proj, *([ck] * pps), *([cv] * pps))


def _merge_kernel(ya_ref, yb_ref, yc_ref, mg_ref, x_ref, gate_ref, wb_ref, wo_ref, nf_ref, *out_refs, final):
    d = x_ref.shape[1]
    mixed = None
    for n, y_ref in enumerate((ya_ref, yb_ref, yc_ref)):
        per_branch = _bdot(y_ref[...], wb_ref[n])
        term = jax.nn.sigmoid(mg_ref[:, n * d:(n + 1) * d]) * per_branch
        mixed = term if mixed is None else mixed + term
    x_out = x_ref[...] + gate_ref[...] * _bdot(mixed, wo_ref[...])
    out_refs[0][...] = x_out
    if final:
        out_refs[1][...] = (x_out * lax.rsqrt(jnp.mean(x_out * x_out, axis=-1, keepdims=True) + EPS)
                            * nf_ref[...])


def _merge_call(y_a, y_b, y_c, proj, x2d, gate, w_branch, w_out, norm_f, *, tm, rows_per_mod, final):
    m, d = x2d.shape
    if gate.ndim == 3:
        gate_spec = pl.BlockSpec((None, 1, d), lambda i: ((i * tm) // rows_per_mod, 0, 0))
    else:
        gate_spec = pl.BlockSpec((tm, d), lambda i: (i, 0))
    y_spec = pl.BlockSpec((tm, BRANCH_W), lambda i: (i, 0))
    x_spec = pl.BlockSpec((tm, d), lambda i: (i, 0))
    n_out = 2 if final else 1
    outs = pl.pallas_call(
        functools.partial(_merge_kernel, final=final),
        grid=(m // tm,),
        in_specs=[y_spec, y_spec, y_spec,
                  pl.BlockSpec((tm, N_BRANCH * d), lambda i: (i, (COL_MERGE * BRANCH_W) // (N_BRANCH * d))),
                  x_spec, gate_spec,
                  pl.BlockSpec((N_BRANCH, BRANCH_W, d), lambda i: (0, 0, 0)),
                  pl.BlockSpec((d, d), lambda i: (0, 0)),
                  pl.BlockSpec((1, d), lambda i: (0, 0))],
        out_specs=[x_spec] * n_out,
        out_shape=[jax.ShapeDtypeStruct((m, d), F32)] * n_out,
        compiler_params=_params("arbitrary"),
        name="merge",
    )(y_a, y_b, y_c, proj, x2d, gate, w_branch, w_out, norm_f.reshape(1, d))
    return outs


def _rope_tables(pos):
    t = pos.shape[0]
    posf = pos.astype(F32)

    def cos_sin(n_rot, theta):
        half = n_rot // 2
        inv = theta ** (-jnp.arange(half, dtype=F32) / half)
        ang = posf[:, None] * inv[None, :]
        return jnp.cos(ang), jnp.sin(ang)

    cb, sb = cos_sin(ROPE_DIMS, ROPE_THETA)
    hb = ROPE_DIMS // 2
    tab_b = (jnp.concatenate([cb, cb, jnp.ones((t, HEAD_DIM - ROPE_DIMS), F32)], axis=1),
             jnp.concatenate([jnp.zeros((t, hb), F32), sb, jnp.zeros((t, HEAD_DIM - ROPE_DIMS), F32)], axis=1),
             jnp.concatenate([-sb, jnp.zeros((t, HEAD_DIM - hb), F32)], axis=1))
    cc, sc = cos_sin(HEAD_DIM, RET_THETA)
    tab_c = (jnp.concatenate([cc, cc], axis=1), jnp.concatenate([-sc, sc], axis=1))
    return tab_b, tab_c


def _pad_rows(a2d, n_b, t, t_pad):
    w = a2d.shape[1]
    return jnp.pad(a2d.reshape(n_b, t, w), ((0, 0), (0, t_pad - t), (0, 0))).reshape(n_b * t_pad, w)


def _vec128(v):
    return jnp.pad(v.astype(F32), (0, HEAD_DIM - v.shape[0])).reshape(1, HEAD_DIM)


def kernel(x_prompt, x_sample, cache_k, cache_v, state_gdn, state_conv, state_ret, page_table, c_prompt, c_sample,
           norm_in, w_ada, b_ada, w_in, conv_w, a_log, dt_bias, norm_a, norm_c, w_branch, w_out, norm_f):
    n_b, seq, d = x_prompt.shape
    n_db, dec_seq, _ = x_sample.shape
    depth = w_in.shape[0]
    n_pages = page_table.shape[1]
    past_len = n_pages * cache_k.shape[2]
    assert d == 2 * BRANCH_W and dec_seq <= SUBLANES and dec_seq >= CONV_W - 1
    assert seq % GDN_CHUNK == 0 and seq % RET_CHUNK == 0
    t_pad = SUBLANES

    ba0 = 4 * BRANCH_W
    w_main = jnp.concatenate([w_in[:, :, :ba0], w_in[:, :, ba0 + 2 * N_HEADS:]], axis=2).astype(BF16)
    w_ba = jnp.pad(w_in[:, :, ba0:ba0 + 2 * N_HEADS], ((0, 0), (0, 0), (0, BA_W - 2 * N_HEADS))).astype(BF16)
    w_ada_b = w_ada.astype(BF16)
    w_branch_b = w_branch.astype(BF16)
    w_out_b = w_out.astype(BF16)

    n_c = n_b + n_db
    c_rows = -(-n_c // SUBLANES) * SUBLANES
    c_all = jnp.pad(jnp.concatenate([c_prompt, c_sample], axis=0), ((0, c_rows - n_c), (0, 0)))
    mods = _mod_call(c_all, w_ada_b, b_ada)

    tab_b_p, tab_c_p = _rope_tables(jnp.arange(seq, dtype=jnp.int32))
    tab_b_s, tab_c_s = _rope_tables(past_len + jnp.arange(t_pad, dtype=jnp.int32))

    zeros_state = jnp.zeros((n_b, N_HEADS, HEAD_DIM, HEAD_DIM), F32)
    zeros_hist = jnp.zeros((n_b, SUBLANES, 3 * BRANCH_W), F32)

    xp = x_prompt.reshape(n_b * seq, d)
    xs = x_sample.reshape(n_db * dec_seq, d)
    outs = {k: [] for k in ("kp", "vp", "ks", "vs", "gp", "gs", "cp", "cs", "rp", "rs")}
    y_p = y_s = None
    for l in range(depth):
        final = l == depth - 1
        alog, dtb = _vec128(a_log[l]), _vec128(dt_bias[l])
        na, nc = norm_a[l].reshape(1, HEAD_DIM), norm_c[l].reshape(1, HEAD_DIM)
        shift, scale, gate = jnp.split(mods[l], 3, axis=-1)

        mod_p = [t[:n_b].reshape(n_b, 1, d) for t in (scale, shift, gate)]
        proj, ba = _inproj_call(xp, mod_p[0], mod_p[1], norm_in[l], w_main[l], w_ba[l],
                                tm=min(seq, 1024), rows_per_mod=seq)
        q_rot, k_rot = _rope_b_call(proj, tab_b_p, tr=min(seq, 512), t_rows=seq)
        y_a, gdn_new = _gdn_call(proj, ba, zeros_hist, zeros_state, conv_w[l], alog, dtb, na,
                                 n_b=n_b, t_rows=seq, C=GDN_CHUNK, n_valid=GDN_CHUNK)
        y_c, ret_new = _ret_call(proj, tab_c_p[0], tab_c_p[1], zeros_state, nc,
                                 n_b=n_b, t_rows=seq, C=RET_CHUNK, n_valid=RET_CHUNK)
        y_b = _moba_p_call(q_rot, k_rot, proj, n_b=n_b, t_rows=seq)
        res = _merge_call(y_a, y_b, y_c, proj, xp, mod_p[2], w_branch_b[l], w_out_b[l], norm_f,
                          tm=min(seq, 256), rows_per_mod=seq, final=final)
        xp = res[0]
        if final:
            y_p = res[1]
        proj3 = proj.reshape(n_b, seq, MAIN_W)
        outs["kp"].append(k_rot.reshape(n_b, seq, N_HEADS, HEAD_DIM))
        outs["vp"].append(proj3[:, :, COL_V_B * BRANCH_W:(COL_V_B + 1) * BRANCH_W].reshape(n_b, seq, N_HEADS, HEAD_DIM))
        outs["gp"].append(gdn_new)
        outs["cp"].append(proj3[:, seq - (CONV_W - 1):, :3 * BRANCH_W])
        outs["rp"].append(ret_new)

        mod_s = [jnp.repeat(t[n_b:n_c], dec_seq, axis=0) for t in (scale, shift, gate)]
        proj_s, ba_s = _inproj_call(xs, mod_s[0], mod_s[1], norm_in[l], w_main[l], w_ba[l],
                                    tm=n_db * dec_seq, rows_per_mod=dec_seq)
        proj_sp = _pad_rows(proj_s, n_db, dec_seq, t_pad)
        ba_sp = _pad_rows(ba_s, n_db, dec_seq, t_pad)
        hist = jnp.pad(state_conv[l], ((0, 0), (SUBLANES - (CONV_W - 1), 0), (0, 0)))
        q_rot_s, k_rot_s = _rope_b_call(proj_sp, tab_b_s, tr=t_pad, t_rows=t_pad)
        y_a_s, gdn_new_s = _gdn_call(proj_sp, ba_sp, hist, state_gdn[l], conv_w[l], alog, dtb, na,
                                     n_b=n_db, t_rows=t_pad, C=t_pad, n_valid=dec_seq)
        y_c_s, ret_new_s = _ret_call(proj_sp, tab_c_s[0], tab_c_s[1], state_ret[l], nc,
                                     n_b=n_db, t_rows=t_pad, C=t_pad, n_valid=dec_seq)
        y_b_s = _moba_s_call(page_table, q_rot_s, k_rot_s, proj_sp, cache_k, cache_v, l, n_valid=dec_seq)
        unpad = lambda y: y.reshape(n_db, t_pad, BRANCH_W)[:, :dec_seq].reshape(n_db * dec_seq, BRANCH_W)
        res_s = _merge_call(unpad(y_a_s), unpad(y_b_s), unpad(y_c_s), proj_s, xs, mod_s[2],
                            w_branch_b[l], w_out_b[l], norm_f, tm=n_db * dec_seq, rows_per_mod=dec_seq, final=final)
        xs = res_s[0]
        if final:
            y_s = res_s[1]
        proj_s3 = proj_s.reshape(n_db, dec_seq, MAIN_W)
        outs["ks"].append(k_rot_s.reshape(n_db, t_pad, N_HEADS, HEAD_DIM)[:, :dec_seq])
        outs["vs"].append(proj_s3[:, :, COL_V_B * BRANCH_W:(COL_V_B + 1) * BRANCH_W]
                          .reshape(n_db, dec_seq, N_HEADS, HEAD_DIM))
        outs["gs"].append(gdn_new_s)
        outs["cs"].append(proj_s3[:, dec_seq - (CONV_W - 1):, :3 * BRANCH_W])
        outs["rs"].append(ret_new_s)

    st = {k: jnp.stack(v) for k, v in outs.items()}
    return (y_p.reshape(n_b, seq, d), y_s.reshape(n_db, dec_seq, d),
            st["kp"], st["vp"], st["ks"], st["vs"], st["gp"], st["gs"],
            st["cp"], st["cs"], st["rp"], st["rs"])
```

```python
import functools
import math

import jax
import jax.numpy as jnp
from jax import lax
from jax.experimental import pallas as pl
from jax.experimental.pallas import tpu as pltpu

F32 = jnp.float32
BF16 = jnp.bfloat16
HI = lax.Precision.HIGHEST

HEAD_DIM = 128
N_HEADS = 4
BRANCH_W = N_HEADS * HEAD_DIM
N_BRANCH = 3
CONV_W = 4
GDN_CHUNK = 64
RET_CHUNK = 64
MOBA_BLOCK = 256
MOBA_TOPK = 3
ROPE_THETA = 500000.0
ROPE_DIMS = HEAD_DIM // 4
RET_THETA = 10000.0
EPS = 1e-6
NEG = -1e30
SUBLANES = 8
QK_SCALE = HEAD_DIM ** -0.5

COL_QKV_A, COL_Z_A, COL_Q_B, COL_K_B, COL_V_B, COL_Z_B = 0, 3, 4, 5, 6, 7
COL_Q_C, COL_K_C, COL_V_C, COL_Z_C, COL_MERGE = 8, 9, 10, 11, 12
MAIN_W = 18 * BRANCH_W
BA_W = 128

NT = (((1,), (1,)), ((), ()))
TN = (((0,), (0,)), ((), ()))

VMEM_LIMIT = 48 * 1024 * 1024


def _params(*sem):
    return pltpu.CompilerParams(dimension_semantics=sem, vmem_limit_bytes=VMEM_LIMIT)


def _silu(x):
    return x * jax.nn.sigmoid(x)


def _bdot(a, b):
    return jnp.dot(a.astype(BF16), b.astype(BF16), preferred_element_type=F32)


def _bdot_g(a, b, dims):
    return lax.dot_general(a.astype(BF16), b.astype(BF16), dims, preferred_element_type=F32)


def _hdot(a, b):
    return jnp.dot(a, b, precision=HI, preferred_element_type=F32)


def _split_bf16(a):
    hi = a.astype(BF16)
    return hi, (a - hi.astype(F32)).astype(BF16)


def _dot3(a, b):
    a_hi, a_lo = _split_bf16(a)
    b_hi, b_lo = _split_bf16(b)
    dot = lambda x, y: jnp.dot(x, y, preferred_element_type=F32)
    return dot(a_hi, b_hi) + (dot(a_hi, b_lo) + dot(a_lo, b_hi))


def _mod_kernel(c_ref, w_ref, b_ref, o_ref):
    o_ref[...] = _bdot(_silu(c_ref[...]), w_ref[...]) + b_ref[...]


def _mod_call(c_all, w_ada, b_ada):
    depth, d, d3 = w_ada.shape
    rows = c_all.shape[0]
    tn = d
    return pl.pallas_call(
        _mod_kernel,
        grid=(depth, d3 // tn),
        in_specs=[pl.BlockSpec((rows, d), lambda l, j: (0, 0)),
                  pl.BlockSpec((None, d, tn), lambda l, j: (l, 0, j)),
                  pl.BlockSpec((None, 1, tn), lambda l, j: (l, 0, j))],
        out_specs=pl.BlockSpec((None, rows, tn), lambda l, j: (l, 0, j)),
        out_shape=jax.ShapeDtypeStruct((depth, rows, d3), F32),
        compiler_params=_params("arbitrary", "arbitrary"),
        name="mod",
    )(c_all, w_ada, b_ada.reshape(depth, 1, d3))


def _inproj_kernel(x_ref, sc_ref, sh_ref, g_ref, w_ref, wba_ref, o_ref, ba_ref, h_scr):
    @pl.when(pl.program_id(1) == 0)
    def _():
        x = x_ref[...]
        y = x * lax.rsqrt(jnp.mean(x * x, axis=-1, keepdims=True) + EPS) * g_ref[...]
        h = (y * (1.0 + sc_ref[...]) + sh_ref[...]).astype(BF16)
        h_scr[...] = h
        ba_ref[...] = jnp.dot(h, wba_ref[...], preferred_element_type=F32)

    o_ref[...] = jnp.dot(h_scr[...], w_ref[...], preferred_element_type=F32)


def _inproj_call(x2d, scale, shift, norm_g, w_main, w_ba, *, tm, rows_per_mod):
    m, d = x2d.shape
    tn = 1024
    if scale.ndim == 3:
        mod_spec = pl.BlockSpec((None, 1, d), lambda i, j: ((i * tm) // rows_per_mod, 0, 0))
    else:
        mod_spec = pl.BlockSpec((tm, d), lambda i, j: (i, 0))
    return pl.pallas_call(
        _inproj_kernel,
        grid=(m // tm, MAIN_W // tn),
        in_specs=[pl.BlockSpec((tm, d), lambda i, j: (i, 0)),
                  mod_spec, mod_spec,
                  pl.BlockSpec((1, d), lambda i, j: (0, 0)),
                  pl.BlockSpec((d, tn), lambda i, j: (0, j)),
                  pl.BlockSpec((d, BA_W), lambda i, j: (0, 0))],
        out_specs=[pl.BlockSpec((tm, tn), lambda i, j: (i, j)),
                   pl.BlockSpec((tm, BA_W), lambda i, j: (i, 0))],
        out_shape=[jax.ShapeDtypeStruct((m, MAIN_W), F32),
                   jax.ShapeDtypeStruct((m, BA_W), F32)],
        scratch_shapes=[pltpu.VMEM((tm, d), BF16)],
        compiler_params=_params("arbitrary", "arbitrary"),
        name="in_proj",
    )(x2d, scale, shift, norm_g.reshape(1, d), w_main, w_ba)


def _rope_b_kernel(q_ref, k_ref, c_ref, s1_ref, s2_ref, qo_ref, ko_ref):
    c, s1, s2 = c_ref[...], s1_ref[...], s2_ref[...]
    half = ROPE_DIMS // 2
    for h in range(N_HEADS):
        sl = slice(h * HEAD_DIM, (h + 1) * HEAD_DIM)
        q = q_ref[:, sl]
        k = k_ref[:, sl]
        qr = q * c + pltpu.roll(q, half, 1) * s1 + pltpu.roll(q, HEAD_DIM - half, 1) * s2
        kr = k * c + pltpu.roll(k, half, 1) * s1 + pltpu.roll(k, HEAD_DIM - half, 1) * s2
        qo_ref[:, sl] = qr * QK_SCALE
        ko_ref[:, sl] = kr


def _rope_b_call(proj, tabs, *, tr, t_rows):
    m = proj.shape[0]
    nt = t_rows // tr
    tab_spec = pl.BlockSpec((tr, HEAD_DIM), lambda i: (i % nt, 0))
    return pl.pallas_call(
        _rope_b_kernel,
        grid=(m // tr,),
        in_specs=[pl.BlockSpec((tr, BRANCH_W), lambda i: (i, COL_Q_B)),
                  pl.BlockSpec((tr, BRANCH_W), lambda i: (i, COL_K_B)),
                  tab_spec, tab_spec, tab_spec],
        out_specs=[pl.BlockSpec((tr, BRANCH_W), lambda i: (i, 0)),
                   pl.BlockSpec((tr, BRANCH_W), lambda i: (i, 0))],
        out_shape=[jax.ShapeDtypeStruct((m, BRANCH_W), F32),
                   jax.ShapeDtypeStruct((m, BRANCH_W), F32)],
        compiler_params=_params("arbitrary"),
        name="rope_b",
    )(proj, proj, *tabs)


INV_BASE = 8
GDN_CHUNKS_PER_STEP = 4


def _dot3_each(xs, ys):
    xs = [_split_bf16(x) for x in xs]
    ys = [_split_bf16(y) for y in ys]
    dot = lambda x, y: jnp.dot(x, y, preferred_element_type=F32)
    hh = [dot(x[0], y[0]) for x, y in zip(xs, ys)]
    hl = [dot(x[0], y[1]) for x, y in zip(xs, ys)]
    lh = [dot(x[1], y[0]) for x, y in zip(xs, ys)]
    return [a + (b + c) for a, b, c in zip(hh, hl, lh)]


def _unit_lower_inverses(mats, ri, ci, n):
    base = min(INV_BASE, n)
    eye = (ri == ci).astype(F32)
    dpows = [jnp.where((ri // base) == (ci // base), a, 0.0) for a in mats]
    invs = [eye + d for d in dpows]
    for _ in range(int(math.log2(base)) - 1):
        dpows = _dot3_each(dpows, dpows)
        invs = [inv + t for inv, t in zip(invs, _dot3_each(invs, dpows))]
    size = base
    while size < n:
        off = ((ri // (2 * size)) == (ci // (2 * size))) & ((ri // size) != (ci // size))
        es = [jnp.where(off, a, 0.0) for a in mats]
        invs = [inv + t for inv, t in zip(invs, _dot3_each(invs, _dot3_each(es, invs)))]
        size *= 2
    return invs


def _gdn_kernel(qkv_ref, hist_ref, ba_ref, z_ref, s0_ref, cw_ref, alog_ref, dtb_ref, na_ref,
                y_ref, sout_ref, s_scr, tail_scr, *, C, n_valid, n_sub):
    c_idx = pl.program_id(1)
    rows = n_sub * C

    @pl.when(c_idx == 0)
    def _():
        s_scr[...] = s0_ref[...]
        tail_scr[...] = hist_ref[...]

    u = qkv_ref[...]
    prev = tail_scr[...]
    w = cw_ref[...]
    acc = u * w[CONV_W - 1:CONV_W, :]
    row8 = lax.broadcasted_iota(jnp.int32, (SUBLANES, u.shape[1]), 0)
    for s in range(1, CONV_W):
        rolled = pltpu.roll(u, s, 0)
        first = jnp.where(row8 < s, pltpu.roll(prev, s, 0), rolled[:SUBLANES])
        shifted = first if rows == SUBLANES else jnp.concatenate([first, rolled[SUBLANES:]], axis=0)
        acc = acc + shifted * w[CONV_W - 1 - s:CONV_W - s, :]
    tail_scr[...] = u[rows - SUBLANES:, :]
    conv = _silu(acc)

    ba = ba_ref[...]
    beta_t = jax.nn.sigmoid(ba)
    xg = ba + dtb_ref[...]
    g_t = -jnp.exp(alog_ref[...]) * (jnp.maximum(xg, 0.0) + jnp.log1p(jnp.exp(-jnp.abs(xg))))
    row_t = lax.broadcasted_iota(jnp.int32, ba.shape, 0) % C
    if n_valid < C:
        beta_t = jnp.where(row_t < n_valid, beta_t, 0.0)
        g_t = jnp.where(row_t < n_valid, g_t, 0.0)
    gc_t = g_t
    shift = 1
    while shift < C:
        gc_t = gc_t + jnp.where(row_t >= shift, pltpu.roll(gc_t, shift, 0), 0.0)
        shift *= 2

    cs = N_HEADS * C
    subs = range(n_sub)
    stack = lambda f: jnp.concatenate([f(h) for h in range(N_HEADS)], axis=0)
    ri = lax.broadcasted_iota(jnp.int32, (cs, cs), 0)
    ci = lax.broadcasted_iota(jnp.int32, (cs, cs), 1)
    same_head = (ri // C) == (ci // C)
    tri = same_head & (ri >= ci)
    strict = same_head & (ri > ci)

    qn, kn, kb, vb, gc, g_last, decay = [], [], [], [], [], [], []
    for j in subs:
        r0 = j * C
        lanes = lambda t, col: jnp.broadcast_to(t[r0:r0 + C, col:col + 1], (C, HEAD_DIM))
        head_cols = lambda base: stack(lambda h: conv[r0:r0 + C, base + h * HEAD_DIM:base + (h + 1) * HEAD_DIM])
        q, k, v = head_cols(0), head_cols(BRANCH_W), head_cols(2 * BRANCH_W)
        beta = stack(lambda h: lanes(beta_t, h))
        gc_j = stack(lambda h: lanes(gc_t, N_HEADS + h))
        if cs % HEAD_DIM == 0:
            gc_row = jnp.concatenate([gc_j.T] * (cs // HEAD_DIM), axis=0)
            gc_col = jnp.concatenate([gc_j] * (cs // HEAD_DIM), axis=1)
        else:
            g_st = stack(lambda h: lanes(g_t, N_HEADS + h))
            upper = (same_head & (ri <= ci)).astype(F32)
            gc_row = _hdot(jnp.ones((cs, cs), F32), g_st[:, :cs] * upper)
            gc_col = gc_j[:, :cs]
        kn_j = k * lax.rsqrt(jnp.sum(k * k, axis=-1, keepdims=True) + EPS)
        qn.append(q * lax.rsqrt(jnp.sum(q * q, axis=-1, keepdims=True) + EPS) * QK_SCALE)
        kn.append(kn_j)
        kb.append(kn_j * beta)
        vb.append(v * beta)
        gc.append(gc_j)
        g_last.append(stack(lambda h: jnp.broadcast_to(
            gc_t[r0 + C - 1:r0 + C, N_HEADS + h:N_HEADS + h + 1], (C, HEAD_DIM))))
        decay.append(jnp.where(tri, jnp.exp(jnp.where(tri, gc_col - gc_row, 0.0)), 0.0))

    kk = [_bdot_g(kb[j], kn[j], NT) for j in subs]
    qk = [_bdot_g(qn[j], kn[j], NT) for j in subs]
    invs = _unit_lower_inverses([-jnp.where(strict, kk[j] * decay[j], 0.0) for j in subs], ri, ci, C)
    eg = [jnp.exp(gc[j]) for j in subs]
    sols = _dot3_each(invs, [jnp.concatenate([vb[j], kb[j] * eg[j]], axis=1) for j in subs])
    attn = [qk[j] * decay[j] for j in subs]
    q_dec = [qn[j] * eg[j] for j in subs]
    k_dec = [kn[j] * jnp.exp(g_last[j] - gc[j]) for j in subs]

    hrows = [slice(h * C, (h + 1) * C) for h in range(N_HEADS)]
    states = [s_scr[h] for h in range(N_HEADS)]
    for j in subs:
        u_s, w_s = sols[j][:, :HEAD_DIM], sols[j][:, HEAD_DIM:]
        v_new = jnp.concatenate([u_s[hr] - _bdot(w_s[hr], st) for hr, st in zip(hrows, states)], axis=0)
        o = (jnp.concatenate([_bdot(q_dec[j][hr], st) for hr, st in zip(hrows, states)], axis=0)
             + _bdot(attn[j], v_new))
        states = [st * jnp.exp(g_last[j][h * C:h * C + 1, :]) + _bdot_g(k_dec[j][hr], v_new[hr], TN)
                  for h, (hr, st) in enumerate(zip(hrows, states))]
        on = o * lax.rsqrt(jnp.mean(o * o, axis=-1, keepdims=True) + EPS) * na_ref[...]
        for h, hr in enumerate(hrows):
            sl = slice(h * HEAD_DIM, (h + 1) * HEAD_DIM)
            y_ref[j * C:(j + 1) * C, sl] = on[hr] * _silu(z_ref[j * C:(j + 1) * C, sl])
    for h in range(N_HEADS):
        s_scr[h] = states[h]

    @pl.when(c_idx == pl.num_programs(1) - 1)
    def _():
        sout_ref[...] = s_scr[...]


def _gdn_call(proj, ba, hist, s0, conv_w, alog, dtb, norm_a, *, n_b, t_rows, C, n_valid, n_sub):
    tile = n_sub * C
    n_c = t_rows // tile
    assert t_rows % tile == 0
    row = lambda b, c: b * n_c + c
    vec_spec = pl.BlockSpec((1, HEAD_DIM), lambda b, c: (0, 0))
    state_spec = pl.BlockSpec((None, N_HEADS, HEAD_DIM, HEAD_DIM), lambda b, c: (b, 0, 0, 0))
    return pl.pallas_call(
        functools.partial(_gdn_kernel, C=C, n_valid=n_valid, n_sub=n_sub),
        grid=(n_b, n_c),
        in_specs=[pl.BlockSpec((tile, 3 * BRANCH_W), lambda b, c: (row(b, c), 0)),
                  pl.BlockSpec((None, SUBLANES, 3 * BRANCH_W), lambda b, c: (b, 0, 0)),
                  pl.BlockSpec((tile, BA_W), lambda b, c: (row(b, c), 0)),
                  pl.BlockSpec((tile, BRANCH_W), lambda b, c: (row(b, c), COL_Z_A)),
                  state_spec,
                  pl.BlockSpec((CONV_W, 3 * BRANCH_W), lambda b, c: (0, 0)),
                  vec_spec, vec_spec, vec_spec],
        out_specs=[pl.BlockSpec((tile, BRANCH_W), lambda b, c: (row(b, c), 0)), state_spec],
        out_shape=[jax.ShapeDtypeStruct((n_b * t_rows, BRANCH_W), F32),
                   jax.ShapeDtypeStruct((n_b, N_HEADS, HEAD_DIM, HEAD_DIM), F32)],
        scratch_shapes=[pltpu.VMEM((N_HEADS, HEAD_DIM, HEAD_DIM), F32),
                        pltpu.VMEM((SUBLANES, 3 * BRANCH_W), F32)],
        compiler_params=_params("arbitrary", "arbitrary"),
        name="gdn",
    )(proj, hist, ba, proj, s0, conv_w, alog, dtb, norm_a)


def _ret_kernel(q_ref, k_ref, v_ref, z_ref, cos_ref, sin_ref, s0_ref, dmat_ref, qdec_ref, kdec_ref,
                cdec_ref, nc_ref, y_ref, sout_ref, s_scr):
    c_idx = pl.program_id(1)

    @pl.when(c_idx == 0)
    def _():
        s_scr[...] = s0_ref[...]

    cos, sin = cos_ref[...], sin_ref[...]
    for h in range(N_HEADS):
        sl = slice(h * HEAD_DIM, (h + 1) * HEAD_DIM)
        q = q_ref[:, sl]
        k = k_ref[:, sl]
        v = v_ref[:, sl]
        qr = q * cos + pltpu.roll(q, HEAD_DIM // 2, 1) * sin
        kr = (k * cos + pltpu.roll(k, HEAD_DIM // 2, 1) * sin) * QK_SCALE
        o_intra = _bdot(_bdot_g(qr, kr, NT) * dmat_ref[h], v)
        kv = _bdot_g(kr * kdec_ref[h], v, TN)
        st = s_scr[h]
        o = o_intra + _bdot(qr * qdec_ref[h], st)
        s_scr[h] = st * cdec_ref[h] + kv
        on = o * lax.rsqrt(jnp.mean(o * o, axis=-1, keepdims=True) + EPS) * nc_ref[...]
        y_ref[:, sl] = on * _silu(z_ref[:, sl])

    @pl.when(c_idx == pl.num_programs(1) - 1)
    def _():
        sout_ref[...] = s_scr[...]


def _ret_tables(C, n_valid):
    log_g = jnp.log1p(-(2.0 ** (-5.0 - jnp.arange(N_HEADS, dtype=F32))))
    idx = jnp.arange(C, dtype=F32)
    tri = jnp.tril(jnp.ones((C, C), dtype=bool))
    dmat = jnp.where(tri, jnp.exp(log_g[:, None, None] * jnp.where(tri, idx[:, None] - idx[None, :], 0.0)), 0.0)
    q_dec = jnp.exp(log_g[:, None] * (idx + 1.0))[..., None]
    k_dec = jnp.exp(log_g[:, None] * (n_valid - 1.0 - idx))[..., None]
    c_dec = jnp.exp(log_g * n_valid)[:, None, None]
    bc = lambda t, r: jnp.broadcast_to(t, (N_HEADS, r, HEAD_DIM))
    return dmat, bc(q_dec, C), bc(k_dec, C), bc(c_dec, 1)


def _ret_call(proj, cos2, sin2, s0, norm_c, *, n_b, t_rows, C, n_valid):
    n_c = t_rows // C
    row = lambda b, c: b * n_c + c
    dmat, q_dec, k_dec, c_dec = _ret_tables(C, n_valid)
    col_spec = lambda col: pl.BlockSpec((C, BRANCH_W), lambda b, c: (row(b, c), col))
    tab_spec = pl.BlockSpec((C, HEAD_DIM), lambda b, c: (c, 0))
    state_spec = pl.BlockSpec((None, N_HEADS, HEAD_DIM, HEAD_DIM), lambda b, c: (b, 0, 0, 0))
    const_spec = lambda r, w: pl.BlockSpec((N_HEADS, r, w), lambda b, c: (0, 0, 0))
    return pl.pallas_call(
        _ret_kernel,
        grid=(n_b, n_c),
        in_specs=[col_spec(COL_Q_C), col_spec(COL_K_C), col_spec(COL_V_C), col_spec(COL_Z_C),
                  tab_spec, tab_spec, state_spec,
                  const_spec(C, C), const_spec(C, HEAD_DIM), const_spec(C, HEAD_DIM), const_spec(1, HEAD_DIM),
                  pl.BlockSpec((1, HEAD_DIM), lambda b, c: (0, 0))],
        out_specs=[pl.BlockSpec((C, BRANCH_W), lambda b, c: (row(b, c), 0)), state_spec],
        out_shape=[jax.ShapeDtypeStruct((n_b * t_rows, BRANCH_W), F32),
                   jax.ShapeDtypeStruct((n_b, N_HEADS, HEAD_DIM, HEAD_DIM), F32)],
        scratch_shapes=[pltpu.VMEM((N_HEADS, HEAD_DIM, HEAD_DIM), F32)],
        compiler_params=_params("arbitrary", "arbitrary"),
        name="ret",
    )(proj, proj, proj, proj, cos2, sin2, s0, dmat, q_dec, k_dec, c_dec, norm_c)


def _topk_mask(gate, valid, axis):
    idx = lax.broadcasted_iota(jnp.int32, gate.shape, axis)
    gm = jnp.where(valid, gate, -jnp.inf)
    rank = jnp.zeros(gate.shape, jnp.int32)
    for m in range(gate.shape[axis]):
        gmm = gm[m:m + 1, :] if axis == 0 else gm[:, m:m + 1]
        beats = (gmm > gm) | ((gmm == gm) & (idx > m))
        rank = rank + jnp.where(beats, 1, 0)
    return valid & (rank < MOBA_TOPK)


def _moba_p_kernel(q_ref, k_ref, v_ref, z_ref, y_ref, kmean_scr, kb_scr, vt_scr, bias_scr, acc_scr, *, n_blk):
    qi = pl.program_id(1)
    blk = MOBA_BLOCK
    heads = [slice(h * HEAD_DIM, (h + 1) * HEAD_DIM) for h in range(N_HEADS)]

    @pl.when(qi == 0)
    def _():
        def prep(n, carry):
            rows = pl.ds(pl.multiple_of(n * blk, blk), blk)
            for h, sl in enumerate(heads):
                kn = k_ref[rows, sl]
                kmean_scr[h, pl.ds(n, 1), :] = jnp.sum(kn, axis=0, keepdims=True) * (1.0 / blk)
                kb_scr[h * n_blk + n] = kn.astype(BF16)
                vt_scr[h * n_blk + n] = v_ref[rows, sl].T.astype(BF16)
            return carry

        lax.fori_loop(0, n_blk, prep, 0)

    key_i = lax.broadcasted_iota(jnp.int32, (blk, blk), 0)
    qry_i = lax.broadcasted_iota(jnp.int32, (blk, blk), 1)
    blk_id = lax.broadcasted_iota(jnp.int32, (n_blk, blk), 0)
    qs = [q_ref[:, sl] for sl in heads]
    qbs = [q.astype(BF16) for q in qs]
    ss = [jnp.where(key_i <= qry_i,
                    lax.dot_general(kb_scr[h * n_blk + qi], qbs[h], NT, preferred_element_type=F32), NEG)
          for h in range(N_HEADS)]
    ms = [jnp.max(s, axis=0, keepdims=True) for s in ss]
    ps = [jnp.exp(ss[h] - ms[h]) for h in range(N_HEADS)]
    ls = [jnp.sum(p, axis=0, keepdims=True) for p in ps]
    for h in range(N_HEADS):
        acc_scr[h] = jnp.dot(vt_scr[h * n_blk + qi], ps[h].astype(BF16), preferred_element_type=F32)
    for h in range(N_HEADS):
        gate = lax.dot_general(kmean_scr[h], qs[h], NT, precision=HI, preferred_element_type=F32)
        bias_scr[h] = jnp.where(_topk_mask(gate, blk_id < qi, 0), 0.0, NEG)

    def body(n, carry):
        ms, ls = carry
        ss = [lax.dot_general(kb_scr[h * n_blk + n], qbs[h], NT, preferred_element_type=F32)
              + bias_scr[h, pl.ds(n, 1), :] for h in range(N_HEADS)]
        ms_new = [jnp.maximum(ms[h], jnp.max(ss[h], axis=0, keepdims=True)) for h in range(N_HEADS)]
        ps = [jnp.exp(ss[h] - ms_new[h]) for h in range(N_HEADS)]
        alphas = [jnp.exp(ms[h] - ms_new[h]) for h in range(N_HEADS)]
        pvs = [jnp.dot(vt_scr[h * n_blk + n], ps[h].astype(BF16), preferred_element_type=F32)
               for h in range(N_HEADS)]
        ls_new = [alphas[h] * ls[h] + jnp.sum(ps[h], axis=0, keepdims=True) for h in range(N_HEADS)]
        for h in range(N_HEADS):
            acc_scr[h] = alphas[h] * acc_scr[h] + pvs[h]
        return tuple(ms_new), tuple(ls_new)

    _, ls = lax.fori_loop(0, qi, body, (tuple(ms), tuple(ls)))
    for h, sl in enumerate(heads):
        y_ref[:, sl] = (acc_scr[h] / ls[h]).T * _silu(z_ref[:, sl])


def _moba_p_call(q_rot, k_rot, proj, *, n_b, t_rows):
    n_blk = t_rows // MOBA_BLOCK
    assert t_rows % MOBA_BLOCK == 0 and n_blk >= MOBA_TOPK
    blk = MOBA_BLOCK
    seq_spec = lambda col: pl.BlockSpec((t_rows, BRANCH_W), lambda b, i: (b, col), pipeline_mode=pl.Buffered(1))
    tile_spec = lambda col: pl.BlockSpec((blk, BRANCH_W), lambda b, i: (b * n_blk + i, col))
    return pl.pallas_call(
        functools.partial(_moba_p_kernel, n_blk=n_blk),
        grid=(n_b, n_blk),
        in_specs=[tile_spec(0), seq_spec(0), seq_spec(COL_V_B), tile_spec(COL_Z_B)],
        out_specs=tile_spec(0),
        out_shape=jax.ShapeDtypeStruct((n_b * t_rows, BRANCH_W), F32),
        scratch_shapes=[pltpu.VMEM((N_HEADS, n_blk, HEAD_DIM), F32),
                        pltpu.VMEM((N_HEADS * n_blk, blk, HEAD_DIM), BF16),
                        pltpu.VMEM((N_HEADS * n_blk, HEAD_DIM, blk), BF16),
                        pltpu.VMEM((N_HEADS, n_blk, blk), F32),
                        pltpu.VMEM((N_HEADS, HEAD_DIM, blk), F32)],
        compiler_params=_params("arbitrary", "arbitrary"),
        name="moba_p",
    )(q_rot, k_rot, proj, proj)


PAGES_PER_STEP = 8


def _moba_s_kernel(pt_ref, q_ref, kn_ref, vn_ref, z_ref, *refs, n_valid, pages_per_blk, pps):
    del pt_ref
    k_refs, v_refs = refs[:pps], refs[pps:2 * pps]
    y_ref, q_scr, ksum_scr, m_scr, l_scr, o_scr = refs[2 * pps:]
    step = pl.program_id(1)
    rows = N_HEADS * SUBLANES
    page_rows = k_refs[0].shape[0]
    heads = [slice(h * HEAD_DIM, (h + 1) * HEAD_DIM) for h in range(N_HEADS)]
    hrows = [slice(h * SUBLANES, (h + 1) * SUBLANES) for h in range(N_HEADS)]

    @pl.when(step == 0)
    def _():
        q = q_ref[...]
        q_scr[...] = jnp.concatenate([q[:, sl] for sl in heads], axis=0)

    q32 = q_scr[...]
    qb = q32.astype(BF16)
    row_head = lax.broadcasted_iota(jnp.int32, (rows, page_rows), 0) // SUBLANES
    col_head = lax.broadcasted_iota(jnp.int32, (rows, page_rows), 1) % N_HEADS
    head_bias = jnp.where(row_head == col_head, 0.0, NEG)

    blks_per_step = pps // pages_per_blk
    kps = [k_refs[j][...] for j in range(pps)]
    ss = [_bdot_g(qb, kp, NT) + head_bias for kp in kps]
    ms = [jnp.max(s, axis=1, keepdims=True) for s in ss]
    es = [jnp.exp(s - m) for s, m in zip(ss, ms)]
    os_ = [_bdot(e, v_refs[j][...]) for j, e in enumerate(es)]
    for j in range(pps):
        pg = step * pps + j
        m_scr[pg] = jnp.broadcast_to(ms[j], (rows, HEAD_DIM))
        l_scr[pg] = jnp.broadcast_to(jnp.sum(es[j], axis=1, keepdims=True), (rows, HEAD_DIM))
        o_scr[pg] = os_[j]
    parts = [jnp.sum(kp.reshape(page_rows // SUBLANES, SUBLANES, HEAD_DIM), axis=0) for kp in kps]
    for bl in range(blks_per_step):
        ksum = parts[bl * pages_per_blk]
        for j in range(bl * pages_per_blk + 1, (bl + 1) * pages_per_blk):
            ksum = ksum + parts[j]
        ksum_scr[step * blks_per_step + bl] = ksum

    @pl.when(step == pl.num_programs(1) - 1)
    def _():
        n_pg = m_scr.shape[0]
        n_blk = ksum_scr.shape[0]
        kflat = ksum_scr[...].reshape(n_blk * SUBLANES, HEAD_DIM) * (1.0 / MOBA_BLOCK)
        g_all = lax.dot_general(q32, kflat, NT, precision=HI, preferred_element_type=F32)
        rh = lax.broadcasted_iota(jnp.int32, g_all.shape, 0) // SUBLANES
        ch = lax.broadcasted_iota(jnp.int32, g_all.shape, 1) % N_HEADS
        pool = (lax.broadcasted_iota(jnp.int32, (n_blk * SUBLANES, n_blk), 0) // SUBLANES
                == lax.broadcasted_iota(jnp.int32, (n_blk * SUBLANES, n_blk), 1)).astype(F32)
        gate = _hdot(jnp.where(rh == ch, g_all, 0.0), pool)
        sel_f = jnp.where(_topk_mask(gate, jnp.ones(gate.shape, jnp.bool_), 1), 1.0, 0.0)
        selw = [jnp.broadcast_to(sel_f[:, n:n + 1], (rows, HEAD_DIM)) > 0.5 for n in range(n_blk)]

        kn, vn = kn_ref[...], vn_ref[...]
        s_own = jnp.concatenate([_bdot_g(q32[hr], kn[:, sl], NT) for hr, sl in zip(hrows, heads)], axis=0)
        rq = lax.broadcasted_iota(jnp.int32, s_own.shape, 0) % SUBLANES
        cj = lax.broadcasted_iota(jnp.int32, s_own.shape, 1)
        own_ok = (cj <= rq) & (cj < n_valid)
        s_own = jnp.where(own_ok, s_own, NEG)
        mx = jnp.broadcast_to(jnp.max(s_own, axis=1, keepdims=True), (rows, HEAD_DIM))
        for pg in range(n_pg):
            mx = jnp.maximum(mx, jnp.where(selw[pg // pages_per_blk], m_scr[pg], NEG))
        e_own = jnp.where(own_ok, jnp.exp(s_own - mx[:, :SUBLANES]), 0.0)
        l_tot = jnp.broadcast_to(jnp.sum(e_own, axis=1, keepdims=True), (rows, HEAD_DIM))
        o_tot = jnp.concatenate([_bdot(e_own[hr], vn[:, sl]) for hr, sl in zip(hrows, heads)], axis=0)
        for pg in range(n_pg):
            wgt = jnp.where(selw[pg // pages_per_blk], jnp.exp(jnp.minimum(m_scr[pg] - mx, 0.0)), 0.0)
            l_tot = l_tot + wgt * l_scr[pg]
            o_tot = o_tot + wgt * o_scr[pg]
        o = o_tot / l_tot
        y_ref[...] = jnp.concatenate([o[hr] for hr in hrows], axis=1) * _silu(z_ref[...])


def _moba_s_call(page_table, q_rot, k_rot, proj, cache_k, cache_v, layer, *, n_valid):
    n_b, n_pages = page_table.shape
    depth, n_pool, page = cache_k.shape[:3]
    assert MOBA_BLOCK % page == 0
    pages_per_blk = MOBA_BLOCK // page
    pps = PAGES_PER_STEP
    assert pps % pages_per_blk == 0 and n_pages % pps == 0 and n_pages // pages_per_blk >= MOBA_TOPK
    assert (page * N_HEADS) % SUBLANES == 0 and SUBLANES % N_HEADS == 0
    ck = cache_k.reshape(depth, n_pool, page * N_HEADS, HEAD_DIM)
    cv = cache_v.reshape(depth, n_pool, page * N_HEADS, HEAD_DIM)
    rows = N_HEADS * SUBLANES
    row_spec = lambda col: pl.BlockSpec((SUBLANES, BRANCH_W), lambda b, p, pt: (b, col))

    def page_spec(j):
        return pl.BlockSpec((None, None, page * N_HEADS, HEAD_DIM),
                            lambda b, p, pt: (layer, pt[b, p * pps + j], 0, 0))

    page_specs = [page_spec(j) for j in range(pps)]
    grid_spec = pltpu.PrefetchScalarGridSpec(
        num_scalar_prefetch=1,
        grid=(n_b, n_pages // pps),
        in_specs=[row_spec(0), row_spec(0), row_spec(COL_V_B), row_spec(COL_Z_B)] + page_specs + page_specs,
        out_specs=pl.BlockSpec((SUBLANES, BRANCH_W), lambda b, p, pt: (b, 0)),
        scratch_shapes=[pltpu.VMEM((rows, HEAD_DIM), F32),
                        pltpu.VMEM((n_pages // pages_per_blk, SUBLANES, HEAD_DIM), F32),
                        pltpu.VMEM((n_pages, rows, HEAD_DIM), F32),
                        pltpu.VMEM((n_pages, rows, HEAD_DIM), F32),
                        pltpu.VMEM((n_pages, rows, HEAD_DIM), F32)])
    return pl.pallas_call(
        functools.partial(_moba_s_kernel, n_valid=n_valid, pages_per_blk=pages_per_blk, pps=pps),
        grid_spec=grid_spec,
        out_shape=jax.ShapeDtypeStruct((n_b * SUBLANES, BRANCH_W), F32),
        compiler_params=_params("arbitrary", "arbitrary"),
        name="moba_s",
    )(page_table, q_rot, k_rot, proj, proj, *([ck] * pps), *([cv] * pps))


def _merge_kernel(ya_ref, yb_ref, yc_ref, mg_ref, x_ref, gate_ref, wb_ref, wo_ref, nf_ref, *out_refs, final):
    d = x_ref.shape[1]
    mixed = None
    for n, y_ref in enumerate((ya_ref, yb_ref, yc_ref)):
        per_branch = _bdot(y_ref[...], wb_ref[n])
        term = jax.nn.sigmoid(mg_ref[:, n * d:(n + 1) * d]) * per_branch
        mixed = term if mixed is None else mixed + term
    x_out = x_ref[...] + gate_ref[...] * _bdot(mixed, wo_ref[...])
    out_refs[0][...] = x_out
    if final:
        out_refs[1][...] = (x_out * lax.rsqrt(jnp.mean(x_out * x_out, axis=-1, keepdims=True) + EPS)
                            * nf_ref[...])


def _merge_call(y_a, y_b, y_c, proj, x2d, gate, w_branch, w_out, norm_f, *, tm, rows_per_mod, final):
    m, d = x2d.shape
    if gate.ndim == 3:
        gate_spec = pl.BlockSpec((None, 1, d), lambda i: ((i * tm) // rows_per_mod, 0, 0))
    else:
        gate_spec = pl.BlockSpec((tm, d), lambda i: (i, 0))
    y_spec = pl.BlockSpec((tm, BRANCH_W), lambda i: (i, 0))
    x_spec = pl.BlockSpec((tm, d), lambda i: (i, 0))
    n_out = 2 if final else 1
    outs = pl.pallas_call(
        functools.partial(_merge_kernel, final=final),
        grid=(m // tm,),
        in_specs=[y_spec, y_spec, y_spec,
                  pl.BlockSpec((tm, N_BRANCH * d), lambda i: (i, (COL_MERGE * BRANCH_W) // (N_BRANCH * d))),
                  x_spec, gate_spec,
                  pl.BlockSpec((N_BRANCH, BRANCH_W, d), lambda i: (0, 0, 0)),
                  pl.BlockSpec((d, d), lambda i: (0, 0)),
                  pl.BlockSpec((1, d), lambda i: (0, 0))],
        out_specs=[x_spec] * n_out,
        out_shape=[jax.ShapeDtypeStruct((m, d), F32)] * n_out,
        compiler_params=_params("arbitrary"),
        name="merge",
    )(y_a, y_b, y_c, proj, x2d, gate, w_branch, w_out, norm_f.reshape(1, d))
    return outs


def _rope_tables(pos):
    t = pos.shape[0]
    posf = pos.astype(F32)

    def cos_sin(n_rot, theta):
        half = n_rot // 2
        inv = theta ** (-jnp.arange(half, dtype=F32) / half)
        ang = posf[:, None] * inv[None, :]
        return jnp.cos(ang), jnp.sin(ang)

    cb, sb = cos_sin(ROPE_DIMS, ROPE_THETA)
    hb = ROPE_DIMS // 2
    tab_b = (jnp.concatenate([cb, cb, jnp.ones((t, HEAD_DIM - ROPE_DIMS), F32)], axis=1),
             jnp.concatenate([jnp.zeros((t, hb), F32), sb, jnp.zeros((t, HEAD_DIM - ROPE_DIMS), F32)], axis=1),
             jnp.concatenate([-sb, jnp.zeros((t, HEAD_DIM - hb), F32)], axis=1))
    cc, sc = cos_sin(HEAD_DIM, RET_THETA)
    tab_c = (jnp.concatenate([cc, cc], axis=1), jnp.concatenate([-sc, sc], axis=1))
    return tab_b, tab_c


def _pad_rows(a2d, n_b, t, t_pad):
    w = a2d.shape[1]
    return jnp.pad(a2d.reshape(n_b, t, w), ((0, 0), (0, t_pad - t), (0, 0))).reshape(n_b * t_pad, w)


def _alpha_lanes(v):
    return jnp.pad(v.astype(F32), (N_HEADS, BA_W - 2 * N_HEADS)).reshape(1, BA_W)


def kernel(x_prompt, x_sample, cache_k, cache_v, state_gdn, state_conv, state_ret, page_table, c_prompt, c_sample,
           norm_in, w_ada, b_ada, w_in, conv_w, a_log, dt_bias, norm_a, norm_c, w_branch, w_out, norm_f):
    n_b, seq, d = x_prompt.shape
    n_db, dec_seq, _ = x_sample.shape
    depth = w_in.shape[0]
    n_pages = page_table.shape[1]
    past_len = n_pages * cache_k.shape[2]
    assert d == 2 * BRANCH_W and dec_seq <= SUBLANES and dec_seq >= CONV_W - 1
    assert seq % GDN_CHUNK == 0 and seq % RET_CHUNK == 0
    t_pad = SUBLANES

    ba0 = 4 * BRANCH_W
    w_main = jnp.concatenate([w_in[:, :, :ba0], w_in[:, :, ba0 + 2 * N_HEADS:]], axis=2).astype(BF16)
    w_ba = jnp.pad(w_in[:, :, ba0:ba0 + 2 * N_HEADS], ((0, 0), (0, 0), (0, BA_W - 2 * N_HEADS))).astype(BF16)
    w_ada_b = w_ada.astype(BF16)
    w_branch_b = w_branch.astype(BF16)
    w_out_b = w_out.astype(BF16)

    n_c = n_b + n_db
    c_rows = -(-n_c // SUBLANES) * SUBLANES
    c_all = jnp.pad(jnp.concatenate([c_prompt, c_sample], axis=0), ((0, c_rows - n_c), (0, 0)))
    mods = _mod_call(c_all, w_ada_b, b_ada)

    tab_b_p, tab_c_p = _rope_tables(jnp.arange(seq, dtype=jnp.int32))
    tab_b_s, tab_c_s = _rope_tables(past_len + jnp.arange(t_pad, dtype=jnp.int32))

    zeros_state = jnp.zeros((n_b, N_HEADS, HEAD_DIM, HEAD_DIM), F32)
    zeros_hist = jnp.zeros((n_b, SUBLANES, 3 * BRANCH_W), F32)

    xp = x_prompt.reshape(n_b * seq, d)
    xs = x_sample.reshape(n_db * dec_seq, d)
    outs = {k: [] for k in ("kp", "vp", "ks", "vs", "gp", "gs", "cp", "cs", "rp", "rs")}
    y_p = y_s = None
    for l in range(depth):
        final = l == depth - 1
        alog, dtb = _alpha_lanes(a_log[l]), _alpha_lanes(dt_bias[l])
        na, nc = norm_a[l].reshape(1, HEAD_DIM), norm_c[l].reshape(1, HEAD_DIM)
        shift, scale, gate = jnp.split(mods[l], 3, axis=-1)

        mod_p = [t[:n_b].reshape(n_b, 1, d) for t in (scale, shift, gate)]
        proj, ba = _inproj_call(xp, mod_p[0], mod_p[1], norm_in[l], w_main[l], w_ba[l],
                                tm=min(seq, 1024), rows_per_mod=seq)
        q_rot, k_rot = _rope_b_call(proj, tab_b_p, tr=min(seq, 512), t_rows=seq)
        y_a, gdn_new = _gdn_call(proj, ba, zeros_hist, zeros_state, conv_w[l], alog, dtb, na,
                                 n_b=n_b, t_rows=seq, C=GDN_CHUNK, n_valid=GDN_CHUNK, n_sub=GDN_CHUNKS_PER_STEP)
        y_c, ret_new = _ret_call(proj, tab_c_p[0], tab_c_p[1], zeros_state, nc,
                                 n_b=n_b, t_rows=seq, C=RET_CHUNK, n_valid=RET_CHUNK)
        y_b = _moba_p_call(q_rot, k_rot, proj, n_b=n_b, t_rows=seq)
        res = _merge_call(y_a, y_b, y_c, proj, xp, mod_p[2], w_branch_b[l], w_out_b[l], norm_f,
                          tm=min(seq, 256), rows_per_mod=seq, final=final)
        xp = res[0]
        if final:
            y_p = res[1]
        proj3 = proj.reshape(n_b, seq, MAIN_W)
        outs["kp"].append(k_rot.reshape(n_b, seq, N_HEADS, HEAD_DIM))
        outs["vp"].append(proj3[:, :, COL_V_B * BRANCH_W:(COL_V_B + 1) * BRANCH_W].reshape(n_b, seq, N_HEADS, HEAD_DIM))
        outs["gp"].append(gdn_new)
        outs["cp"].append(proj3[:, seq - (CONV_W - 1):, :3 * BRANCH_W])
        outs["rp"].append(ret_new)

        mod_s = [jnp.repeat(t[n_b:n_c], dec_seq, axis=0) for t in (scale, shift, gate)]
        proj_s, ba_s = _inproj_call(xs, mod_s[0], mod_s[1], norm_in[l], w_main[l], w_ba[l],
                                    tm=n_db * dec_seq, rows_per_mod=dec_seq)
        proj_sp = _pad_rows(proj_s, n_db, dec_seq, t_pad)
        ba_sp = _pad_rows(ba_s, n_db, dec_seq, t_pad)
        hist = jnp.pad(state_conv[l], ((0, 0), (SUBLANES - (CONV_W - 1), 0), (0, 0)))
        q_rot_s, k_rot_s = _rope_b_call(proj_sp, tab_b_s, tr=t_pad, t_rows=t_pad)
        y_a_s, gdn_new_s = _gdn_call(proj_sp, ba_sp, hist, state_gdn[l], conv_w[l], alog, dtb, na,
                                     n_b=n_db, t_rows=t_pad, C=t_pad, n_valid=dec_seq, n_sub=1)
        y_c_s, ret_new_s = _ret_call(proj_sp, tab_c_s[0], tab_c_s[1], state_ret[l], nc,
                                     n_b=n_db, t_rows=t_pad, C=t_pad, n_valid=dec_seq)
        y_b_s = _moba_s_call(page_table, q_rot_s, k_rot_s, proj_sp, cache_k, cache_v, l, n_valid=dec_seq)
        unpad = lambda y: y.reshape(n_db, t_pad, BRANCH_W)[:, :dec_seq].reshape(n_db * dec_seq, BRANCH_W)
        res_s = _merge_call(unpad(y_a_s), unpad(y_b_s), unpad(y_c_s), proj_s, xs, mod_s[2],
                            w_branch_b[l], w_out_b[l], norm_f, tm=n_db * dec_seq, rows_per_mod=dec_seq, final=final)
        xs = res_s[0]
        if final:
            y_s = res_s[1]
        proj_s3 = proj_s.reshape(n_db, dec_seq, MAIN_W)
        outs["ks"].append(k_rot_s.reshape(n_db, t_pad, N_HEADS, HEAD_DIM)[:, :dec_seq])
        outs["vs"].append(proj_s3[:, :, COL_V_B * BRANCH_W:(COL_V_B + 1) * BRANCH_W]
                          .reshape(n_db, dec_seq, N_HEADS, HEAD_DIM))
        outs["gs"].append(gdn_new_s)
        outs["cs"].append(proj_s3[:, dec_seq - (CONV_W - 1):, :3 * BRANCH_W])
        outs["rs"].append(ret_new_s)

    st = {k: jnp.stack(v) for k, v in outs.items()}
    return (y_p.reshape(n_b, seq, d), y_s.reshape(n_db, dec_seq, d),
            st["kp"], st["vp"], st["ks"], st["vs"], st["gp"], st["gs"],
            st["cp"], st["cs"], st["rp"], st["rs"])
```

```python
import functools
import math

import jax
import jax.numpy as jnp
from jax import lax
from jax.experimental import pallas as pl
from jax.experimental.pallas import tpu as pltpu

F32 = jnp.float32
BF16 = jnp.bfloat16
HI = lax.Precision.HIGHEST

HEAD_DIM = 128
N_HEADS = 4
BRANCH_W = N_HEADS * HEAD_DIM
N_BRANCH = 3
CONV_W = 4
GDN_CHUNK = 64
RET_CHUNK = 64
MOBA_BLOCK = 256
MOBA_TOPK = 3
ROPE_THETA = 500000.0
ROPE_DIMS = HEAD_DIM // 4
RET_THETA = 10000.0
EPS = 1e-6
NEG = -1e30
SUBLANES = 8
QK_SCALE = HEAD_DIM ** -0.5

COL_QKV_A, COL_Z_A, COL_Q_B, COL_K_B, COL_V_B, COL_Z_B = 0, 3, 4, 5, 6, 7
COL_Q_C, COL_K_C, COL_V_C, COL_Z_C, COL_MERGE = 8, 9, 10, 11, 12
MAIN_W = 18 * BRANCH_W
BA_W = 128

NT = (((1,), (1,)), ((), ()))
TN = (((0,), (0,)), ((), ()))

VMEM_LIMIT = 48 * 1024 * 1024


def _params(*sem):
    return pltpu.CompilerParams(dimension_semantics=sem, vmem_limit_bytes=VMEM_LIMIT)


def _silu(x):
    return x * jax.nn.sigmoid(x)


def _bdot(a, b):
    return jnp.dot(a.astype(BF16), b.astype(BF16), preferred_element_type=F32)


def _bdot_g(a, b, dims):
    return lax.dot_general(a.astype(BF16), b.astype(BF16), dims, preferred_element_type=F32)


def _hdot(a, b):
    return jnp.dot(a, b, precision=HI, preferred_element_type=F32)


def _mod_kernel(c_ref, w_ref, b_ref, o_ref):
    o_ref[...] = _bdot(_silu(c_ref[...]), w_ref[...]) + b_ref[...]


def _mod_call(c_all, w_ada, b_ada):
    depth, d, d3 = w_ada.shape
    rows = c_all.shape[0]
    tn = d
    return pl.pallas_call(
        _mod_kernel,
        grid=(depth, d3 // tn),
        in_specs=[pl.BlockSpec((rows, d), lambda l, j: (0, 0)),
                  pl.BlockSpec((None, d, tn), lambda l, j: (l, 0, j)),
                  pl.BlockSpec((None, 1, tn), lambda l, j: (l, 0, j))],
        out_specs=pl.BlockSpec((None, rows, tn), lambda l, j: (l, 0, j)),
        out_shape=jax.ShapeDtypeStruct((depth, rows, d3), F32),
        compiler_params=_params("arbitrary", "arbitrary"),
        name="mod",
    )(c_all, w_ada, b_ada.reshape(depth, 1, d3))


def _inproj_kernel(x_ref, sc_ref, sh_ref, g_ref, w_ref, wba_ref, o_ref, ba_ref, h_scr):
    @pl.when(pl.program_id(1) == 0)
    def _():
        x = x_ref[...]
        y = x * lax.rsqrt(jnp.mean(x * x, axis=-1, keepdims=True) + EPS) * g_ref[...]
        h = (y * (1.0 + sc_ref[...]) + sh_ref[...]).astype(BF16)
        h_scr[...] = h
        ba_ref[...] = jnp.dot(h, wba_ref[...], preferred_element_type=F32)

    o_ref[...] = jnp.dot(h_scr[...], w_ref[...], preferred_element_type=F32)


def _inproj_call(x2d, scale, shift, norm_g, w_main, w_ba, *, tm, rows_per_mod):
    m, d = x2d.shape
    tn = 1024
    if scale.ndim == 3:
        mod_spec = pl.BlockSpec((None, 1, d), lambda i, j: ((i * tm) // rows_per_mod, 0, 0))
    else:
        mod_spec = pl.BlockSpec((tm, d), lambda i, j: (i, 0))
    return pl.pallas_call(
        _inproj_kernel,
        grid=(m // tm, MAIN_W // tn),
        in_specs=[pl.BlockSpec((tm, d), lambda i, j: (i, 0)),
                  mod_spec, mod_spec,
                  pl.BlockSpec((1, d), lambda i, j: (0, 0)),
                  pl.BlockSpec((d, tn), lambda i, j: (0, j)),
                  pl.BlockSpec((d, BA_W), lambda i, j: (0, 0))],
        out_specs=[pl.BlockSpec((tm, tn), lambda i, j: (i, j)),
                   pl.BlockSpec((tm, BA_W), lambda i, j: (i, 0))],
        out_shape=[jax.ShapeDtypeStruct((m, MAIN_W), F32),
                   jax.ShapeDtypeStruct((m, BA_W), F32)],
        scratch_shapes=[pltpu.VMEM((tm, d), BF16)],
        compiler_params=_params("arbitrary", "arbitrary"),
        name="in_proj",
    )(x2d, scale, shift, norm_g.reshape(1, d), w_main, w_ba)


def _rope_b_kernel(q_ref, k_ref, c_ref, s1_ref, s2_ref, qo_ref, ko_ref):
    c, s1, s2 = c_ref[...], s1_ref[...], s2_ref[...]
    half = ROPE_DIMS // 2
    for h in range(N_HEADS):
        sl = slice(h * HEAD_DIM, (h + 1) * HEAD_DIM)
        q = q_ref[:, sl]
        k = k_ref[:, sl]
        qr = q * c + pltpu.roll(q, half, 1) * s1 + pltpu.roll(q, HEAD_DIM - half, 1) * s2
        kr = k * c + pltpu.roll(k, half, 1) * s1 + pltpu.roll(k, HEAD_DIM - half, 1) * s2
        qo_ref[:, sl] = qr * QK_SCALE
        ko_ref[:, sl] = kr


def _rope_b_call(proj, tabs, *, tr, t_rows):
    m = proj.shape[0]
    nt = t_rows // tr
    tab_spec = pl.BlockSpec((tr, HEAD_DIM), lambda i: (i % nt, 0))
    return pl.pallas_call(
        _rope_b_kernel,
        grid=(m // tr,),
        in_specs=[pl.BlockSpec((tr, BRANCH_W), lambda i: (i, COL_Q_B)),
                  pl.BlockSpec((tr, BRANCH_W), lambda i: (i, COL_K_B)),
                  tab_spec, tab_spec, tab_spec],
        out_specs=[pl.BlockSpec((tr, BRANCH_W), lambda i: (i, 0)),
                   pl.BlockSpec((tr, BRANCH_W), lambda i: (i, 0))],
        out_shape=[jax.ShapeDtypeStruct((m, BRANCH_W), F32),
                   jax.ShapeDtypeStruct((m, BRANCH_W), F32)],
        compiler_params=_params("arbitrary"),
        name="rope_b",
    )(proj, proj, *tabs)


INV_BASE = 8
GDN_CHUNKS_PER_STEP = 4


def _bdot_each(xs, ys):
    return [_bdot(x, y) for x, y in zip(xs, ys)]


def _unit_lower_inverses(mats, ri, ci, n):
    base = min(INV_BASE, n)
    eye = (ri == ci).astype(F32)
    dpows = [jnp.where((ri // base) == (ci // base), a, 0.0) for a in mats]
    invs = [eye + d for d in dpows]
    for _ in range(int(math.log2(base)) - 1):
        dpows = _bdot_each(dpows, dpows)
        invs = [inv + t for inv, t in zip(invs, _bdot_each(invs, dpows))]
    size = base
    while size < n:
        off = ((ri // (2 * size)) == (ci // (2 * size))) & ((ri // size) != (ci // size))
        es = [jnp.where(off, a, 0.0) for a in mats]
        invs = [inv + t for inv, t in zip(invs, _bdot_each(invs, _bdot_each(es, invs)))]
        size *= 2
    return invs


def _gdn_kernel(qkv_ref, hist_ref, ba_ref, z_ref, s0_ref, cw_ref, alog_ref, dtb_ref, na_ref,
                y_ref, sout_ref, s_scr, tail_scr, *, C, n_valid, n_sub):
    c_idx = pl.program_id(1)
    rows = n_sub * C

    @pl.when(c_idx == 0)
    def _():
        s_scr[...] = s0_ref[...]
        tail_scr[...] = hist_ref[...]

    u = qkv_ref[...]
    prev = tail_scr[...]
    w = cw_ref[...]
    acc = u * w[CONV_W - 1:CONV_W, :]
    row8 = lax.broadcasted_iota(jnp.int32, (SUBLANES, u.shape[1]), 0)
    for s in range(1, CONV_W):
        rolled = pltpu.roll(u, s, 0)
        first = jnp.where(row8 < s, pltpu.roll(prev, s, 0), rolled[:SUBLANES])
        shifted = first if rows == SUBLANES else jnp.concatenate([first, rolled[SUBLANES:]], axis=0)
        acc = acc + shifted * w[CONV_W - 1 - s:CONV_W - s, :]
    tail_scr[...] = u[rows - SUBLANES:, :]
    conv = _silu(acc)

    ba = ba_ref[...]
    beta_t = jax.nn.sigmoid(ba)
    xg = ba + dtb_ref[...]
    g_t = -jnp.exp(alog_ref[...]) * (jnp.maximum(xg, 0.0) + jnp.log1p(jnp.exp(-jnp.abs(xg))))
    row_t = lax.broadcasted_iota(jnp.int32, ba.shape, 0) % C
    if n_valid < C:
        beta_t = jnp.where(row_t < n_valid, beta_t, 0.0)
        g_t = jnp.where(row_t < n_valid, g_t, 0.0)
    gc_t = g_t
    shift = 1
    while shift < C:
        gc_t = gc_t + jnp.where(row_t >= shift, pltpu.roll(gc_t, shift, 0), 0.0)
        shift *= 2

    cs = N_HEADS * C
    subs = range(n_sub)
    stack = lambda f: jnp.concatenate([f(h) for h in range(N_HEADS)], axis=0)
    ri = lax.broadcasted_iota(jnp.int32, (cs, cs), 0)
    ci = lax.broadcasted_iota(jnp.int32, (cs, cs), 1)
    same_head = (ri // C) == (ci // C)
    tri = same_head & (ri >= ci)
    strict = same_head & (ri > ci)

    qn, kn, kb, vb, gc, g_last, decay = [], [], [], [], [], [], []
    for j in subs:
        r0 = j * C
        lanes = lambda t, col: jnp.broadcast_to(t[r0:r0 + C, col:col + 1], (C, HEAD_DIM))
        head_cols = lambda base: stack(lambda h: conv[r0:r0 + C, base + h * HEAD_DIM:base + (h + 1) * HEAD_DIM])
        q, k, v = head_cols(0), head_cols(BRANCH_W), head_cols(2 * BRANCH_W)
        beta = stack(lambda h: lanes(beta_t, h))
        gc_j = stack(lambda h: lanes(gc_t, N_HEADS + h))
        if cs % HEAD_DIM == 0:
            gc_row = jnp.concatenate([gc_j.T] * (cs // HEAD_DIM), axis=0)
            gc_col = jnp.concatenate([gc_j] * (cs // HEAD_DIM), axis=1)
        else:
            g_st = stack(lambda h: lanes(g_t, N_HEADS + h))
            upper = (same_head & (ri <= ci)).astype(F32)
            gc_row = _hdot(jnp.ones((cs, cs), F32), g_st[:, :cs] * upper)
            gc_col = gc_j[:, :cs]
        kn_j = k * lax.rsqrt(jnp.sum(k * k, axis=-1, keepdims=True) + EPS)
        qn.append(q * lax.rsqrt(jnp.sum(q * q, axis=-1, keepdims=True) + EPS) * QK_SCALE)
        kn.append(kn_j)
        kb.append(kn_j * beta)
        vb.append(v * beta)
        gc.append(gc_j)
        g_last.append(stack(lambda h: jnp.broadcast_to(
            gc_t[r0 + C - 1:r0 + C, N_HEADS + h:N_HEADS + h + 1], (C, HEAD_DIM))))
        decay.append(jnp.where(tri, jnp.exp(jnp.where(tri, gc_col - gc_row, 0.0)), 0.0))

    kk = [_bdot_g(kb[j], kn[j], NT) for j in subs]
    qk = [_bdot_g(qn[j], kn[j], NT) for j in subs]
    invs = _unit_lower_inverses([-jnp.where(strict, kk[j] * decay[j], 0.0) for j in subs], ri, ci, C)
    eg = [jnp.exp(gc[j]) for j in subs]
    sols = _bdot_each(invs, [jnp.concatenate([vb[j], kb[j] * eg[j]], axis=1) for j in subs])
    attn = [qk[j] * decay[j] for j in subs]
    q_dec = [qn[j] * eg[j] for j in subs]
    k_dec = [kn[j] * jnp.exp(g_last[j] - gc[j]) for j in subs]

    hrows = [slice(h * C, (h + 1) * C) for h in range(N_HEADS)]
    states = [s_scr[h] for h in range(N_HEADS)]
    for j in subs:
        u_s, w_s = sols[j][:, :HEAD_DIM], sols[j][:, HEAD_DIM:]
        v_new = jnp.concatenate([u_s[hr] - _bdot(w_s[hr], st) for hr, st in zip(hrows, states)], axis=0)
        o = (jnp.concatenate([_bdot(q_dec[j][hr], st) for hr, st in zip(hrows, states)], axis=0)
             + _bdot(attn[j], v_new))
        states = [st * jnp.exp(g_last[j][h * C:h * C + 1, :]) + _bdot_g(k_dec[j][hr], v_new[hr], TN)
                  for h, (hr, st) in enumerate(zip(hrows, states))]
        on = o * lax.rsqrt(jnp.mean(o * o, axis=-1, keepdims=True) + EPS) * na_ref[...]
        for h, hr in enumerate(hrows):
            sl = slice(h * HEAD_DIM, (h + 1) * HEAD_DIM)
            y_ref[j * C:(j + 1) * C, sl] = on[hr] * _silu(z_ref[j * C:(j + 1) * C, sl])
    for h in range(N_HEADS):
        s_scr[h] = states[h]

    @pl.when(c_idx == pl.num_programs(1) - 1)
    def _():
        sout_ref[...] = s_scr[...]


def _gdn_call(proj, ba, hist, s0, conv_w, alog, dtb, norm_a, *, n_b, t_rows, C, n_valid, n_sub):
    tile = n_sub * C
    n_c = t_rows // tile
    assert t_rows % tile == 0
    row = lambda b, c: b * n_c + c
    vec_spec = pl.BlockSpec((1, HEAD_DIM), lambda b, c: (0, 0))
    state_spec = pl.BlockSpec((None, N_HEADS, HEAD_DIM, HEAD_DIM), lambda b, c: (b, 0, 0, 0))
    return pl.pallas_call(
        functools.partial(_gdn_kernel, C=C, n_valid=n_valid, n_sub=n_sub),
        grid=(n_b, n_c),
        in_specs=[pl.BlockSpec((tile, 3 * BRANCH_W), lambda b, c: (row(b, c), 0)),
                  pl.BlockSpec((None, SUBLANES, 3 * BRANCH_W), lambda b, c: (b, 0, 0)),
                  pl.BlockSpec((tile, BA_W), lambda b, c: (row(b, c), 0)),
                  pl.BlockSpec((tile, BRANCH_W), lambda b, c: (row(b, c), COL_Z_A)),
                  state_spec,
                  pl.BlockSpec((CONV_W, 3 * BRANCH_W), lambda b, c: (0, 0)),
                  vec_spec, vec_spec, vec_spec],
        out_specs=[pl.BlockSpec((tile, BRANCH_W), lambda b, c: (row(b, c), 0)), state_spec],
        out_shape=[jax.ShapeDtypeStruct((n_b * t_rows, BRANCH_W), F32),
                   jax.ShapeDtypeStruct((n_b, N_HEADS, HEAD_DIM, HEAD_DIM), F32)],
        scratch_shapes=[pltpu.VMEM((N_HEADS, HEAD_DIM, HEAD_DIM), F32),
                        pltpu.VMEM((SUBLANES, 3 * BRANCH_W), F32)],
        compiler_params=_params("arbitrary", "arbitrary"),
        name="gdn",
    )(proj, hist, ba, proj, s0, conv_w, alog, dtb, norm_a)


RET_CHUNKS_PER_STEP = 4

def _ret_kernel(q_ref, k_ref, v_ref, z_ref, cos_ref, sin_ref, s0_ref, dmat_ref, qdec_ref, kdec_ref,
                cdec_ref, nc_ref, y_ref, sout_ref, s_scr, *, C, n_sub):
    c_idx = pl.program_id(1)

    @pl.when(c_idx == 0)
    def _():
        s_scr[...] = s0_ref[...]

    cos, sin = cos_ref[...], sin_ref[...]
    heads = [slice(h * HEAD_DIM, (h + 1) * HEAD_DIM) for h in range(N_HEADS)]
    units = [(j, h) for j in range(n_sub) for h in range(N_HEADS)]
    rows = lambda j: slice(j * C, (j + 1) * C)
    rope = lambda x: x * cos + pltpu.roll(x, HEAD_DIM // 2, 1) * sin
    qr = [rope(q_ref[:, sl]) for sl in heads]
    kr = [rope(k_ref[:, sl]) * QK_SCALE for sl in heads]
    vs = [v_ref[:, sl] for sl in heads]
    sc = [_bdot_g(qr[h][rows(j)], kr[h][rows(j)], NT) * dmat_ref[h] for j, h in units]
    o_intra = [_bdot(s, vs[h][rows(j)]) for s, (j, h) in zip(sc, units)]
    kv = [_bdot_g(kr[h][rows(j)] * kdec_ref[h], vs[h][rows(j)], TN) for j, h in units]

    states = [s_scr[h] for h in range(N_HEADS)]
    for j in range(n_sub):
        for h, sl in enumerate(heads):
            u = j * N_HEADS + h
            o = o_intra[u] + _bdot(qr[h][rows(j)] * qdec_ref[h], states[h])
            states[h] = states[h] * cdec_ref[h] + kv[u]
            on = o * lax.rsqrt(jnp.mean(o * o, axis=-1, keepdims=True) + EPS) * nc_ref[...]
            y_ref[rows(j), sl] = on * _silu(z_ref[rows(j), sl])
    for h in range(N_HEADS):
        s_scr[h] = states[h]

    @pl.when(c_idx == pl.num_programs(1) - 1)
    def _():
        sout_ref[...] = s_scr[...]


def _ret_tables(C, n_valid):
    log_g = jnp.log1p(-(2.0 ** (-5.0 - jnp.arange(N_HEADS, dtype=F32))))
    idx = jnp.arange(C, dtype=F32)
    tri = jnp.tril(jnp.ones((C, C), dtype=bool))
    dmat = jnp.where(tri, jnp.exp(log_g[:, None, None] * jnp.where(tri, idx[:, None] - idx[None, :], 0.0)), 0.0)
    q_dec = jnp.exp(log_g[:, None] * (idx + 1.0))[..., None]
    k_dec = jnp.exp(log_g[:, None] * (n_valid - 1.0 - idx))[..., None]
    c_dec = jnp.exp(log_g * n_valid)[:, None, None]
    bc = lambda t, r: jnp.broadcast_to(t, (N_HEADS, r, HEAD_DIM))
    return dmat, bc(q_dec, C), bc(k_dec, C), bc(c_dec, 1)


def _ret_call(proj, cos2, sin2, s0, norm_c, *, n_b, t_rows, C, n_valid, n_sub):
    tile = n_sub * C
    n_c = t_rows // tile
    assert t_rows % tile == 0
    row = lambda b, c: b * n_c + c
    dmat, q_dec, k_dec, c_dec = _ret_tables(C, n_valid)
    col_spec = lambda col: pl.BlockSpec((tile, BRANCH_W), lambda b, c: (row(b, c), col))
    tab_spec = pl.BlockSpec((tile, HEAD_DIM), lambda b, c: (c, 0))
    state_spec = pl.BlockSpec((None, N_HEADS, HEAD_DIM, HEAD_DIM), lambda b, c: (b, 0, 0, 0))
    const_spec = lambda r, w: pl.BlockSpec((N_HEADS, r, w), lambda b, c: (0, 0, 0))
    return pl.pallas_call(
        functools.partial(_ret_kernel, C=C, n_sub=n_sub),
        grid=(n_b, n_c),
        in_specs=[col_spec(COL_Q_C), col_spec(COL_K_C), col_spec(COL_V_C), col_spec(COL_Z_C),
                  tab_spec, tab_spec, state_spec,
                  const_spec(C, C), const_spec(C, HEAD_DIM), const_spec(C, HEAD_DIM), const_spec(1, HEAD_DIM),
                  pl.BlockSpec((1, HEAD_DIM), lambda b, c: (0, 0))],
        out_specs=[pl.BlockSpec((tile, BRANCH_W), lambda b, c: (row(b, c), 0)), state_spec],
        out_shape=[jax.ShapeDtypeStruct((n_b * t_rows, BRANCH_W), F32),
                   jax.ShapeDtypeStruct((n_b, N_HEADS, HEAD_DIM, HEAD_DIM), F32)],
        scratch_shapes=[pltpu.VMEM((N_HEADS, HEAD_DIM, HEAD_DIM), F32)],
        compiler_params=_params("arbitrary", "arbitrary"),
        name="ret",
    )(proj, proj, proj, proj, cos2, sin2, s0, dmat, q_dec, k_dec, c_dec, norm_c)


def _topk_mask(gate, valid, axis):
    idx = lax.broadcasted_iota(jnp.int32, gate.shape, axis)
    gm = jnp.where(valid, gate, -jnp.inf)
    rank = jnp.zeros(gate.shape, jnp.int32)
    for m in range(gate.shape[axis]):
        gmm = gm[m:m + 1, :] if axis == 0 else gm[:, m:m + 1]
        beats = (gmm > gm) | ((gmm == gm) & (idx > m))
        rank = rank + jnp.where(beats, 1, 0)
    return valid & (rank < MOBA_TOPK)


def _moba_p_kernel(q_ref, k_ref, v_ref, z_ref, y_ref, kmean_scr, kb_scr, vt_scr, bias_scr, acc_scr, *, n_blk):
    qi = pl.program_id(1)
    blk = MOBA_BLOCK
    heads = [slice(h * HEAD_DIM, (h + 1) * HEAD_DIM) for h in range(N_HEADS)]

    @pl.when(qi == 0)
    def _():
        def prep(n, carry):
            rows = pl.ds(pl.multiple_of(n * blk, blk), blk)
            for h, sl in enumerate(heads):
                kn = k_ref[rows, sl]
                kmean_scr[h, pl.ds(n, 1), :] = jnp.sum(kn, axis=0, keepdims=True) * (1.0 / blk)
                kb_scr[h * n_blk + n] = kn.astype(BF16)
                vt_scr[h * n_blk + n] = v_ref[rows, sl].T.astype(BF16)
            return carry

        lax.fori_loop(0, n_blk, prep, 0)

    key_i = lax.broadcasted_iota(jnp.int32, (blk, blk), 0)
    qry_i = lax.broadcasted_iota(jnp.int32, (blk, blk), 1)
    blk_id = lax.broadcasted_iota(jnp.int32, (n_blk, blk), 0)
    qs = [q_ref[:, sl] for sl in heads]
    qbs = [q.astype(BF16) for q in qs]
    ss = [jnp.where(key_i <= qry_i,
                    lax.dot_general(kb_scr[h * n_blk + qi], qbs[h], NT, preferred_element_type=F32), NEG)
          for h in range(N_HEADS)]
    ms = [jnp.max(s, axis=0, keepdims=True) for s in ss]
    ps = [jnp.exp(ss[h] - ms[h]) for h in range(N_HEADS)]
    ls = [jnp.sum(p, axis=0, keepdims=True) for p in ps]
    for h in range(N_HEADS):
        acc_scr[h] = jnp.dot(vt_scr[h * n_blk + qi], ps[h].astype(BF16), preferred_element_type=F32)
    for h in range(N_HEADS):
        gate = lax.dot_general(kmean_scr[h], qs[h], NT, precision=HI, preferred_element_type=F32)
        bias_scr[h] = jnp.where(_topk_mask(gate, blk_id < qi, 0), 0.0, NEG)

    def body(n, carry):
        ms, ls = carry
        ss = [lax.dot_general(kb_scr[h * n_blk + n], qbs[h], NT, preferred_element_type=F32)
              + bias_scr[h, pl.ds(n, 1), :] for h in range(N_HEADS)]
        ms_new = [jnp.maximum(ms[h], jnp.max(ss[h], axis=0, keepdims=True)) for h in range(N_HEADS)]
        ps = [jnp.exp(ss[h] - ms_new[h]) for h in range(N_HEADS)]
        alphas = [jnp.exp(ms[h] - ms_new[h]) for h in range(N_HEADS)]
        pvs = [jnp.dot(vt_scr[h * n_blk + n], ps[h].astype(BF16), preferred_element_type=F32)
               for h in range(N_HEADS)]
        ls_new = [alphas[h] * ls[h] + jnp.sum(ps[h], axis=0, keepdims=True) for h in range(N_HEADS)]
        for h in range(N_HEADS):
            acc_scr[h] = alphas[h] * acc_scr[h] + pvs[h]
        return tuple(ms_new), tuple(ls_new)

    _, ls = lax.fori_loop(0, qi, body, (tuple(ms), tuple(ls)))
    for h, sl in enumerate(heads):
        y_ref[:, sl] = (acc_scr[h] / ls[h]).T * _silu(z_ref[:, sl])


def _moba_p_call(q_rot, k_rot, proj, *, n_b, t_rows):
    n_blk = t_rows // MOBA_BLOCK
    assert t_rows % MOBA_BLOCK == 0 and n_blk >= MOBA_TOPK
    blk = MOBA_BLOCK
    seq_spec = lambda col: pl.BlockSpec((t_rows, BRANCH_W), lambda b, i: (b, col), pipeline_mode=pl.Buffered(1))
    tile_spec = lambda col: pl.BlockSpec((blk, BRANCH_W), lambda b, i: (b * n_blk + i, col))
    return pl.pallas_call(
        functools.partial(_moba_p_kernel, n_blk=n_blk),
        grid=(n_b, n_blk),
        in_specs=[tile_spec(0), seq_spec(0), seq_spec(COL_V_B), tile_spec(COL_Z_B)],
        out_specs=tile_spec(0),
        out_shape=jax.ShapeDtypeStruct((n_b * t_rows, BRANCH_W), F32),
        scratch_shapes=[pltpu.VMEM((N_HEADS, n_blk, HEAD_DIM), F32),
                        pltpu.VMEM((N_HEADS * n_blk, blk, HEAD_DIM), BF16),
                        pltpu.VMEM((N_HEADS * n_blk, HEAD_DIM, blk), BF16),
                        pltpu.VMEM((N_HEADS, n_blk, blk), F32),
                        pltpu.VMEM((N_HEADS, HEAD_DIM, blk), F32)],
        compiler_params=_params("arbitrary", "arbitrary"),
        name="moba_p",
    )(q_rot, k_rot, proj, proj)


PAGES_PER_STEP = 16


def _moba_s_kernel(pt_ref, q_ref, kn_ref, vn_ref, z_ref, *refs, n_valid, pages_per_blk, pps):
    del pt_ref
    k_refs, v_refs = refs[:pps], refs[pps:2 * pps]
    y_ref, q_scr, ksum_scr, m_scr, l_scr, o_scr = refs[2 * pps:]
    step = pl.program_id(1)
    rows = N_HEADS * SUBLANES
    page_rows = k_refs[0].shape[0]
    heads = [slice(h * HEAD_DIM, (h + 1) * HEAD_DIM) for h in range(N_HEADS)]
    hrows = [slice(h * SUBLANES, (h + 1) * SUBLANES) for h in range(N_HEADS)]

    @pl.when(step == 0)
    def _():
        q = q_ref[...]
        q_scr[...] = jnp.concatenate([q[:, sl] for sl in heads], axis=0)

    q32 = q_scr[...]
    qb = q32.astype(BF16)
    row_head = lax.broadcasted_iota(jnp.int32, (rows, page_rows), 0) // SUBLANES
    col_head = lax.broadcasted_iota(jnp.int32, (rows, page_rows), 1) % N_HEADS
    head_bias = jnp.where(row_head == col_head, 0.0, NEG)

    blks_per_step = pps // pages_per_blk
    kps = [k_refs[j][...] for j in range(pps)]
    ss = [_bdot_g(qb, kp, NT) + head_bias for kp in kps]
    ms = [jnp.max(s, axis=1, keepdims=True) for s in ss]
    es = [jnp.exp(s - m) for s, m in zip(ss, ms)]
    os_ = [_bdot(e, v_refs[j][...]) for j, e in enumerate(es)]
    for j in range(pps):
        pg = step * pps + j
        m_scr[pg] = jnp.broadcast_to(ms[j], (rows, HEAD_DIM))
        l_scr[pg] = jnp.broadcast_to(jnp.sum(es[j], axis=1, keepdims=True), (rows, HEAD_DIM))
        o_scr[pg] = os_[j]
    parts = [jnp.sum(kp.reshape(page_rows // SUBLANES, SUBLANES, HEAD_DIM), axis=0) for kp in kps]
    for bl in range(blks_per_step):
        ksum = parts[bl * pages_per_blk]
        for j in range(bl * pages_per_blk + 1, (bl + 1) * pages_per_blk):
            ksum = ksum + parts[j]
        ksum_scr[step * blks_per_step + bl] = ksum

    @pl.when(step == pl.num_programs(1) - 1)
    def _():
        n_pg = m_scr.shape[0]
        n_blk = ksum_scr.shape[0]
        kflat = ksum_scr[...].reshape(n_blk * SUBLANES, HEAD_DIM) * (1.0 / MOBA_BLOCK)
        g_all = lax.dot_general(q32, kflat, NT, precision=HI, preferred_element_type=F32)
        rh = lax.broadcasted_iota(jnp.int32, g_all.shape, 0) // SUBLANES
        ch = lax.broadcasted_iota(jnp.int32, g_all.shape, 1) % N_HEADS
        pool = (lax.broadcasted_iota(jnp.int32, (n_blk * SUBLANES, n_blk), 0) // SUBLANES
                == lax.broadcasted_iota(jnp.int32, (n_blk * SUBLANES, n_blk), 1)).astype(F32)
        gate = _hdot(jnp.where(rh == ch, g_all, 0.0), pool)
        sel_f = jnp.where(_topk_mask(gate, jnp.ones(gate.shape, jnp.bool_), 1), 1.0, 0.0)
        selw = [jnp.broadcast_to(sel_f[:, n:n + 1], (rows, HEAD_DIM)) > 0.5 for n in range(n_blk)]

        kn, vn = kn_ref[...], vn_ref[...]
        s_own = jnp.concatenate([_bdot_g(q32[hr], kn[:, sl], NT) for hr, sl in zip(hrows, heads)], axis=0)
        rq = lax.broadcasted_iota(jnp.int32, s_own.shape, 0) % SUBLANES
        cj = lax.broadcasted_iota(jnp.int32, s_own.shape, 1)
        own_ok = (cj <= rq) & (cj < n_valid)
        s_own = jnp.where(own_ok, s_own, NEG)
        mx = jnp.broadcast_to(jnp.max(s_own, axis=1, keepdims=True), (rows, HEAD_DIM))
        for pg in range(n_pg):
            mx = jnp.maximum(mx, jnp.where(selw[pg // pages_per_blk], m_scr[pg], NEG))
        e_own = jnp.where(own_ok, jnp.exp(s_own - mx[:, :SUBLANES]), 0.0)
        l_tot = jnp.broadcast_to(jnp.sum(e_own, axis=1, keepdims=True), (rows, HEAD_DIM))
        o_tot = jnp.concatenate([_bdot(e_own[hr], vn[:, sl]) for hr, sl in zip(hrows, heads)], axis=0)
        for pg in range(n_pg):
            wgt = jnp.where(selw[pg // pages_per_blk], jnp.exp(jnp.minimum(m_scr[pg] - mx, 0.0)), 0.0)
            l_tot = l_tot + wgt * l_scr[pg]
            o_tot = o_tot + wgt * o_scr[pg]
        o = o_tot / l_tot
        y_ref[...] = jnp.concatenate([o[hr] for hr in hrows], axis=1) * _silu(z_ref[...])


def _moba_s_call(page_table, q_rot, k_rot, proj, cache_k, cache_v, layer, *, n_valid):
    n_b, n_pages = page_table.shape
    depth, n_pool, page = cache_k.shape[:3]
    assert MOBA_BLOCK % page == 0
    pages_per_blk = MOBA_BLOCK // page
    pps = PAGES_PER_STEP
    assert pps % pages_per_blk == 0 and n_pages % pps == 0 and n_pages // pages_per_blk >= MOBA_TOPK
    assert (page * N_HEADS) % SUBLANES == 0 and SUBLANES % N_HEADS == 0
    ck = cache_k.reshape(depth, n_pool, page * N_HEADS, HEAD_DIM)
    cv = cache_v.reshape(depth, n_pool, page * N_HEADS, HEAD_DIM)
    rows = N_HEADS * SUBLANES
    row_spec = lambda col: pl.BlockSpec((SUBLANES, BRANCH_W), lambda b, p, pt: (b, col))

    def page_spec(j):
        return pl.BlockSpec((None, None, page * N_HEADS, HEAD_DIM),
                            lambda b, p, pt: (layer, pt[b, p * pps + j], 0, 0))

    page_specs = [page_spec(j) for j in range(pps)]
    grid_spec = pltpu.PrefetchScalarGridSpec(
        num_scalar_prefetch=1,
        grid=(n_b, n_pages // pps),
        in_specs=[row_spec(0), row_spec(0), row_spec(COL_V_B), row_spec(COL_Z_B)] + page_specs + page_specs,
        out_specs=pl.BlockSpec((SUBLANES, BRANCH_W), lambda b, p, pt: (b, 0)),
        scratch_shapes=[pltpu.VMEM((rows, HEAD_DIM), F32),
                        pltpu.VMEM((n_pages // pages_per_blk, SUBLANES, HEAD_DIM), F32),
                        pltpu.VMEM((n_pages, rows, HEAD_DIM), F32),
                        pltpu.VMEM((n_pages, rows, HEAD_DIM), F32),
                        pltpu.VMEM((n_pages, rows, HEAD_DIM), F32)])
    return pl.pallas_call(
        functools.partial(_moba_s_kernel, n_valid=n_valid, pages_per_blk=pages_per_blk, pps=pps),
        grid_spec=grid_spec,
        out_shape=jax.ShapeDtypeStruct((n_b * SUBLANES, BRANCH_W), F32),
        compiler_params=_params("arbitrary", "arbitrary"),
        name="moba_s",
    )(page_table, q_rot, k_rot, proj, proj, *([ck] * pps), *([cv] * pps))


def _merge_kernel(ya_ref, yb_ref, yc_ref, mg_ref, x_ref, gate_ref, wb_ref, wo_ref, nf_ref, *out_refs, final):
    d = x_ref.shape[1]
    mixed = None
    for n, y_ref in enumerate((ya_ref, yb_ref, yc_ref)):
        per_branch = _bdot(y_ref[...], wb_ref[n])
        term = jax.nn.sigmoid(mg_ref[:, n * d:(n + 1) * d]) * per_branch
        mixed = term if mixed is None else mixed + term
    x_out = x_ref[...] + gate_ref[...] * _bdot(mixed, wo_ref[...])
    out_refs[0][...] = x_out
    if final:
        out_refs[1][...] = (x_out * lax.rsqrt(jnp.mean(x_out * x_out, axis=-1, keepdims=True) + EPS)
                            * nf_ref[...])


def _merge_call(y_a, y_b, y_c, proj, x2d, gate, w_branch, w_out, norm_f, *, tm, rows_per_mod, final):
    m, d = x2d.shape
    if gate.ndim == 3:
        gate_spec = pl.BlockSpec((None, 1, d), lambda i: ((i * tm) // rows_per_mod, 0, 0))
    else:
        gate_spec = pl.BlockSpec((tm, d), lambda i: (i, 0))
    y_spec = pl.BlockSpec((tm, BRANCH_W), lambda i: (i, 0))
    x_spec = pl.BlockSpec((tm, d), lambda i: (i, 0))
    n_out = 2 if final else 1
    outs = pl.pallas_call(
        functools.partial(_merge_kernel, final=final),
        grid=(m // tm,),
        in_specs=[y_spec, y_spec, y_spec,
                  pl.BlockSpec((tm, N_BRANCH * d), lambda i: (i, (COL_MERGE * BRANCH_W) // (N_BRANCH * d))),
                  x_spec, gate_spec,
                  pl.BlockSpec((N_BRANCH, BRANCH_W, d), lambda i: (0, 0, 0)),
                  pl.BlockSpec((d, d), lambda i: (0, 0)),
                  pl.BlockSpec((1, d), lambda i: (0, 0))],
        out_specs=[x_spec] * n_out,
        out_shape=[jax.ShapeDtypeStruct((m, d), F32)] * n_out,
        compiler_params=_params("arbitrary"),
        name="merge",
    )(y_a, y_b, y_c, proj, x2d, gate, w_branch, w_out, norm_f.reshape(1, d))
    return outs


def _rope_tables(pos):
    t = pos.shape[0]
    posf = pos.astype(F32)

    def cos_sin(n_rot, theta):
        half = n_rot // 2
        inv = theta ** (-jnp.arange(half, dtype=F32) / half)
        ang = posf[:, None] * inv[None, :]
        return jnp.cos(ang), jnp.sin(ang)

    cb, sb = cos_sin(ROPE_DIMS, ROPE_THETA)
    hb = ROPE_DIMS // 2
    tab_b = (jnp.concatenate([cb, cb, jnp.ones((t, HEAD_DIM - ROPE_DIMS), F32)], axis=1),
             jnp.concatenate([jnp.zeros((t, hb), F32), sb, jnp.zeros((t, HEAD_DIM - ROPE_DIMS), F32)], axis=1),
             jnp.concatenate([-sb, jnp.zeros((t, HEAD_DIM - hb), F32)], axis=1))
    cc, sc = cos_sin(HEAD_DIM, RET_THETA)
    tab_c = (jnp.concatenate([cc, cc], axis=1), jnp.concatenate([-sc, sc], axis=1))
    return tab_b, tab_c


def _pad_rows(a2d, n_b, t, t_pad):
    w = a2d.shape[1]
    return jnp.pad(a2d.reshape(n_b, t, w), ((0, 0), (0, t_pad - t), (0, 0))).reshape(n_b * t_pad, w)


def _alpha_lanes(v):
    return jnp.pad(v.astype(F32), (N_HEADS, BA_W - 2 * N_HEADS)).reshape(1, BA_W)


def kernel(x_prompt, x_sample, cache_k, cache_v, state_gdn, state_conv, state_ret, page_table, c_prompt, c_sample,
           norm_in, w_ada, b_ada, w_in, conv_w, a_log, dt_bias, norm_a, norm_c, w_branch, w_out, norm_f):
    n_b, seq, d = x_prompt.shape
    n_db, dec_seq, _ = x_sample.shape
    depth = w_in.shape[0]
    n_pages = page_table.shape[1]
    past_len = n_pages * cache_k.shape[2]
    assert d == 2 * BRANCH_W and dec_seq <= SUBLANES and dec_seq >= CONV_W - 1
    assert seq % GDN_CHUNK == 0 and seq % RET_CHUNK == 0
    t_pad = SUBLANES

    ba0 = 4 * BRANCH_W
    w_main = jnp.concatenate([w_in[:, :, :ba0], w_in[:, :, ba0 + 2 * N_HEADS:]], axis=2).astype(BF16)
    w_ba = jnp.pad(w_in[:, :, ba0:ba0 + 2 * N_HEADS], ((0, 0), (0, 0), (0, BA_W - 2 * N_HEADS))).astype(BF16)
    w_ada_b = w_ada.astype(BF16)
    w_branch_b = w_branch.astype(BF16)
    w_out_b = w_out.astype(BF16)

    n_c = n_b + n_db
    c_rows = -(-n_c // SUBLANES) * SUBLANES
    c_all = jnp.pad(jnp.concatenate([c_prompt, c_sample], axis=0), ((0, c_rows - n_c), (0, 0)))
    mods = _mod_call(c_all, w_ada_b, b_ada)

    tab_b_p, tab_c_p = _rope_tables(jnp.arange(seq, dtype=jnp.int32))
    tab_b_s, tab_c_s = _rope_tables(past_len + jnp.arange(t_pad, dtype=jnp.int32))

    zeros_state = jnp.zeros((n_b, N_HEADS, HEAD_DIM, HEAD_DIM), F32)
    zeros_hist = jnp.zeros((n_b, SUBLANES, 3 * BRANCH_W), F32)

    xp = x_prompt.reshape(n_b * seq, d)
    xs = x_sample.reshape(n_db * dec_seq, d)
    outs = {k: [] for k in ("kp", "vp", "ks", "vs", "gp", "gs", "cp", "cs", "rp", "rs")}
    y_p = y_s = None
    for l in range(depth):
        final = l == depth - 1
        alog, dtb = _alpha_lanes(a_log[l]), _alpha_lanes(dt_bias[l])
        na, nc = norm_a[l].reshape(1, HEAD_DIM), norm_c[l].reshape(1, HEAD_DIM)
        shift, scale, gate = jnp.split(mods[l], 3, axis=-1)

        mod_p = [t[:n_b].reshape(n_b, 1, d) for t in (scale, shift, gate)]
        proj, ba = _inproj_call(xp, mod_p[0], mod_p[1], norm_in[l], w_main[l], w_ba[l],
                                tm=min(seq, 1024), rows_per_mod=seq)
        q_rot, k_rot = _rope_b_call(proj, tab_b_p, tr=min(seq, 512), t_rows=seq)
        y_a, gdn_new = _gdn_call(proj, ba, zeros_hist, zeros_state, conv_w[l], alog, dtb, na,
                                 n_b=n_b, t_rows=seq, C=GDN_CHUNK, n_valid=GDN_CHUNK, n_sub=GDN_CHUNKS_PER_STEP)
        y_c, ret_new = _ret_call(proj, tab_c_p[0], tab_c_p[1], zeros_state, nc,
                                 n_b=n_b, t_rows=seq, C=RET_CHUNK, n_valid=RET_CHUNK, n_sub=RET_CHUNKS_PER_STEP)
        y_b = _moba_p_call(q_rot, k_rot, proj, n_b=n_b, t_rows=seq)
        res = _merge_call(y_a, y_b, y_c, proj, xp, mod_p[2], w_branch_b[l], w_out_b[l], norm_f,
                          tm=min(seq, 256), rows_per_mod=seq, final=final)
        xp = res[0]
        if final:
            y_p = res[1]
        proj3 = proj.reshape(n_b, seq, MAIN_W)
        outs["kp"].append(k_rot.reshape(n_b, seq, N_HEADS, HEAD_DIM))
        outs["vp"].append(proj3[:, :, COL_V_B * BRANCH_W:(COL_V_B + 1) * BRANCH_W].reshape(n_b, seq, N_HEADS, HEAD_DIM))
        outs["gp"].append(gdn_new)
        outs["cp"].append(proj3[:, seq - (CONV_W - 1):, :3 * BRANCH_W])
        outs["rp"].append(ret_new)

        mod_s = [jnp.repeat(t[n_b:n_c], dec_seq, axis=0) for t in (scale, shift, gate)]
        proj_s, ba_s = _inproj_call(xs, mod_s[0], mod_s[1], norm_in[l], w_main[l], w_ba[l],
                                    tm=n_db * dec_seq, rows_per_mod=dec_seq)
        proj_sp = _pad_rows(proj_s, n_db, dec_seq, t_pad)
        ba_sp = _pad_rows(ba_s, n_db, dec_seq, t_pad)
        hist = jnp.pad(state_conv[l], ((0, 0), (SUBLANES - (CONV_W - 1), 0), (0, 0)))
        q_rot_s, k_rot_s = _rope_b_call(proj_sp, tab_b_s, tr=t_pad, t_rows=t_pad)
        y_a_s, gdn_new_s = _gdn_call(proj_sp, ba_sp, hist, state_gdn[l], conv_w[l], alog, dtb, na,
                                     n_b=n_db, t_rows=t_pad, C=t_pad, n_valid=dec_seq, n_sub=1)
        y_c_s, ret_new_s = _ret_call(proj_sp, tab_c_s[0], tab_c_s[1], state_ret[l], nc,
                                     n_b=n_db, t_rows=t_pad, C=t_pad, n_valid=dec_seq, n_sub=1)
        y_b_s = _moba_s_call(page_table, q_rot_s, k_rot_s, proj_sp, cache_k, cache_v, l, n_valid=dec_seq)
        unpad = lambda y: y.reshape(n_db, t_pad, BRANCH_W)[:, :dec_seq].reshape(n_db * dec_seq, BRANCH_W)
        res_s = _merge_call(unpad(y_a_s), unpad(y_b_s), unpad(y_c_s), proj_s, xs, mod_s[2],
                            w_branch_b[l], w_out_b[l], norm_f, tm=n_db * dec_seq, rows_per_mod=dec_seq, final=final)
        xs = res_s[0]
        if final:
            y_s = res_s[1]
        proj_s3 = proj_s.reshape(n_db, dec_seq, MAIN_W)
        outs["ks"].append(k_rot_s.reshape(n_db, t_pad, N_HEADS, HEAD_DIM)[:, :dec_seq])
        outs["vs"].append(proj_s3[:, :, COL_V_B * BRANCH_W:(COL_V_B + 1) * BRANCH_W]
                          .reshape(n_db, dec_seq, N_HEADS, HEAD_DIM))
        outs["gs"].append(gdn_new_s)
        outs["cs"].append(proj_s3[:, dec_seq - (CONV_W - 1):, :3 * BRANCH_W])
        outs["rs"].append(ret_new_s)

    st = {k: jnp.stack(v) for k, v in outs.items()}
    return (y_p.reshape(n_b, seq, d), y_s.reshape(n_db, dec_seq, d),
            st["kp"], st["vp"], st["ks"], st["vs"], st["gp"], st["gs"],
            st["cp"], st["cs"], st["rp"], st["rs"])
```

```python
import functools
import math

import jax
import jax.numpy as jnp
from jax import lax
from jax.experimental import pallas as pl
from jax.experimental.pallas import tpu as pltpu

F32 = jnp.float32
BF16 = jnp.bfloat16
HI = lax.Precision.HIGHEST

HEAD_DIM = 128
N_HEADS = 4
BRANCH_W = N_HEADS * HEAD_DIM
N_BRANCH = 3
CONV_W = 4
GDN_CHUNK = 64
RET_CHUNK = 64
MOBA_BLOCK = 256
MOBA_TOPK = 3
ROPE_THETA = 500000.0
ROPE_DIMS = HEAD_DIM // 4
RET_THETA = 10000.0
EPS = 1e-6
NEG = -1e30
SUBLANES = 8
QK_SCALE = HEAD_DIM ** -0.5

COL_QKV_A, COL_Z_A, COL_Q_B, COL_K_B, COL_V_B, COL_Z_B = 0, 3, 4, 5, 6, 7
COL_Q_C, COL_K_C, COL_V_C, COL_Z_C, COL_MERGE = 8, 9, 10, 11, 12
MAIN_W = 18 * BRANCH_W
BA_W = 128

NT = (((1,), (1,)), ((), ()))
TN = (((0,), (0,)), ((), ()))

VMEM_LIMIT = 48 * 1024 * 1024


def _params(*sem):
    return pltpu.CompilerParams(dimension_semantics=sem, vmem_limit_bytes=VMEM_LIMIT)


def _silu(x):
    return x * jax.nn.sigmoid(x)


def _bdot(a, b):
    return jnp.dot(a.astype(BF16), b.astype(BF16), preferred_element_type=F32)


def _bdot_g(a, b, dims):
    return lax.dot_general(a.astype(BF16), b.astype(BF16), dims, preferred_element_type=F32)


def _hdot(a, b):
    return jnp.dot(a, b, precision=HI, preferred_element_type=F32)


def _mod_kernel(c_ref, w_ref, b_ref, o_ref):
    o_ref[...] = _bdot(_silu(c_ref[...]), w_ref[...]) + b_ref[...]


def _mod_call(c_all, w_ada, b_ada):
    depth, d, d3 = w_ada.shape
    rows = c_all.shape[0]
    tn = d
    return pl.pallas_call(
        _mod_kernel,
        grid=(depth, d3 // tn),
        in_specs=[pl.BlockSpec((rows, d), lambda l, j: (0, 0)),
                  pl.BlockSpec((None, d, tn), lambda l, j: (l, 0, j)),
                  pl.BlockSpec((None, 1, tn), lambda l, j: (l, 0, j))],
        out_specs=pl.BlockSpec((None, rows, tn), lambda l, j: (l, 0, j)),
        out_shape=jax.ShapeDtypeStruct((depth, rows, d3), F32),
        compiler_params=_params("arbitrary", "arbitrary"),
        name="mod",
    )(c_all, w_ada, b_ada.reshape(depth, 1, d3))


def _inproj_kernel(x_ref, sc_ref, sh_ref, g_ref, w_ref, wba_ref, o_ref, ba_ref, h_scr):
    @pl.when(pl.program_id(1) == 0)
    def _():
        x = x_ref[...]
        y = x * lax.rsqrt(jnp.mean(x * x, axis=-1, keepdims=True) + EPS) * g_ref[...]
        h = (y * (1.0 + sc_ref[...]) + sh_ref[...]).astype(BF16)
        h_scr[...] = h
        ba_ref[...] = jnp.dot(h, wba_ref[...], preferred_element_type=F32)

    o_ref[...] = jnp.dot(h_scr[...], w_ref[...], preferred_element_type=F32)


def _inproj_call(x2d, scale, shift, norm_g, w_main, w_ba, *, tm, rows_per_mod):
    m, d = x2d.shape
    tn = 1024
    assert m % tm == 0 and MAIN_W % tn == 0
    if scale.ndim == 3:
        assert rows_per_mod % tm == 0
        mod_spec = pl.BlockSpec((None, 1, d), lambda i, j: ((i * tm) // rows_per_mod, 0, 0))
    else:
        mod_spec = pl.BlockSpec((tm, d), lambda i, j: (i, 0))
    return pl.pallas_call(
        _inproj_kernel,
        grid=(m // tm, MAIN_W // tn),
        in_specs=[pl.BlockSpec((tm, d), lambda i, j: (i, 0)),
                  mod_spec, mod_spec,
                  pl.BlockSpec((1, d), lambda i, j: (0, 0)),
                  pl.BlockSpec((d, tn), lambda i, j: (0, j)),
                  pl.BlockSpec((d, BA_W), lambda i, j: (0, 0))],
        out_specs=[pl.BlockSpec((tm, tn), lambda i, j: (i, j)),
                   pl.BlockSpec((tm, BA_W), lambda i, j: (i, 0))],
        out_shape=[jax.ShapeDtypeStruct((m, MAIN_W), F32),
                   jax.ShapeDtypeStruct((m, BA_W), F32)],
        scratch_shapes=[pltpu.VMEM((tm, d), BF16)],
        compiler_params=_params("arbitrary", "arbitrary"),
        name="in_proj",
    )(x2d, scale, shift, norm_g.reshape(1, d), w_main, w_ba)


def _rope_b_kernel(q_ref, k_ref, v_ref, c_ref, s1_ref, s2_ref, qo_ref, ko_ref, *cache_refs):
    c, s1, s2 = c_ref[...], s1_ref[...], s2_ref[...]
    half = ROPE_DIMS // 2
    tr = q_ref.shape[0]
    for h in range(N_HEADS):
        sl = slice(h * HEAD_DIM, (h + 1) * HEAD_DIM)
        q = q_ref[:, sl]
        k = k_ref[:, sl]
        qr = q * c + pltpu.roll(q, half, 1) * s1 + pltpu.roll(q, HEAD_DIM - half, 1) * s2
        kr = k * c + pltpu.roll(k, half, 1) * s1 + pltpu.roll(k, HEAD_DIM - half, 1) * s2
        qo_ref[:, sl] = qr * QK_SCALE
        ko_ref[:, sl] = kr
        if cache_refs:
            kc_ref, vc_ref = cache_refs
            kc_ref[pl.ds(h, tr, stride=N_HEADS), :] = kr
            vc_ref[pl.ds(h, tr, stride=N_HEADS), :] = v_ref[:, sl]


def _rope_b_call(proj, tabs, *, tr, t_rows, cache_layout):
    m = proj.shape[0]
    nt = t_rows // tr
    assert t_rows % tr == 0 and m % t_rows == 0
    tab_spec = pl.BlockSpec((tr, HEAD_DIM), lambda i: (i % nt, 0))
    col_spec = lambda col: pl.BlockSpec((tr, BRANCH_W), lambda i: (i, col))
    out_specs = [col_spec(0), col_spec(0)]
    out_shape = [jax.ShapeDtypeStruct((m, BRANCH_W), F32)] * 2
    if cache_layout:
        out_specs += [pl.BlockSpec((tr * N_HEADS, HEAD_DIM), lambda i: (i, 0))] * 2
        out_shape += [jax.ShapeDtypeStruct((m * N_HEADS, HEAD_DIM), F32)] * 2
    return pl.pallas_call(
        _rope_b_kernel,
        grid=(m // tr,),
        in_specs=[col_spec(COL_Q_B), col_spec(COL_K_B), col_spec(COL_V_B), tab_spec, tab_spec, tab_spec],
        out_specs=out_specs,
        out_shape=out_shape,
        compiler_params=_params("arbitrary"),
        name="rope_b",
    )(proj, proj, proj, *tabs)


INV_BASE = 8
GDN_CHUNKS_PER_STEP = 4


def _bdot_each(xs, ys):
    return [_bdot(x, y) for x, y in zip(xs, ys)]


def _unit_lower_inverses(mats, ri, ci, n):
    base = min(INV_BASE, n)
    eye = (ri == ci).astype(F32)
    dpows = [jnp.where((ri // base) == (ci // base), a, 0.0) for a in mats]
    invs = [eye + d for d in dpows]
    for _ in range(int(math.log2(base)) - 1):
        dpows = _bdot_each(dpows, dpows)
        invs = [inv + t for inv, t in zip(invs, _bdot_each(invs, dpows))]
    size = base
    while size < n:
        off = ((ri // (2 * size)) == (ci // (2 * size))) & ((ri // size) != (ci // size))
        es = [jnp.where(off, a, 0.0) for a in mats]
        invs = [inv + t for inv, t in zip(invs, _bdot_each(invs, _bdot_each(es, invs)))]
        size *= 2
    return invs


def _gdn_kernel(qkv_ref, hist_ref, ba_ref, z_ref, s0_ref, cw_ref, alog_ref, dtb_ref, na_ref,
                y_ref, sout_ref, s_scr, tail_scr, *, C, n_valid, n_sub):
    c_idx = pl.program_id(1)
    rows = n_sub * C

    @pl.when(c_idx == 0)
    def _():
        s_scr[...] = s0_ref[...]
        tail_scr[...] = hist_ref[...]

    u = qkv_ref[...]
    prev = tail_scr[...]
    w = cw_ref[...]
    acc = u * w[CONV_W - 1:CONV_W, :]
    row8 = lax.broadcasted_iota(jnp.int32, (SUBLANES, u.shape[1]), 0)
    for s in range(1, CONV_W):
        rolled = pltpu.roll(u, s, 0)
        first = jnp.where(row8 < s, pltpu.roll(prev, s, 0), rolled[:SUBLANES])
        shifted = first if rows == SUBLANES else jnp.concatenate([first, rolled[SUBLANES:]], axis=0)
        acc = acc + shifted * w[CONV_W - 1 - s:CONV_W - s, :]
    tail_scr[...] = u[rows - SUBLANES:, :]
    conv = _silu(acc)

    ba = ba_ref[...]
    beta_t = jax.nn.sigmoid(ba)
    xg = ba + dtb_ref[...]
    g_t = -jnp.exp(alog_ref[...]) * (jnp.maximum(xg, 0.0) + jnp.log1p(jnp.exp(-jnp.abs(xg))))
    row_t = lax.broadcasted_iota(jnp.int32, ba.shape, 0) % C
    if n_valid < C:
        beta_t = jnp.where(row_t < n_valid, beta_t, 0.0)
        g_t = jnp.where(row_t < n_valid, g_t, 0.0)
    gc_t = g_t
    shift = 1
    while shift < C:
        gc_t = gc_t + jnp.where(row_t >= shift, pltpu.roll(gc_t, shift, 0), 0.0)
        shift *= 2

    cs = N_HEADS * C
    subs = range(n_sub)
    stack = lambda f: jnp.concatenate([f(h) for h in range(N_HEADS)], axis=0)
    ri = lax.broadcasted_iota(jnp.int32, (cs, cs), 0)
    ci = lax.broadcasted_iota(jnp.int32, (cs, cs), 1)
    same_head = (ri // C) == (ci // C)
    tri = same_head & (ri >= ci)
    strict = same_head & (ri > ci)

    qn, kn, kb, vb, gc, g_last, decay = [], [], [], [], [], [], []
    for j in subs:
        r0 = j * C
        lanes = lambda t, col: jnp.broadcast_to(t[r0:r0 + C, col:col + 1], (C, HEAD_DIM))
        head_cols = lambda base: stack(lambda h: conv[r0:r0 + C, base + h * HEAD_DIM:base + (h + 1) * HEAD_DIM])
        q, k, v = head_cols(0), head_cols(BRANCH_W), head_cols(2 * BRANCH_W)
        beta = stack(lambda h: lanes(beta_t, h))
        gc_j = stack(lambda h: lanes(gc_t, N_HEADS + h))
        if cs % HEAD_DIM == 0:
            gc_row = jnp.concatenate([gc_j.T] * (cs // HEAD_DIM), axis=0)
            gc_col = jnp.concatenate([gc_j] * (cs // HEAD_DIM), axis=1)
        else:
            g_st = stack(lambda h: lanes(g_t, N_HEADS + h))
            upper = (same_head & (ri <= ci)).astype(F32)
            gc_row = _hdot(jnp.ones((cs, cs), F32), g_st[:, :cs] * upper)
            gc_col = gc_j[:, :cs]
        kn_j = k * lax.rsqrt(jnp.sum(k * k, axis=-1, keepdims=True) + EPS)
        qn.append(q * lax.rsqrt(jnp.sum(q * q, axis=-1, keepdims=True) + EPS) * QK_SCALE)
        kn.append(kn_j)
        kb.append(kn_j * beta)
        vb.append(v * beta)
        gc.append(gc_j)
        g_last.append(stack(lambda h: jnp.broadcast_to(
            gc_t[r0 + C - 1:r0 + C, N_HEADS + h:N_HEADS + h + 1], (C, HEAD_DIM))))
        decay.append(jnp.where(tri, jnp.exp(jnp.where(tri, gc_col - gc_row, 0.0)), 0.0))

    kk = [_bdot_g(kb[j], kn[j], NT) for j in subs]
    qk = [_bdot_g(qn[j], kn[j], NT) for j in subs]
    invs = _unit_lower_inverses([-jnp.where(strict, kk[j] * decay[j], 0.0) for j in subs], ri, ci, C)
    eg = [jnp.exp(gc[j]) for j in subs]
    sols = _bdot_each(invs, [jnp.concatenate([vb[j], kb[j] * eg[j]], axis=1) for j in subs])
    attn = [qk[j] * decay[j] for j in subs]
    q_dec = [qn[j] * eg[j] for j in subs]
    k_dec = [kn[j] * jnp.exp(g_last[j] - gc[j]) for j in subs]

    hrows = [slice(h * C, (h + 1) * C) for h in range(N_HEADS)]
    states = [s_scr[h] for h in range(N_HEADS)]
    for j in subs:
        u_s, w_s = sols[j][:, :HEAD_DIM], sols[j][:, HEAD_DIM:]
        v_new = jnp.concatenate([u_s[hr] - _bdot(w_s[hr], st) for hr, st in zip(hrows, states)], axis=0)
        o = (jnp.concatenate([_bdot(q_dec[j][hr], st) for hr, st in zip(hrows, states)], axis=0)
             + _bdot(attn[j], v_new))
        states = [st * jnp.exp(g_last[j][h * C:h * C + 1, :]) + _bdot_g(k_dec[j][hr], v_new[hr], TN)
                  for h, (hr, st) in enumerate(zip(hrows, states))]
        on = o * lax.rsqrt(jnp.mean(o * o, axis=-1, keepdims=True) + EPS) * na_ref[...]
        for h, hr in enumerate(hrows):
            sl = slice(h * HEAD_DIM, (h + 1) * HEAD_DIM)
            y_ref[j * C:(j + 1) * C, sl] = on[hr] * _silu(z_ref[j * C:(j + 1) * C, sl])
    for h in range(N_HEADS):
        s_scr[h] = states[h]

    @pl.when(c_idx == pl.num_programs(1) - 1)
    def _():
        sout_ref[...] = s_scr[...]


def _gdn_call(proj, ba, hist, s0, conv_w, alog, dtb, norm_a, *, n_b, t_rows, C, n_valid, n_sub):
    tile = n_sub * C
    n_c = t_rows // tile
    assert t_rows % tile == 0
    row = lambda b, c: b * n_c + c
    vec_spec = pl.BlockSpec((1, HEAD_DIM), lambda b, c: (0, 0))
    state_spec = pl.BlockSpec((None, N_HEADS, HEAD_DIM, HEAD_DIM), lambda b, c: (b, 0, 0, 0))
    return pl.pallas_call(
        functools.partial(_gdn_kernel, C=C, n_valid=n_valid, n_sub=n_sub),
        grid=(n_b, n_c),
        in_specs=[pl.BlockSpec((tile, 3 * BRANCH_W), lambda b, c: (row(b, c), 0)),
                  pl.BlockSpec((None, SUBLANES, 3 * BRANCH_W), lambda b, c: (b, 0, 0)),
                  pl.BlockSpec((tile, BA_W), lambda b, c: (row(b, c), 0)),
                  pl.BlockSpec((tile, BRANCH_W), lambda b, c: (row(b, c), COL_Z_A)),
                  state_spec,
                  pl.BlockSpec((CONV_W, 3 * BRANCH_W), lambda b, c: (0, 0)),
                  vec_spec, vec_spec, vec_spec],
        out_specs=[pl.BlockSpec((tile, BRANCH_W), lambda b, c: (row(b, c), 0)), state_spec],
        out_shape=[jax.ShapeDtypeStruct((n_b * t_rows, BRANCH_W), F32),
                   jax.ShapeDtypeStruct((n_b, N_HEADS, HEAD_DIM, HEAD_DIM), F32)],
        scratch_shapes=[pltpu.VMEM((N_HEADS, HEAD_DIM, HEAD_DIM), F32),
                        pltpu.VMEM((SUBLANES, 3 * BRANCH_W), F32)],
        compiler_params=_params("arbitrary", "arbitrary"),
        name="gdn",
    )(proj, hist, ba, proj, s0, conv_w, alog, dtb, norm_a)


RET_CHUNKS_PER_STEP = 4

def _ret_kernel(q_ref, k_ref, v_ref, z_ref, cos_ref, sin_ref, s0_ref, dmat_ref, qdec_ref, kdec_ref,
                cdec_ref, nc_ref, y_ref, sout_ref, s_scr, *, C, n_sub):
    c_idx = pl.program_id(1)

    @pl.when(c_idx == 0)
    def _():
        s_scr[...] = s0_ref[...]

    cos, sin = cos_ref[...], sin_ref[...]
    heads = [slice(h * HEAD_DIM, (h + 1) * HEAD_DIM) for h in range(N_HEADS)]
    units = [(j, h) for j in range(n_sub) for h in range(N_HEADS)]
    rows = lambda j: slice(j * C, (j + 1) * C)
    rope = lambda x: x * cos + pltpu.roll(x, HEAD_DIM // 2, 1) * sin
    qr = [rope(q_ref[:, sl]) for sl in heads]
    kr = [rope(k_ref[:, sl]) * QK_SCALE for sl in heads]
    vs = [v_ref[:, sl] for sl in heads]
    sc = [_bdot_g(qr[h][rows(j)], kr[h][rows(j)], NT) * dmat_ref[h] for j, h in units]
    o_intra = [_bdot(s, vs[h][rows(j)]) for s, (j, h) in zip(sc, units)]
    kv = [_bdot_g(kr[h][rows(j)] * kdec_ref[h], vs[h][rows(j)], TN) for j, h in units]

    states = [s_scr[h] for h in range(N_HEADS)]
    for j in range(n_sub):
        for h, sl in enumerate(heads):
            u = j * N_HEADS + h
            o = o_intra[u] + _bdot(qr[h][rows(j)] * qdec_ref[h], states[h])
            states[h] = states[h] * cdec_ref[h] + kv[u]
            on = o * lax.rsqrt(jnp.mean(o * o, axis=-1, keepdims=True) + EPS) * nc_ref[...]
            y_ref[rows(j), sl] = on * _silu(z_ref[rows(j), sl])
    for h in range(N_HEADS):
        s_scr[h] = states[h]

    @pl.when(c_idx == pl.num_programs(1) - 1)
    def _():
        sout_ref[...] = s_scr[...]


def _ret_tables(C, n_valid):
    log_g = jnp.log1p(-(2.0 ** (-5.0 - jnp.arange(N_HEADS, dtype=F32))))
    idx = jnp.arange(C, dtype=F32)
    tri = jnp.tril(jnp.ones((C, C), dtype=bool))
    dmat = jnp.where(tri, jnp.exp(log_g[:, None, None] * jnp.where(tri, idx[:, None] - idx[None, :], 0.0)), 0.0)
    q_dec = jnp.exp(log_g[:, None] * (idx + 1.0))[..., None]
    k_dec = jnp.exp(log_g[:, None] * (n_valid - 1.0 - idx))[..., None]
    c_dec = jnp.exp(log_g * n_valid)[:, None, None]
    bc = lambda t, r: jnp.broadcast_to(t, (N_HEADS, r, HEAD_DIM))
    return dmat, bc(q_dec, C), bc(k_dec, C), bc(c_dec, 1)


def _ret_call(proj, cos2, sin2, s0, norm_c, *, n_b, t_rows, C, n_valid, n_sub):
    tile = n_sub * C
    n_c = t_rows // tile
    assert t_rows % tile == 0
    row = lambda b, c: b * n_c + c
    dmat, q_dec, k_dec, c_dec = _ret_tables(C, n_valid)
    col_spec = lambda col: pl.BlockSpec((tile, BRANCH_W), lambda b, c: (row(b, c), col))
    tab_spec = pl.BlockSpec((tile, HEAD_DIM), lambda b, c: (c, 0))
    state_spec = pl.BlockSpec((None, N_HEADS, HEAD_DIM, HEAD_DIM), lambda b, c: (b, 0, 0, 0))
    const_spec = lambda r, w: pl.BlockSpec((N_HEADS, r, w), lambda b, c: (0, 0, 0))
    return pl.pallas_call(
        functools.partial(_ret_kernel, C=C, n_sub=n_sub),
        grid=(n_b, n_c),
        in_specs=[col_spec(COL_Q_C), col_spec(COL_K_C), col_spec(COL_V_C), col_spec(COL_Z_C),
                  tab_spec, tab_spec, state_spec,
                  const_spec(C, C), const_spec(C, HEAD_DIM), const_spec(C, HEAD_DIM), const_spec(1, HEAD_DIM),
                  pl.BlockSpec((1, HEAD_DIM), lambda b, c: (0, 0))],
        out_specs=[pl.BlockSpec((tile, BRANCH_W), lambda b, c: (row(b, c), 0)), state_spec],
        out_shape=[jax.ShapeDtypeStruct((n_b * t_rows, BRANCH_W), F32),
                   jax.ShapeDtypeStruct((n_b, N_HEADS, HEAD_DIM, HEAD_DIM), F32)],
        scratch_shapes=[pltpu.VMEM((N_HEADS, HEAD_DIM, HEAD_DIM), F32)],
        compiler_params=_params("arbitrary", "arbitrary"),
        name="ret",
    )(proj, proj, proj, proj, cos2, sin2, s0, dmat, q_dec, k_dec, c_dec, norm_c)


def _topk_mask(gate, valid, axis):
    idx = lax.broadcasted_iota(jnp.int32, gate.shape, axis)
    gm = jnp.where(valid, gate, -jnp.inf)
    rank = jnp.zeros(gate.shape, jnp.int32)
    for m in range(gate.shape[axis]):
        gmm = gm[m:m + 1, :] if axis == 0 else gm[:, m:m + 1]
        beats = (gmm > gm) | ((gmm == gm) & (idx > m))
        rank = rank + jnp.where(beats, 1, 0)
    return valid & (rank < MOBA_TOPK)


def _moba_p_kernel(q_ref, k_ref, v_ref, z_ref, y_ref, kmean_scr, kb_scr, vt_scr, bias_scr, acc_scr, *, n_blk):
    qi = pl.program_id(1)
    blk = MOBA_BLOCK
    heads = [slice(h * HEAD_DIM, (h + 1) * HEAD_DIM) for h in range(N_HEADS)]

    @pl.when(qi == 0)
    def _():
        def prep(n, carry):
            rows = pl.ds(pl.multiple_of(n * blk, blk), blk)
            for h, sl in enumerate(heads):
                kn = k_ref[rows, sl]
                kmean_scr[h, pl.ds(n, 1), :] = jnp.sum(kn, axis=0, keepdims=True) * (1.0 / blk)
                kb_scr[h * n_blk + n] = kn.astype(BF16)
                vt_scr[h * n_blk + n] = v_ref[rows, sl].T.astype(BF16)
            return carry

        lax.fori_loop(0, n_blk, prep, 0)

    key_i = lax.broadcasted_iota(jnp.int32, (blk, blk), 0)
    qry_i = lax.broadcasted_iota(jnp.int32, (blk, blk), 1)
    blk_id = lax.broadcasted_iota(jnp.int32, (n_blk, blk), 0)
    qs = [q_ref[:, sl] for sl in heads]
    qbs = [q.astype(BF16) for q in qs]
    ss = [jnp.where(key_i <= qry_i,
                    lax.dot_general(kb_scr[h * n_blk + qi], qbs[h], NT, preferred_element_type=F32), NEG)
          for h in range(N_HEADS)]
    ms = [jnp.max(s, axis=0, keepdims=True) for s in ss]
    ps = [jnp.exp(ss[h] - ms[h]) for h in range(N_HEADS)]
    ls = [jnp.sum(p, axis=0, keepdims=True) for p in ps]
    for h in range(N_HEADS):
        acc_scr[h] = jnp.dot(vt_scr[h * n_blk + qi], ps[h].astype(BF16), preferred_element_type=F32)
    for h in range(N_HEADS):
        gate = lax.dot_general(kmean_scr[h], qs[h], NT, precision=HI, preferred_element_type=F32)
        bias_scr[h] = jnp.where(_topk_mask(gate, blk_id < qi, 0), 0.0, NEG)

    hs = range(N_HEADS)

    def body(i, carry):
        ms, ls = carry
        ns = (2 * i, 2 * i + 1)
        ss = [[lax.dot_general(kb_scr[h * n_blk + n], qbs[h], NT, preferred_element_type=F32)
               + bias_scr[h, pl.ds(n, 1), :] for n in ns] for h in hs]
        ms_new = [jnp.maximum(ms[h], jnp.max(jnp.maximum(ss[h][0], ss[h][1]), axis=0, keepdims=True)) for h in hs]
        ps = [[jnp.exp(s - ms_new[h]) for s in ss[h]] for h in hs]
        alphas = [jnp.exp(ms[h] - ms_new[h]) for h in hs]
        pvs = [[jnp.dot(vt_scr[h * n_blk + n], p.astype(BF16), preferred_element_type=F32)
                for n, p in zip(ns, ps[h])] for h in hs]
        ls_new = [alphas[h] * ls[h] + jnp.sum(ps[h][0] + ps[h][1], axis=0, keepdims=True) for h in hs]
        for h in hs:
            acc_scr[h] = alphas[h] * acc_scr[h] + (pvs[h][0] + pvs[h][1])
        return tuple(ms_new), tuple(ls_new)

    _, ls = lax.fori_loop(0, (qi + 1) // 2, body, (tuple(ms), tuple(ls)))
    for h, sl in enumerate(heads):
        y_ref[:, sl] = (acc_scr[h] / ls[h]).T * _silu(z_ref[:, sl])


def _moba_p_call(q_rot, k_rot, proj, *, n_b, t_rows):
    n_blk = t_rows // MOBA_BLOCK
    assert t_rows % MOBA_BLOCK == 0 and n_blk >= MOBA_TOPK
    blk = MOBA_BLOCK
    seq_spec = lambda col: pl.BlockSpec((t_rows, BRANCH_W), lambda b, i: (b, col), pipeline_mode=pl.Buffered(1))
    tile_spec = lambda col: pl.BlockSpec((blk, BRANCH_W), lambda b, i: (b * n_blk + i, col))
    return pl.pallas_call(
        functools.partial(_moba_p_kernel, n_blk=n_blk),
        grid=(n_b, n_blk),
        in_specs=[tile_spec(0), seq_spec(0), seq_spec(COL_V_B), tile_spec(COL_Z_B)],
        out_specs=tile_spec(0),
        out_shape=jax.ShapeDtypeStruct((n_b * t_rows, BRANCH_W), F32),
        scratch_shapes=[pltpu.VMEM((N_HEADS, n_blk, HEAD_DIM), F32),
                        pltpu.VMEM((N_HEADS * n_blk, blk, HEAD_DIM), BF16),
                        pltpu.VMEM((N_HEADS * n_blk, HEAD_DIM, blk), BF16),
                        pltpu.VMEM((N_HEADS, n_blk, blk), F32),
                        pltpu.VMEM((N_HEADS, HEAD_DIM, blk), F32)],
        compiler_params=_params("arbitrary", "arbitrary"),
        name="moba_p",
    )(q_rot, k_rot, proj, proj)


PAGES_PER_STEP = 16


def _moba_s_kernel(pt_ref, q_ref, kn_ref, vn_ref, z_ref, *refs, n_valid, pages_per_blk, pps):
    del pt_ref
    k_refs, v_refs = refs[:pps], refs[pps:2 * pps]
    y_ref, q_scr, ksum_scr, m_scr, l_scr, o_scr = refs[2 * pps:]
    step = pl.program_id(1)
    rows = N_HEADS * SUBLANES
    page_rows = k_refs[0].shape[0]
    heads = [slice(h * HEAD_DIM, (h + 1) * HEAD_DIM) for h in range(N_HEADS)]
    hrows = [slice(h * SUBLANES, (h + 1) * SUBLANES) for h in range(N_HEADS)]

    @pl.when(step == 0)
    def _():
        q = q_ref[...]
        q_scr[...] = jnp.concatenate([q[:, sl] for sl in heads], axis=0)

    q32 = q_scr[...]
    qb = q32.astype(BF16)
    row_head = lax.broadcasted_iota(jnp.int32, (rows, page_rows), 0) // SUBLANES
    col_head = lax.broadcasted_iota(jnp.int32, (rows, page_rows), 1) % N_HEADS
    head_bias = jnp.where(row_head == col_head, 0.0, NEG)

    blks_per_step = pps // pages_per_blk
    kps = [k_refs[j][...] for j in range(pps)]
    ss = [_bdot_g(qb, kp, NT) + head_bias for kp in kps]
    ms = [jnp.max(s, axis=1, keepdims=True) for s in ss]
    es = [jnp.exp(s - m) for s, m in zip(ss, ms)]
    os_ = [_bdot(e, v_refs[j][...]) for j, e in enumerate(es)]
    for j in range(pps):
        pg = step * pps + j
        m_scr[pg] = jnp.broadcast_to(ms[j], (rows, HEAD_DIM))
        l_scr[pg] = jnp.broadcast_to(jnp.sum(es[j], axis=1, keepdims=True), (rows, HEAD_DIM))
        o_scr[pg] = os_[j]
    parts = [jnp.sum(kp.reshape(page_rows // SUBLANES, SUBLANES, HEAD_DIM), axis=0) for kp in kps]
    for bl in range(blks_per_step):
        ksum = parts[bl * pages_per_blk]
        for j in range(bl * pages_per_blk + 1, (bl + 1) * pages_per_blk):
            ksum = ksum + parts[j]
        ksum_scr[step * blks_per_step + bl] = ksum

    @pl.when(step == pl.num_programs(1) - 1)
    def _():
        n_pg = m_scr.shape[0]
        n_blk = ksum_scr.shape[0]
        kflat = ksum_scr[...].reshape(n_blk * SUBLANES, HEAD_DIM) * (1.0 / MOBA_BLOCK)
        g_all = lax.dot_general(q32, kflat, NT, precision=HI, preferred_element_type=F32)
        rh = lax.broadcasted_iota(jnp.int32, g_all.shape, 0) // SUBLANES
        ch = lax.broadcasted_iota(jnp.int32, g_all.shape, 1) % N_HEADS
        pool = (lax.broadcasted_iota(jnp.int32, (n_blk * SUBLANES, n_blk), 0) // SUBLANES
                == lax.broadcasted_iota(jnp.int32, (n_blk * SUBLANES, n_blk), 1)).astype(F32)
        gate = _hdot(jnp.where(rh == ch, g_all, 0.0), pool)
        sel_f = jnp.where(_topk_mask(gate, jnp.ones(gate.shape, jnp.bool_), 1), 1.0, 0.0)
        selw = [jnp.broadcast_to(sel_f[:, n:n + 1], (rows, HEAD_DIM)) > 0.5 for n in range(n_blk)]

        kn, vn = kn_ref[...], vn_ref[...]
        s_own = jnp.concatenate([_bdot_g(q32[hr], kn[:, sl], NT) for hr, sl in zip(hrows, heads)], axis=0)
        rq = lax.broadcasted_iota(jnp.int32, s_own.shape, 0) % SUBLANES
        cj = lax.broadcasted_iota(jnp.int32, s_own.shape, 1)
        own_ok = (cj <= rq) & (cj < n_valid)
        s_own = jnp.where(own_ok, s_own, NEG)
        mx = jnp.broadcast_to(jnp.max(s_own, axis=1, keepdims=True), (rows, HEAD_DIM))
        for pg in range(n_pg):
            mx = jnp.maximum(mx, jnp.where(selw[pg // pages_per_blk], m_scr[pg], NEG))
        e_own = jnp.where(own_ok, jnp.exp(s_own - mx[:, :SUBLANES]), 0.0)
        l_tot = jnp.broadcast_to(jnp.sum(e_own, axis=1, keepdims=True), (rows, HEAD_DIM))
        o_tot = jnp.concatenate([_bdot(e_own[hr], vn[:, sl]) for hr, sl in zip(hrows, heads)], axis=0)
        for pg in range(n_pg):
            wgt = jnp.where(selw[pg // pages_per_blk], jnp.exp(jnp.minimum(m_scr[pg] - mx, 0.0)), 0.0)
            l_tot = l_tot + wgt * l_scr[pg]
            o_tot = o_tot + wgt * o_scr[pg]
        o = o_tot / l_tot
        y_ref[...] = jnp.concatenate([o[hr] for hr in hrows], axis=1) * _silu(z_ref[...])


def _moba_s_call(page_table, q_rot, k_rot, proj, cache_k, cache_v, layer, *, n_valid):
    n_b, n_pages = page_table.shape
    depth, n_pool, page = cache_k.shape[:3]
    assert MOBA_BLOCK % page == 0
    pages_per_blk = MOBA_BLOCK // page
    pps = PAGES_PER_STEP
    assert pps % pages_per_blk == 0 and n_pages % pps == 0 and n_pages // pages_per_blk >= MOBA_TOPK
    assert (page * N_HEADS) % SUBLANES == 0 and SUBLANES % N_HEADS == 0
    ck = cache_k.reshape(depth, n_pool, page * N_HEADS, HEAD_DIM)
    cv = cache_v.reshape(depth, n_pool, page * N_HEADS, HEAD_DIM)
    rows = N_HEADS * SUBLANES
    row_spec = lambda col: pl.BlockSpec((SUBLANES, BRANCH_W), lambda b, p, pt: (b, col))

    def page_spec(j):
        return pl.BlockSpec((None, None, page * N_HEADS, HEAD_DIM),
                            lambda b, p, pt: (layer, pt[b, p * pps + j], 0, 0))

    page_specs = [page_spec(j) for j in range(pps)]
    grid_spec = pltpu.PrefetchScalarGridSpec(
        num_scalar_prefetch=1,
        grid=(n_b, n_pages // pps),
        in_specs=[row_spec(0), row_spec(0), row_spec(COL_V_B), row_spec(COL_Z_B)] + page_specs + page_specs,
        out_specs=pl.BlockSpec((SUBLANES, BRANCH_W), lambda b, p, pt: (b, 0)),
        scratch_shapes=[pltpu.VMEM((rows, HEAD_DIM), F32),
                        pltpu.VMEM((n_pages // pages_per_blk, SUBLANES, HEAD_DIM), F32),
                        pltpu.VMEM((n_pages, rows, HEAD_DIM), F32),
                        pltpu.VMEM((n_pages, rows, HEAD_DIM), F32),
                        pltpu.VMEM((n_pages, rows, HEAD_DIM), F32)])
    return pl.pallas_call(
        functools.partial(_moba_s_kernel, n_valid=n_valid, pages_per_blk=pages_per_blk, pps=pps),
        grid_spec=grid_spec,
        out_shape=jax.ShapeDtypeStruct((n_b * SUBLANES, BRANCH_W), F32),
        compiler_params=_params("arbitrary", "arbitrary"),
        name="moba_s",
    )(page_table, q_rot, k_rot, proj, proj, *([ck] * pps), *([cv] * pps))


def _merge_kernel(ya_ref, yb_ref, yc_ref, mg_ref, x_ref, gate_ref, wb_ref, wo_ref, nf_ref, *out_refs, final):
    d = x_ref.shape[1]
    mixed = None
    for n, y_ref in enumerate((ya_ref, yb_ref, yc_ref)):
        per_branch = _bdot(y_ref[...], wb_ref[n])
        term = jax.nn.sigmoid(mg_ref[:, n * d:(n + 1) * d]) * per_branch
        mixed = term if mixed is None else mixed + term
    x_out = x_ref[...] + gate_ref[...] * _bdot(mixed, wo_ref[...])
    out_refs[0][...] = x_out
    if final:
        out_refs[1][...] = (x_out * lax.rsqrt(jnp.mean(x_out * x_out, axis=-1, keepdims=True) + EPS)
                            * nf_ref[...])


def _merge_call(y_a, y_b, y_c, proj, x2d, gate, w_branch, w_out, norm_f, *, tm, rows_per_mod, final):
    m, d = x2d.shape
    assert m % tm == 0
    if gate.ndim == 3:
        assert rows_per_mod % tm == 0
        gate_spec = pl.BlockSpec((None, 1, d), lambda i: ((i * tm) // rows_per_mod, 0, 0))
    else:
        gate_spec = pl.BlockSpec((tm, d), lambda i: (i, 0))
    y_spec = pl.BlockSpec((tm, BRANCH_W), lambda i: (i, 0))
    x_spec = pl.BlockSpec((tm, d), lambda i: (i, 0))
    n_out = 2 if final else 1
    outs = pl.pallas_call(
        functools.partial(_merge_kernel, final=final),
        grid=(m // tm,),
        in_specs=[y_spec, y_spec, y_spec,
                  pl.BlockSpec((tm, N_BRANCH * d), lambda i: (i, (COL_MERGE * BRANCH_W) // (N_BRANCH * d))),
                  x_spec, gate_spec,
                  pl.BlockSpec((N_BRANCH, BRANCH_W, d), lambda i: (0, 0, 0)),
                  pl.BlockSpec((d, d), lambda i: (0, 0)),
                  pl.BlockSpec((1, d), lambda i: (0, 0))],
        out_specs=[x_spec] * n_out,
        out_shape=[jax.ShapeDtypeStruct((m, d), F32)] * n_out,
        compiler_params=_params("arbitrary"),
        name="merge",
    )(y_a, y_b, y_c, proj, x2d, gate, w_branch, w_out, norm_f.reshape(1, d))
    return outs


def _rope_tables(pos):
    t = pos.shape[0]
    posf = pos.astype(F32)

    def cos_sin(n_rot, theta):
        half = n_rot // 2
        inv = theta ** (-jnp.arange(half, dtype=F32) / half)
        ang = posf[:, None] * inv[None, :]
        return jnp.cos(ang), jnp.sin(ang)

    cb, sb = cos_sin(ROPE_DIMS, ROPE_THETA)
    hb = ROPE_DIMS // 2
    tab_b = (jnp.concatenate([cb, cb, jnp.ones((t, HEAD_DIM - ROPE_DIMS), F32)], axis=1),
             jnp.concatenate([jnp.zeros((t, hb), F32), sb, jnp.zeros((t, HEAD_DIM - ROPE_DIMS), F32)], axis=1),
             jnp.concatenate([-sb, jnp.zeros((t, HEAD_DIM - hb), F32)], axis=1))
    cc, sc = cos_sin(HEAD_DIM, RET_THETA)
    tab_c = (jnp.concatenate([cc, cc], axis=1), jnp.concatenate([-sc, sc], axis=1))
    return tab_b, tab_c


def _pad_rows(a2d, n_b, t, t_pad):
    w = a2d.shape[1]
    return jnp.pad(a2d.reshape(n_b, t, w), ((0, 0), (0, t_pad - t), (0, 0))).reshape(n_b * t_pad, w)


def _alpha_lanes(v):
    return jnp.pad(v.astype(F32), (N_HEADS, BA_W - 2 * N_HEADS)).reshape(1, BA_W)


def kernel(x_prompt, x_sample, cache_k, cache_v, state_gdn, state_conv, state_ret, page_table, c_prompt, c_sample,
           norm_in, w_ada, b_ada, w_in, conv_w, a_log, dt_bias, norm_a, norm_c, w_branch, w_out, norm_f):
    n_b, seq, d = x_prompt.shape
    n_db, dec_seq, _ = x_sample.shape
    depth = w_in.shape[0]
    n_pages = page_table.shape[1]
    past_len = n_pages * cache_k.shape[2]
    assert d == 2 * BRANCH_W and dec_seq <= SUBLANES and dec_seq >= CONV_W - 1
    assert seq % GDN_CHUNK == 0 and seq % RET_CHUNK == 0
    t_pad = SUBLANES

    ba0 = 4 * BRANCH_W
    w_main = jnp.concatenate([w_in[:, :, :ba0], w_in[:, :, ba0 + 2 * N_HEADS:]], axis=2).astype(BF16)
    w_ba = jnp.pad(w_in[:, :, ba0:ba0 + 2 * N_HEADS], ((0, 0), (0, 0), (0, BA_W - 2 * N_HEADS))).astype(BF16)
    w_ada_b = w_ada.astype(BF16)
    w_branch_b = w_branch.astype(BF16)
    w_out_b = w_out.astype(BF16)

    n_c = n_b + n_db
    c_rows = -(-n_c // SUBLANES) * SUBLANES
    c_all = jnp.pad(jnp.concatenate([c_prompt, c_sample], axis=0), ((0, c_rows - n_c), (0, 0)))
    mods = _mod_call(c_all, w_ada_b, b_ada)

    tab_b_p, tab_c_p = _rope_tables(jnp.arange(seq, dtype=jnp.int32))
    tab_b_s, tab_c_s = _rope_tables(past_len + jnp.arange(t_pad, dtype=jnp.int32))

    zeros_state = jnp.zeros((n_b, N_HEADS, HEAD_DIM, HEAD_DIM), F32)
    zeros_hist = jnp.zeros((n_b, SUBLANES, 3 * BRANCH_W), F32)

    xp = x_prompt.reshape(n_b * seq, d)
    xs = x_sample.reshape(n_db * dec_seq, d)
    outs = {k: [] for k in ("kp", "vp", "ks", "vs", "gp", "gs", "cp", "cs", "rp", "rs")}
    y_p = y_s = None
    for l in range(depth):
        final = l == depth - 1
        alog, dtb = _alpha_lanes(a_log[l]), _alpha_lanes(dt_bias[l])
        na, nc = norm_a[l].reshape(1, HEAD_DIM), norm_c[l].reshape(1, HEAD_DIM)
        shift, scale, gate = jnp.split(mods[l], 3, axis=-1)

        mod_p = [t[:n_b].reshape(n_b, 1, d) for t in (scale, shift, gate)]
        proj, ba = _inproj_call(xp, mod_p[0], mod_p[1], norm_in[l], w_main[l], w_ba[l],
                                tm=min(seq, 1024), rows_per_mod=seq)
        q_rot, k_rot, k_out, v_out = _rope_b_call(proj, tab_b_p, tr=min(seq, 512), t_rows=seq, cache_layout=True)
        y_a, gdn_new = _gdn_call(proj, ba, zeros_hist, zeros_state, conv_w[l], alog, dtb, na,
                                 n_b=n_b, t_rows=seq, C=GDN_CHUNK, n_valid=GDN_CHUNK, n_sub=GDN_CHUNKS_PER_STEP)
        y_c, ret_new = _ret_call(proj, tab_c_p[0], tab_c_p[1], zeros_state, nc,
                                 n_b=n_b, t_rows=seq, C=RET_CHUNK, n_valid=RET_CHUNK, n_sub=RET_CHUNKS_PER_STEP)
        y_b = _moba_p_call(q_rot, k_rot, proj, n_b=n_b, t_rows=seq)
        res = _merge_call(y_a, y_b, y_c, proj, xp, mod_p[2], w_branch_b[l], w_out_b[l], norm_f,
                          tm=min(seq, 256), rows_per_mod=seq, final=final)
        xp = res[0]
        if final:
            y_p = res[1]
        proj3 = proj.reshape(n_b, seq, MAIN_W)
        outs["kp"].append(k_out.reshape(n_b, seq, N_HEADS, HEAD_DIM))
        outs["vp"].append(v_out.reshape(n_b, seq, N_HEADS, HEAD_DIM))
        outs["gp"].append(gdn_new)
        outs["cp"].append(proj3[:, seq - (CONV_W - 1):, :3 * BRANCH_W])
        outs["rp"].append(ret_new)

        mod_s = [jnp.repeat(t[n_b:n_c], dec_seq, axis=0) for t in (scale, shift, gate)]
        proj_s, ba_s = _inproj_call(xs, mod_s[0], mod_s[1], norm_in[l], w_main[l], w_ba[l],
                                    tm=n_db * dec_seq, rows_per_mod=dec_seq)
        proj_sp = _pad_rows(proj_s, n_db, dec_seq, t_pad)
        ba_sp = _pad_rows(ba_s, n_db, dec_seq, t_pad)
        hist = jnp.pad(state_conv[l], ((0, 0), (SUBLANES - (CONV_W - 1), 0), (0, 0)))
        q_rot_s, k_rot_s = _rope_b_call(proj_sp, tab_b_s, tr=t_pad, t_rows=t_pad, cache_layout=False)
        y_a_s, gdn_new_s = _gdn_call(proj_sp, ba_sp, hist, state_gdn[l], conv_w[l], alog, dtb, na,
                                     n_b=n_db, t_rows=t_pad, C=t_pad, n_valid=dec_seq, n_sub=1)
        y_c_s, ret_new_s = _ret_call(proj_sp, tab_c_s[0], tab_c_s[1], state_ret[l], nc,
                                     n_b=n_db, t_rows=t_pad, C=t_pad, n_valid=dec_seq, n_sub=1)
        y_b_s = _moba_s_call(page_table, q_rot_s, k_rot_s, proj_sp, cache_k, cache_v, l, n_valid=dec_seq)
        unpad = lambda y: y.reshape(n_db, t_pad, BRANCH_W)[:, :dec_seq].reshape(n_db * dec_seq, BRANCH_W)
        res_s = _merge_call(unpad(y_a_s), unpad(y_b_s), unpad(y_c_s), proj_s, xs, mod_s[2],
                            w_branch_b[l], w_out_b[l], norm_f, tm=n_db * dec_seq, rows_per_mod=dec_seq, final=final)
        xs = res_s[0]
        if final:
            y_s = res_s[1]
        proj_s3 = proj_s.reshape(n_db, dec_seq, MAIN_W)
        outs["ks"].append(k_rot_s.reshape(n_db, t_pad, N_HEADS, HEAD_DIM)[:, :dec_seq])
        outs["vs"].append(proj_s3[:, :, COL_V_B * BRANCH_W:(COL_V_B + 1) * BRANCH_W]
                          .reshape(n_db, dec_seq, N_HEADS, HEAD_DIM))
        outs["gs"].append(gdn_new_s)
        outs["cs"].append(proj_s3[:, dec_seq - (CONV_W - 1):, :3 * BRANCH_W])
        outs["rs"].append(ret_new_s)

    st = {k: jnp.stack(v) for k, v in outs.items()}
    return (y_p.reshape(n_b, seq, d), y_s.reshape(n_db, dec_seq, d),
            st["kp"], st["vp"], st["ks"], st["vs"], st["gp"], st["gs"],
            st["cp"], st["cs"], st["rp"], st["rs"])
```

```python
import functools
import math

import jax
import jax.numpy as jnp
from jax import lax
from jax.experimental import pallas as pl
from jax.experimental.pallas import tpu as pltpu

F32 = jnp.float32
BF16 = jnp.bfloat16
HI = lax.Precision.HIGHEST

HEAD_DIM = 128
N_HEADS = 4
BRANCH_W = N_HEADS * HEAD_DIM
N_BRANCH = 3
CONV_W = 4
GDN_CHUNK = 64
RET_CHUNK = 64
MOBA_BLOCK = 256
MOBA_TOPK = 3
ROPE_THETA = 500000.0
ROPE_DIMS = HEAD_DIM // 4
RET_THETA = 10000.0
EPS = 1e-6
NEG = -1e30
SUBLANES = 8
QK_SCALE = HEAD_DIM ** -0.5
LOG2_E = math.log2(math.e)

COL_QKV_A, COL_Z_A, COL_Q_B, COL_K_B, COL_V_B, COL_Z_B = 0, 3, 4, 5, 6, 7
COL_Q_C, COL_K_C, COL_V_C, COL_Z_C, COL_MERGE = 8, 9, 10, 11, 12
MAIN_W = 18 * BRANCH_W
BA_W = 128

NT = (((1,), (1,)), ((), ()))
TN = (((0,), (0,)), ((), ()))

VMEM_LIMIT = 48 * 1024 * 1024


def _params(*sem):
    return pltpu.CompilerParams(dimension_semantics=sem, vmem_limit_bytes=VMEM_LIMIT)


def _silu(x):
    return x * jax.nn.sigmoid(x)


def _bdot(a, b):
    return jnp.dot(a.astype(BF16), b.astype(BF16), preferred_element_type=F32)


def _bdot_g(a, b, dims):
    return lax.dot_general(a.astype(BF16), b.astype(BF16), dims, preferred_element_type=F32)


def _hdot(a, b):
    return jnp.dot(a, b, precision=HI, preferred_element_type=F32)


def _mod_kernel(c_ref, w_ref, b_ref, o_ref):
    o_ref[...] = _bdot(_silu(c_ref[...]), w_ref[...]) + b_ref[...]


def _mod_call(c_all, w_ada, b_ada):
    depth, d, d3 = w_ada.shape
    rows = c_all.shape[0]
    tn = d
    return pl.pallas_call(
        _mod_kernel,
        grid=(depth, d3 // tn),
        in_specs=[pl.BlockSpec((rows, d), lambda l, j: (0, 0)),
                  pl.BlockSpec((None, d, tn), lambda l, j: (l, 0, j)),
                  pl.BlockSpec((None, 1, tn), lambda l, j: (l, 0, j))],
        out_specs=pl.BlockSpec((None, rows, tn), lambda l, j: (l, 0, j)),
        out_shape=jax.ShapeDtypeStruct((depth, rows, d3), F32),
        compiler_params=_params("arbitrary", "arbitrary"),
        name="mod",
    )(c_all, w_ada, b_ada.reshape(depth, 1, d3))


def _inproj_kernel(x_ref, sc_ref, sh_ref, g_ref, w_ref, wba_ref, o_ref, ba_ref, h_scr):
    @pl.when(pl.program_id(1) == 0)
    def _():
        x = x_ref[...]
        y = x * lax.rsqrt(jnp.mean(x * x, axis=-1, keepdims=True) + EPS) * g_ref[...]
        h = (y * (1.0 + sc_ref[...]) + sh_ref[...]).astype(BF16)
        h_scr[...] = h
        ba_ref[...] = jnp.dot(h, wba_ref[...], preferred_element_type=F32)

    o_ref[...] = jnp.dot(h_scr[...], w_ref[...], preferred_element_type=F32)


def _inproj_call(x2d, scale, shift, norm_g, w_main, w_ba, *, tm, rows_per_mod):
    m, d = x2d.shape
    tn = 1536
    assert m % tm == 0 and MAIN_W % tn == 0
    if scale.ndim == 3:
        assert rows_per_mod % tm == 0
        mod_spec = pl.BlockSpec((None, 1, d), lambda i, j: ((i * tm) // rows_per_mod, 0, 0))
    else:
        mod_spec = pl.BlockSpec((tm, d), lambda i, j: (i, 0))
    return pl.pallas_call(
        _inproj_kernel,
        grid=(m // tm, MAIN_W // tn),
        in_specs=[pl.BlockSpec((tm, d), lambda i, j: (i, 0)),
                  mod_spec, mod_spec,
                  pl.BlockSpec((1, d), lambda i, j: (0, 0)),
                  pl.BlockSpec((d, tn), lambda i, j: (0, j)),
                  pl.BlockSpec((d, BA_W), lambda i, j: (0, 0))],
        out_specs=[pl.BlockSpec((tm, tn), lambda i, j: (i, j)),
                   pl.BlockSpec((tm, BA_W), lambda i, j: (i, 0))],
        out_shape=[jax.ShapeDtypeStruct((m, MAIN_W), F32),
                   jax.ShapeDtypeStruct((m, BA_W), F32)],
        scratch_shapes=[pltpu.VMEM((tm, d), BF16)],
        compiler_params=_params("arbitrary", "arbitrary"),
        name="in_proj",
    )(x2d, scale, shift, norm_g.reshape(1, d), w_main, w_ba)


def _rope_b_kernel(q_ref, k_ref, v_ref, c_ref, s1_ref, s2_ref, qo_ref, ko_ref, *cache_refs):
    c, s1, s2 = c_ref[...], s1_ref[...], s2_ref[...]
    half = ROPE_DIMS // 2
    tr = q_ref.shape[0]
    for h in range(N_HEADS):
        sl = slice(h * HEAD_DIM, (h + 1) * HEAD_DIM)
        q = q_ref[:, sl]
        k = k_ref[:, sl]
        qr = q * c + pltpu.roll(q, half, 1) * s1 + pltpu.roll(q, HEAD_DIM - half, 1) * s2
        kr = k * c + pltpu.roll(k, half, 1) * s1 + pltpu.roll(k, HEAD_DIM - half, 1) * s2
        qo_ref[:, sl] = qr * QK_SCALE
        ko_ref[:, sl] = kr
        if cache_refs:
            kc_ref, vc_ref = cache_refs
            kc_ref[pl.ds(h, tr, stride=N_HEADS), :] = kr
            vc_ref[pl.ds(h, tr, stride=N_HEADS), :] = v_ref[:, sl]


def _rope_b_call(proj, tabs, *, tr, t_rows, cache_layout):
    m = proj.shape[0]
    nt = t_rows // tr
    assert t_rows % tr == 0 and m % t_rows == 0
    tab_spec = pl.BlockSpec((tr, HEAD_DIM), lambda i: (i % nt, 0))
    col_spec = lambda col: pl.BlockSpec((tr, BRANCH_W), lambda i: (i, col))
    out_specs = [col_spec(0), col_spec(0)]
    out_shape = [jax.ShapeDtypeStruct((m, BRANCH_W), F32)] * 2
    if cache_layout:
        out_specs += [pl.BlockSpec((tr * N_HEADS, HEAD_DIM), lambda i: (i, 0))] * 2
        out_shape += [jax.ShapeDtypeStruct((m * N_HEADS, HEAD_DIM), F32)] * 2
    return pl.pallas_call(
        _rope_b_kernel,
        grid=(m // tr,),
        in_specs=[col_spec(COL_Q_B), col_spec(COL_K_B), col_spec(COL_V_B), tab_spec, tab_spec, tab_spec],
        out_specs=out_specs,
        out_shape=out_shape,
        compiler_params=_params("arbitrary"),
        name="rope_b",
    )(proj, proj, proj, *tabs)


INV_BASE = 8
GDN_CHUNKS_PER_STEP = 4


def _bdot_each(xs, ys):
    return [_bdot(x, y) for x, y in zip(xs, ys)]


def _unit_lower_inverses(mats, ri, ci, n):
    base = min(INV_BASE, n)
    eye = (ri == ci).astype(F32)
    dpows = [jnp.where((ri // base) == (ci // base), a, 0.0) for a in mats]
    invs = [eye + d for d in dpows]
    for _ in range(int(math.log2(base)) - 1):
        dpows = _bdot_each(dpows, dpows)
        invs = [inv + t for inv, t in zip(invs, _bdot_each(invs, dpows))]
    size = base
    while size < n:
        off = ((ri // (2 * size)) == (ci // (2 * size))) & ((ri // size) != (ci // size))
        es = [jnp.where(off, a, 0.0) for a in mats]
        invs = [inv + t for inv, t in zip(invs, _bdot_each(invs, _bdot_each(es, invs)))]
        size *= 2
    return invs


def _gdn_kernel(qkv_ref, hist_ref, ba_ref, z_ref, s0_ref, cw_ref, alog_ref, dtb_ref, na_ref,
                y_ref, sout_ref, s_scr, tail_scr, *, C, n_valid, n_sub):
    c_idx = pl.program_id(1)
    rows = n_sub * C

    @pl.when(c_idx == 0)
    def _():
        s_scr[...] = s0_ref[...]
        tail_scr[...] = hist_ref[...]

    u = qkv_ref[...]
    prev = tail_scr[...]
    w = cw_ref[...]
    acc = u * w[CONV_W - 1:CONV_W, :]
    row8 = lax.broadcasted_iota(jnp.int32, (SUBLANES, u.shape[1]), 0)
    for s in range(1, CONV_W):
        rolled = pltpu.roll(u, s, 0)
        first = jnp.where(row8 < s, pltpu.roll(prev, s, 0), rolled[:SUBLANES])
        shifted = first if rows == SUBLANES else jnp.concatenate([first, rolled[SUBLANES:]], axis=0)
        acc = acc + shifted * w[CONV_W - 1 - s:CONV_W - s, :]
    tail_scr[...] = u[rows - SUBLANES:, :]
    conv = _silu(acc)

    ba = ba_ref[...]
    beta_t = jax.nn.sigmoid(ba)
    xg = ba + dtb_ref[...]
    g_t = -jnp.exp(alog_ref[...]) * (jnp.maximum(xg, 0.0) + jnp.log1p(jnp.exp(-jnp.abs(xg))))
    row_t = lax.broadcasted_iota(jnp.int32, ba.shape, 0) % C
    if n_valid < C:
        beta_t = jnp.where(row_t < n_valid, beta_t, 0.0)
        g_t = jnp.where(row_t < n_valid, g_t, 0.0)
    gc_t = g_t
    shift = 1
    while shift < C:
        gc_t = gc_t + jnp.where(row_t >= shift, pltpu.roll(gc_t, shift, 0), 0.0)
        shift *= 2

    cs = N_HEADS * C
    subs = range(n_sub)
    stack = lambda f: jnp.concatenate([f(h) for h in range(N_HEADS)], axis=0)
    ri = lax.broadcasted_iota(jnp.int32, (cs, cs), 0)
    ci = lax.broadcasted_iota(jnp.int32, (cs, cs), 1)
    same_head = (ri // C) == (ci // C)
    tri = same_head & (ri >= ci)
    strict = same_head & (ri > ci)

    qn, kn, kb, vb, gc, g_last, decay = [], [], [], [], [], [], []
    for j in subs:
        r0 = j * C
        lanes = lambda t, col: jnp.broadcast_to(t[r0:r0 + C, col:col + 1], (C, HEAD_DIM))
        head_cols = lambda base: stack(lambda h: conv[r0:r0 + C, base + h * HEAD_DIM:base + (h + 1) * HEAD_DIM])
        q, k, v = head_cols(0), head_cols(BRANCH_W), head_cols(2 * BRANCH_W)
        beta = stack(lambda h: lanes(beta_t, h))
        gc_j = stack(lambda h: lanes(gc_t, N_HEADS + h))
        if cs % HEAD_DIM == 0:
            gc_row = jnp.concatenate([gc_j.T] * (cs // HEAD_DIM), axis=0)
            gc_col = jnp.concatenate([gc_j] * (cs // HEAD_DIM), axis=1)
        else:
            g_st = stack(lambda h: lanes(g_t, N_HEADS + h))
            upper = (same_head & (ri <= ci)).astype(F32)
            gc_row = _hdot(jnp.ones((cs, cs), F32), g_st[:, :cs] * upper)
            gc_col = gc_j[:, :cs]
        kn_j = k * lax.rsqrt(jnp.sum(k * k, axis=-1, keepdims=True) + EPS)
        qn.append(q * lax.rsqrt(jnp.sum(q * q, axis=-1, keepdims=True) + EPS) * QK_SCALE)
        kn.append(kn_j)
        kb.append(kn_j * beta)
        vb.append(v * beta)
        gc.append(gc_j)
        g_last.append(stack(lambda h: jnp.broadcast_to(
            gc_t[r0 + C - 1:r0 + C, N_HEADS + h:N_HEADS + h + 1], (C, HEAD_DIM))))
        decay.append(jnp.where(tri, jnp.exp(jnp.where(tri, gc_col - gc_row, 0.0)), 0.0))

    kk = [_bdot_g(kb[j], kn[j], NT) for j in subs]
    qk = [_bdot_g(qn[j], kn[j], NT) for j in subs]
    invs = _unit_lower_inverses([-jnp.where(strict, kk[j] * decay[j], 0.0) for j in subs], ri, ci, C)
    eg = [jnp.exp(gc[j]) for j in subs]
    sols = _bdot_each(invs, [jnp.concatenate([vb[j], kb[j] * eg[j]], axis=1) for j in subs])
    attn = [qk[j] * decay[j] for j in subs]
    q_dec = [qn[j] * eg[j] for j in subs]
    k_dec = [kn[j] * jnp.exp(g_last[j] - gc[j]) for j in subs]

    hrows = [slice(h * C, (h + 1) * C) for h in range(N_HEADS)]
    states = [s_scr[h] for h in range(N_HEADS)]
    for j in subs:
        u_s, w_s = sols[j][:, :HEAD_DIM], sols[j][:, HEAD_DIM:]
        v_new = jnp.concatenate([u_s[hr] - _bdot(w_s[hr], st) for hr, st in zip(hrows, states)], axis=0)
        o = (jnp.concatenate([_bdot(q_dec[j][hr], st) for hr, st in zip(hrows, states)], axis=0)
             + _bdot(attn[j], v_new))
        states = [st * jnp.exp(g_last[j][h * C:h * C + 1, :]) + _bdot_g(k_dec[j][hr], v_new[hr], TN)
                  for h, (hr, st) in enumerate(zip(hrows, states))]
        on = o * lax.rsqrt(jnp.mean(o * o, axis=-1, keepdims=True) + EPS) * na_ref[...]
        for h, hr in enumerate(hrows):
            sl = slice(h * HEAD_DIM, (h + 1) * HEAD_DIM)
            y_ref[j * C:(j + 1) * C, sl] = on[hr] * _silu(z_ref[j * C:(j + 1) * C, sl])
    for h in range(N_HEADS):
        s_scr[h] = states[h]

    @pl.when(c_idx == pl.num_programs(1) - 1)
    def _():
        sout_ref[...] = s_scr[...]


def _gdn_call(proj, ba, hist, s0, conv_w, alog, dtb, norm_a, *, n_b, t_rows, C, n_valid, n_sub):
    tile = n_sub * C
    n_c = t_rows // tile
    assert t_rows % tile == 0
    row = lambda b, c: b * n_c + c
    vec_spec = pl.BlockSpec((1, HEAD_DIM), lambda b, c: (0, 0))
    state_spec = pl.BlockSpec((None, N_HEADS, HEAD_DIM, HEAD_DIM), lambda b, c: (b, 0, 0, 0))
    return pl.pallas_call(
        functools.partial(_gdn_kernel, C=C, n_valid=n_valid, n_sub=n_sub),
        grid=(n_b, n_c),
        in_specs=[pl.BlockSpec((tile, 3 * BRANCH_W), lambda b, c: (row(b, c), 0)),
                  pl.BlockSpec((None, SUBLANES, 3 * BRANCH_W), lambda b, c: (b, 0, 0)),
                  pl.BlockSpec((tile, BA_W), lambda b, c: (row(b, c), 0)),
                  pl.BlockSpec((tile, BRANCH_W), lambda b, c: (row(b, c), COL_Z_A)),
                  state_spec,
                  pl.BlockSpec((CONV_W, 3 * BRANCH_W), lambda b, c: (0, 0)),
                  vec_spec, vec_spec, vec_spec],
        out_specs=[pl.BlockSpec((tile, BRANCH_W), lambda b, c: (row(b, c), 0)), state_spec],
        out_shape=[jax.ShapeDtypeStruct((n_b * t_rows, BRANCH_W), F32),
                   jax.ShapeDtypeStruct((n_b, N_HEADS, HEAD_DIM, HEAD_DIM), F32)],
        scratch_shapes=[pltpu.VMEM((N_HEADS, HEAD_DIM, HEAD_DIM), F32),
                        pltpu.VMEM((SUBLANES, 3 * BRANCH_W), F32)],
        compiler_params=_params("arbitrary", "arbitrary"),
        name="gdn",
    )(proj, hist, ba, proj, s0, conv_w, alog, dtb, norm_a)


RET_CHUNKS_PER_STEP = 4

def _ret_kernel(q_ref, k_ref, v_ref, z_ref, cos_ref, sin_ref, s0_ref, dmat_ref, qdec_ref, kdec_ref,
                cdec_ref, nc_ref, y_ref, sout_ref, s_scr, *, C, n_sub):
    c_idx = pl.program_id(1)

    @pl.when(c_idx == 0)
    def _():
        s_scr[...] = s0_ref[...]

    cos, sin = cos_ref[...], sin_ref[...]
    heads = [slice(h * HEAD_DIM, (h + 1) * HEAD_DIM) for h in range(N_HEADS)]
    units = [(j, h) for j in range(n_sub) for h in range(N_HEADS)]
    rows = lambda j: slice(j * C, (j + 1) * C)
    rope = lambda x: x * cos + pltpu.roll(x, HEAD_DIM // 2, 1) * sin
    qr = [rope(q_ref[:, sl]) for sl in heads]
    kr = [rope(k_ref[:, sl]) * QK_SCALE for sl in heads]
    vs = [v_ref[:, sl] for sl in heads]
    sc = [_bdot_g(qr[h][rows(j)], kr[h][rows(j)], NT) * dmat_ref[h] for j, h in units]
    o_intra = [_bdot(s, vs[h][rows(j)]) for s, (j, h) in zip(sc, units)]
    kv = [_bdot_g(kr[h][rows(j)] * kdec_ref[h], vs[h][rows(j)], TN) for j, h in units]

    states = [s_scr[h] for h in range(N_HEADS)]
    for j in range(n_sub):
        for h, sl in enumerate(heads):
            u = j * N_HEADS + h
            o = o_intra[u] + _bdot(qr[h][rows(j)] * qdec_ref[h], states[h])
            states[h] = states[h] * cdec_ref[h] + kv[u]
            on = o * lax.rsqrt(jnp.mean(o * o, axis=-1, keepdims=True) + EPS) * nc_ref[...]
            y_ref[rows(j), sl] = on * _silu(z_ref[rows(j), sl])
    for h in range(N_HEADS):
        s_scr[h] = states[h]

    @pl.when(c_idx == pl.num_programs(1) - 1)
    def _():
        sout_ref[...] = s_scr[...]


def _ret_tables(C, n_valid):
    log_g = jnp.log1p(-(2.0 ** (-5.0 - jnp.arange(N_HEADS, dtype=F32))))
    idx = jnp.arange(C, dtype=F32)
    tri = jnp.tril(jnp.ones((C, C), dtype=bool))
    dmat = jnp.where(tri, jnp.exp(log_g[:, None, None] * jnp.where(tri, idx[:, None] - idx[None, :], 0.0)), 0.0)
    q_dec = jnp.exp(log_g[:, None] * (idx + 1.0))[..., None]
    k_dec = jnp.exp(log_g[:, None] * (n_valid - 1.0 - idx))[..., None]
    c_dec = jnp.exp(log_g * n_valid)[:, None, None]
    bc = lambda t, r: jnp.broadcast_to(t, (N_HEADS, r, HEAD_DIM))
    return dmat, bc(q_dec, C), bc(k_dec, C), bc(c_dec, 1)


def _ret_call(proj, cos2, sin2, s0, norm_c, *, n_b, t_rows, C, n_valid, n_sub):
    tile = n_sub * C
    n_c = t_rows // tile
    assert t_rows % tile == 0
    row = lambda b, c: b * n_c + c
    dmat, q_dec, k_dec, c_dec = _ret_tables(C, n_valid)
    col_spec = lambda col: pl.BlockSpec((tile, BRANCH_W), lambda b, c: (row(b, c), col))
    tab_spec = pl.BlockSpec((tile, HEAD_DIM), lambda b, c: (c, 0))
    state_spec = pl.BlockSpec((None, N_HEADS, HEAD_DIM, HEAD_DIM), lambda b, c: (b, 0, 0, 0))
    const_spec = lambda r, w: pl.BlockSpec((N_HEADS, r, w), lambda b, c: (0, 0, 0))
    return pl.pallas_call(
        functools.partial(_ret_kernel, C=C, n_sub=n_sub),
        grid=(n_b, n_c),
        in_specs=[col_spec(COL_Q_C), col_spec(COL_K_C), col_spec(COL_V_C), col_spec(COL_Z_C),
                  tab_spec, tab_spec, state_spec,
                  const_spec(C, C), const_spec(C, HEAD_DIM), const_spec(C, HEAD_DIM), const_spec(1, HEAD_DIM),
                  pl.BlockSpec((1, HEAD_DIM), lambda b, c: (0, 0))],
        out_specs=[pl.BlockSpec((tile, BRANCH_W), lambda b, c: (row(b, c), 0)), state_spec],
        out_shape=[jax.ShapeDtypeStruct((n_b * t_rows, BRANCH_W), F32),
                   jax.ShapeDtypeStruct((n_b, N_HEADS, HEAD_DIM, HEAD_DIM), F32)],
        scratch_shapes=[pltpu.VMEM((N_HEADS, HEAD_DIM, HEAD_DIM), F32)],
        compiler_params=_params("arbitrary", "arbitrary"),
        name="ret",
    )(proj, proj, proj, proj, cos2, sin2, s0, dmat, q_dec, k_dec, c_dec, norm_c)


def _topk_mask(gate, valid, axis):
    idx = lax.broadcasted_iota(jnp.int32, gate.shape, axis)
    gm = jnp.where(valid, gate, -jnp.inf)
    rank = jnp.zeros(gate.shape, jnp.int32)
    for m in range(gate.shape[axis]):
        gmm = gm[m:m + 1, :] if axis == 0 else gm[:, m:m + 1]
        rank = rank + jnp.where(gmm > gm, 1, jnp.where(gmm == gm, jnp.where(idx > m, 1, 0), 0))
    return valid & (rank < MOBA_TOPK)


def _moba_p_kernel(q_ref, k_ref, v_ref, z_ref, y_ref, kmean_scr, kb_scr, vt_scr, bias_scr, acc_scr, *, n_blk):
    qi = pl.program_id(1)
    blk = MOBA_BLOCK
    heads = [slice(h * HEAD_DIM, (h + 1) * HEAD_DIM) for h in range(N_HEADS)]

    @pl.when(qi == 0)
    def _():
        def prep(n, carry):
            rows = pl.ds(pl.multiple_of(n * blk, blk), blk)
            for h, sl in enumerate(heads):
                kn = k_ref[rows, sl]
                kmean_scr[h, pl.ds(n, 1), :] = jnp.sum(kn, axis=0, keepdims=True) * (1.0 / blk)
                kb_scr[h * n_blk + n] = kn.astype(BF16)
                vt_scr[h * n_blk + n] = v_ref[rows, sl].T.astype(BF16)
            return carry

        lax.fori_loop(0, n_blk, prep, 0)

    key_i = lax.broadcasted_iota(jnp.int32, (blk, blk), 0)
    qry_i = lax.broadcasted_iota(jnp.int32, (blk, blk), 1)
    blk_id = lax.broadcasted_iota(jnp.int32, (n_blk, blk), 0)
    qs = [q_ref[:, sl] for sl in heads]
    qbs = [(q * LOG2_E).astype(BF16) for q in qs]
    ss = [jnp.where(key_i <= qry_i,
                    lax.dot_general(kb_scr[h * n_blk + qi], qbs[h], NT, preferred_element_type=F32), NEG)
          for h in range(N_HEADS)]
    ms = [jnp.max(s, axis=0, keepdims=True) for s in ss]
    ps = [jnp.exp2(ss[h] - ms[h]) for h in range(N_HEADS)]
    ls = [jnp.sum(p, axis=0, keepdims=True) for p in ps]
    for h in range(N_HEADS):
        acc_scr[h] = jnp.dot(vt_scr[h * n_blk + qi], ps[h].astype(BF16), preferred_element_type=F32)
    for h in range(N_HEADS):
        gate = lax.dot_general(kmean_scr[h], qs[h], NT, precision=HI, preferred_element_type=F32)
        bias_scr[h] = jnp.where(_topk_mask(gate, blk_id < qi, 0), 0.0, NEG)

    hs = range(N_HEADS)
    HEAD_GROUPS = (tuple(hs),)

    def body(i, carry):
        ms, ls = carry
        ns = (2 * i, 2 * i + 1)
        ms_new, ls_new = list(ms), list(ls)
        for grp in HEAD_GROUPS:
            ss = {h: [lax.dot_general(kb_scr[h * n_blk + n], qbs[h], NT, preferred_element_type=F32)
                      + bias_scr[h, pl.ds(n, 1), :] for n in ns] for h in grp}
            for h in grp:
                ms_new[h] = jnp.maximum(ms[h], jnp.max(jnp.maximum(ss[h][0], ss[h][1]), axis=0, keepdims=True))
            ps = {h: [jnp.exp2(s - ms_new[h]) for s in ss[h]] for h in grp}
            alphas = {h: jnp.exp2(ms[h] - ms_new[h]) for h in grp}
            pvs = {h: [jnp.dot(vt_scr[h * n_blk + n], p.astype(BF16), preferred_element_type=F32)
                       for n, p in zip(ns, ps[h])] for h in grp}
            for h in grp:
                ls_new[h] = alphas[h] * ls[h] + jnp.sum(ps[h][0] + ps[h][1], axis=0, keepdims=True)
                acc_scr[h] = alphas[h] * acc_scr[h] + (pvs[h][0] + pvs[h][1])
        return tuple(ms_new), tuple(ls_new)

    _, ls = lax.fori_loop(0, (qi + 1) // 2, body, (tuple(ms), tuple(ls)))
    for h, sl in enumerate(heads):
        y_ref[:, sl] = (acc_scr[h] / ls[h]).T * _silu(z_ref[:, sl])


def _moba_p_call(q_rot, k_rot, proj, *, n_b, t_rows):
    n_blk = t_rows // MOBA_BLOCK
    assert t_rows % MOBA_BLOCK == 0 and n_blk >= MOBA_TOPK
    blk = MOBA_BLOCK
    seq_spec = lambda col: pl.BlockSpec((t_rows, BRANCH_W), lambda b, i: (b, col), pipeline_mode=pl.Buffered(1))
    tile_spec = lambda col: pl.BlockSpec((blk, BRANCH_W), lambda b, i: (b * n_blk + i, col))
    return pl.pallas_call(
        functools.partial(_moba_p_kernel, n_blk=n_blk),
        grid=(n_b, n_blk),
        in_specs=[tile_spec(0), seq_spec(0), seq_spec(COL_V_B), tile_spec(COL_Z_B)],
        out_specs=tile_spec(0),
        out_shape=jax.ShapeDtypeStruct((n_b * t_rows, BRANCH_W), F32),
        scratch_shapes=[pltpu.VMEM((N_HEADS, n_blk, HEAD_DIM), F32),
                        pltpu.VMEM((N_HEADS * n_blk, blk, HEAD_DIM), BF16),
                        pltpu.VMEM((N_HEADS * n_blk, HEAD_DIM, blk), BF16),
                        pltpu.VMEM((N_HEADS, n_blk, blk), F32),
                        pltpu.VMEM((N_HEADS, HEAD_DIM, blk), F32)],
        compiler_params=_params("arbitrary", "arbitrary"),
        name="moba_p",
    )(q_rot, k_rot, proj, proj)


PAGES_PER_STEP = 16


def _moba_s_kernel(pt_ref, q_ref, kn_ref, vn_ref, z_ref, *refs, n_valid, pages_per_blk, pps):
    del pt_ref
    k_refs, v_refs = refs[:pps], refs[pps:2 * pps]
    y_ref, q_scr, ksum_scr, m_scr, l_scr, o_scr = refs[2 * pps:]
    step = pl.program_id(1)
    rows = N_HEADS * SUBLANES
    page_rows = k_refs[0].shape[0]
    heads = [slice(h * HEAD_DIM, (h + 1) * HEAD_DIM) for h in range(N_HEADS)]
    hrows = [slice(h * SUBLANES, (h + 1) * SUBLANES) for h in range(N_HEADS)]

    @pl.when(step == 0)
    def _():
        q = q_ref[...]
        q_scr[...] = jnp.concatenate([q[:, sl] for sl in heads], axis=0)

    q32 = q_scr[...]
    qb = q32.astype(BF16)
    row_head = lax.broadcasted_iota(jnp.int32, (rows, page_rows), 0) // SUBLANES
    col_head = lax.broadcasted_iota(jnp.int32, (rows, page_rows), 1) % N_HEADS
    head_bias = jnp.where(row_head == col_head, 0.0, NEG)

    blks_per_step = pps // pages_per_blk
    kps = [k_refs[j][...] for j in range(pps)]
    ss = [_bdot_g(qb, kp, NT) + head_bias for kp in kps]
    ms = [jnp.max(s, axis=1, keepdims=True) for s in ss]
    es = [jnp.exp(s - m) for s, m in zip(ss, ms)]
    os_ = [_bdot(e, v_refs[j][...]) for j, e in enumerate(es)]
    for j in range(pps):
        pg = step * pps + j
        m_scr[pg] = jnp.broadcast_to(ms[j], (rows, HEAD_DIM))
        l_scr[pg] = jnp.broadcast_to(jnp.sum(es[j], axis=1, keepdims=True), (rows, HEAD_DIM))
        o_scr[pg] = os_[j]
    parts = [jnp.sum(kp.reshape(page_rows // SUBLANES, SUBLANES, HEAD_DIM), axis=0) for kp in kps]
    for bl in range(blks_per_step):
        ksum = parts[bl * pages_per_blk]
        for j in range(bl * pages_per_blk + 1, (bl + 1) * pages_per_blk):
            ksum = ksum + parts[j]
        ksum_scr[step * blks_per_step + bl] = ksum

    @pl.when(step == pl.num_programs(1) - 1)
    def _():
        n_pg = m_scr.shape[0]
        n_blk = ksum_scr.shape[0]
        kflat = ksum_scr[...].reshape(n_blk * SUBLANES, HEAD_DIM) * (1.0 / MOBA_BLOCK)
        g_all = lax.dot_general(q32, kflat, NT, precision=HI, preferred_element_type=F32)
        rh = lax.broadcasted_iota(jnp.int32, g_all.shape, 0) // SUBLANES
        ch = lax.broadcasted_iota(jnp.int32, g_all.shape, 1) % N_HEADS
        pool = (lax.broadcasted_iota(jnp.int32, (n_blk * SUBLANES, n_blk), 0) // SUBLANES
                == lax.broadcasted_iota(jnp.int32, (n_blk * SUBLANES, n_blk), 1)).astype(F32)
        gate = _hdot(jnp.where(rh == ch, g_all, 0.0), pool)
        sel_f = jnp.where(_topk_mask(gate, jnp.ones(gate.shape, jnp.bool_), 1), 1.0, 0.0)
        selw = [jnp.broadcast_to(sel_f[:, n:n + 1], (rows, HEAD_DIM)) > 0.5 for n in range(n_blk)]

        kn, vn = kn_ref[...], vn_ref[...]
        s_own = jnp.concatenate([_bdot_g(q32[hr], kn[:, sl], NT) for hr, sl in zip(hrows, heads)], axis=0)
        rq = lax.broadcasted_iota(jnp.int32, s_own.shape, 0) % SUBLANES
        cj = lax.broadcasted_iota(jnp.int32, s_own.shape, 1)
        own_ok = (cj <= rq) & (cj < n_valid)
        s_own = jnp.where(own_ok, s_own, NEG)
        mx = jnp.broadcast_to(jnp.max(s_own, axis=1, keepdims=True), (rows, HEAD_DIM))
        for pg in range(n_pg):
            mx = jnp.maximum(mx, jnp.where(selw[pg // pages_per_blk], m_scr[pg], NEG))
        e_own = jnp.where(own_ok, jnp.exp(s_own - mx[:, :SUBLANES]), 0.0)
        l_tot = jnp.broadcast_to(jnp.sum(e_own, axis=1, keepdims=True), (rows, HEAD_DIM))
        o_tot = jnp.concatenate([_bdot(e_own[hr], vn[:, sl]) for hr, sl in zip(hrows, heads)], axis=0)
        for pg in range(n_pg):
            wgt = jnp.where(selw[pg // pages_per_blk], jnp.exp(jnp.minimum(m_scr[pg] - mx, 0.0)), 0.0)
            l_tot = l_tot + wgt * l_scr[pg]
            o_tot = o_tot + wgt * o_scr[pg]
        o = o_tot / l_tot
        y_ref[...] = jnp.concatenate([o[hr] for hr in hrows], axis=1) * _silu(z_ref[...])


def _moba_s_call(page_table, q_rot, k_rot, proj, cache_k, cache_v, layer, *, n_valid):
    n_b, n_pages = page_table.shape
    depth, n_pool, page = cache_k.shape[:3]
    assert MOBA_BLOCK % page == 0
    pages_per_blk = MOBA_BLOCK // page
    pps = PAGES_PER_STEP
    assert pps % pages_per_blk == 0 and n_pages % pps == 0 and n_pages // pages_per_blk >= MOBA_TOPK
    assert (page * N_HEADS) % SUBLANES == 0 and SUBLANES % N_HEADS == 0
    ck = cache_k.reshape(depth, n_pool, page * N_HEADS, HEAD_DIM)
    cv = cache_v.reshape(depth, n_pool, page * N_HEADS, HEAD_DIM)
    rows = N_HEADS * SUBLANES
    row_spec = lambda col: pl.BlockSpec((SUBLANES, BRANCH_W), lambda b, p, pt: (b, col))

    def page_spec(j):
        return pl.BlockSpec((None, None, page * N_HEADS, HEAD_DIM),
                            lambda b, p, pt: (layer, pt[b, p * pps + j], 0, 0))

    page_specs = [page_spec(j) for j in range(pps)]
    grid_spec = pltpu.PrefetchScalarGridSpec(
        num_scalar_prefetch=1,
        grid=(n_b, n_pages // pps),
        in_specs=[row_spec(0), row_spec(0), row_spec(COL_V_B), row_spec(COL_Z_B)] + page_specs + page_specs,
        out_specs=pl.BlockSpec((SUBLANES, BRANCH_W), lambda b, p, pt: (b, 0)),
        scratch_shapes=[pltpu.VMEM((rows, HEAD_DIM), F32),
                        pltpu.VMEM((n_pages // pages_per_blk, SUBLANES, HEAD_DIM), F32),
                        pltpu.VMEM((n_pages, rows, HEAD_DIM), F32),
                        pltpu.VMEM((n_pages, rows, HEAD_DIM), F32),
                        pltpu.VMEM((n_pages, rows, HEAD_DIM), F32)])
    return pl.pallas_call(
        functools.partial(_moba_s_kernel, n_valid=n_valid, pages_per_blk=pages_per_blk, pps=pps),
        grid_spec=grid_spec,
        out_shape=jax.ShapeDtypeStruct((n_b * SUBLANES, BRANCH_W), F32),
        compiler_params=_params("arbitrary", "arbitrary"),
        name="moba_s",
    )(page_table, q_rot, k_rot, proj, proj, *([ck] * pps), *([cv] * pps))


def _merge_kernel(ya_ref, yb_ref, yc_ref, mg_ref, x_ref, gate_ref, wb_ref, wo_ref, nf_ref, *out_refs, final):
    d = x_ref.shape[1]
    mixed = None
    for n, y_ref in enumerate((ya_ref, yb_ref, yc_ref)):
        per_branch = _bdot(y_ref[...], wb_ref[n])
        term = jax.nn.sigmoid(mg_ref[:, n * d:(n + 1) * d]) * per_branch
        mixed = term if mixed is None else mixed + term
    x_out = x_ref[...] + gate_ref[...] * _bdot(mixed, wo_ref[...])
    out_refs[0][...] = x_out
    if final:
        out_refs[1][...] = (x_out * lax.rsqrt(jnp.mean(x_out * x_out, axis=-1, keepdims=True) + EPS)
                            * nf_ref[...])


def _merge_call(y_a, y_b, y_c, proj, x2d, gate, w_branch, w_out, norm_f, *, tm, rows_per_mod, final):
    m, d = x2d.shape
    assert m % tm == 0
    if gate.ndim == 3:
        assert rows_per_mod % tm == 0
        gate_spec = pl.BlockSpec((None, 1, d), lambda i: ((i * tm) // rows_per_mod, 0, 0))
    else:
        gate_spec = pl.BlockSpec((tm, d), lambda i: (i, 0))
    y_spec = pl.BlockSpec((tm, BRANCH_W), lambda i: (i, 0))
    x_spec = pl.BlockSpec((tm, d), lambda i: (i, 0))
    n_out = 2 if final else 1
    outs = pl.pallas_call(
        functools.partial(_merge_kernel, final=final),
        grid=(m // tm,),
        in_specs=[y_spec, y_spec, y_spec,
                  pl.BlockSpec((tm, N_BRANCH * d), lambda i: (i, (COL_MERGE * BRANCH_W) // (N_BRANCH * d))),
                  x_spec, gate_spec,
                  pl.BlockSpec((N_BRANCH, BRANCH_W, d), lambda i: (0, 0, 0)),
                  pl.BlockSpec((d, d), lambda i: (0, 0)),
                  pl.BlockSpec((1, d), lambda i: (0, 0))],
        out_specs=[x_spec] * n_out,
        out_shape=[jax.ShapeDtypeStruct((m, d), F32)] * n_out,
        compiler_params=_params("arbitrary"),
        name="merge",
    )(y_a, y_b, y_c, proj, x2d, gate, w_branch, w_out, norm_f.reshape(1, d))
    return outs


def _rope_tables(pos):
    t = pos.shape[0]
    posf = pos.astype(F32)

    def cos_sin(n_rot, theta):
        half = n_rot // 2
        inv = theta ** (-jnp.arange(half, dtype=F32) / half)
        ang = posf[:, None] * inv[None, :]
        return jnp.cos(ang), jnp.sin(ang)

    cb, sb = cos_sin(ROPE_DIMS, ROPE_THETA)
    hb = ROPE_DIMS // 2
    tab_b = (jnp.concatenate([cb, cb, jnp.ones((t, HEAD_DIM - ROPE_DIMS), F32)], axis=1),
             jnp.concatenate([jnp.zeros((t, hb), F32), sb, jnp.zeros((t, HEAD_DIM - ROPE_DIMS), F32)], axis=1),
             jnp.concatenate([-sb, jnp.zeros((t, HEAD_DIM - hb), F32)], axis=1))
    cc, sc = cos_sin(HEAD_DIM, RET_THETA)
    tab_c = (jnp.concatenate([cc, cc], axis=1), jnp.concatenate([-sc, sc], axis=1))
    return tab_b, tab_c


def _pad_rows(a2d, n_b, t, t_pad):
    w = a2d.shape[1]
    return jnp.pad(a2d.reshape(n_b, t, w), ((0, 0), (0, t_pad - t), (0, 0))).reshape(n_b * t_pad, w)


def _alpha_lanes(v):
    return jnp.pad(v.astype(F32), (N_HEADS, BA_W - 2 * N_HEADS)).reshape(1, BA_W)


def kernel(x_prompt, x_sample, cache_k, cache_v, state_gdn, state_conv, state_ret, page_table, c_prompt, c_sample,
           norm_in, w_ada, b_ada, w_in, conv_w, a_log, dt_bias, norm_a, norm_c, w_branch, w_out, norm_f):
    n_b, seq, d = x_prompt.shape
    n_db, dec_seq, _ = x_sample.shape
    depth = w_in.shape[0]
    n_pages = page_table.shape[1]
    past_len = n_pages * cache_k.shape[2]
    assert d == 2 * BRANCH_W and dec_seq <= SUBLANES and dec_seq >= CONV_W - 1
    assert seq % GDN_CHUNK == 0 and seq % RET_CHUNK == 0
    t_pad = SUBLANES

    ba0 = 4 * BRANCH_W
    w_main = jnp.concatenate([w_in[:, :, :ba0], w_in[:, :, ba0 + 2 * N_HEADS:]], axis=2).astype(BF16)
    w_ba = jnp.pad(w_in[:, :, ba0:ba0 + 2 * N_HEADS], ((0, 0), (0, 0), (0, BA_W - 2 * N_HEADS))).astype(BF16)
    w_ada_b = w_ada.astype(BF16)
    w_branch_b = w_branch.astype(BF16)
    w_out_b = w_out.astype(BF16)

    n_c = n_b + n_db
    c_rows = -(-n_c // SUBLANES) * SUBLANES
    c_all = jnp.pad(jnp.concatenate([c_prompt, c_sample], axis=0), ((0, c_rows - n_c), (0, 0)))
    mods = _mod_call(c_all, w_ada_b, b_ada)

    tab_b_p, tab_c_p = _rope_tables(jnp.arange(seq, dtype=jnp.int32))
    tab_b_s, tab_c_s = _rope_tables(past_len + jnp.arange(t_pad, dtype=jnp.int32))

    zeros_state = jnp.zeros((n_b, N_HEADS, HEAD_DIM, HEAD_DIM), F32)
    zeros_hist = jnp.zeros((n_b, SUBLANES, 3 * BRANCH_W), F32)

    xp = x_prompt.reshape(n_b * seq, d)
    xs = x_sample.reshape(n_db * dec_seq, d)
    outs = {k: [] for k in ("kp", "vp", "ks", "vs", "gp", "gs", "cp", "cs", "rp", "rs")}
    y_p = y_s = None
    for l in range(depth):
        final = l == depth - 1
        alog, dtb = _alpha_lanes(a_log[l]), _alpha_lanes(dt_bias[l])
        na, nc = norm_a[l].reshape(1, HEAD_DIM), norm_c[l].reshape(1, HEAD_DIM)
        shift, scale, gate = jnp.split(mods[l], 3, axis=-1)

        mod_p = [t[:n_b].reshape(n_b, 1, d) for t in (scale, shift, gate)]
        proj, ba = _inproj_call(xp, mod_p[0], mod_p[1], norm_in[l], w_main[l], w_ba[l],
                                tm=min(seq, 1024), rows_per_mod=seq)
        q_rot, k_rot, k_out, v_out = _rope_b_call(proj, tab_b_p, tr=min(seq, 512), t_rows=seq, cache_layout=True)
        y_a, gdn_new = _gdn_call(proj, ba, zeros_hist, zeros_state, conv_w[l], alog, dtb, na,
                                 n_b=n_b, t_rows=seq, C=GDN_CHUNK, n_valid=GDN_CHUNK, n_sub=GDN_CHUNKS_PER_STEP)
        y_c, ret_new = _ret_call(proj, tab_c_p[0], tab_c_p[1], zeros_state, nc,
                                 n_b=n_b, t_rows=seq, C=RET_CHUNK, n_valid=RET_CHUNK, n_sub=RET_CHUNKS_PER_STEP)
        y_b = _moba_p_call(q_rot, k_rot, proj, n_b=n_b, t_rows=seq)
        res = _merge_call(y_a, y_b, y_c, proj, xp, mod_p[2], w_branch_b[l], w_out_b[l], norm_f,
                          tm=min(seq, 256), rows_per_mod=seq, final=final)
        xp = res[0]
        if final:
            y_p = res[1]
        proj3 = proj.reshape(n_b, seq, MAIN_W)
        outs["kp"].append(k_out.reshape(n_b, seq, N_HEADS, HEAD_DIM))
        outs["vp"].append(v_out.reshape(n_b, seq, N_HEADS, HEAD_DIM))
        outs["gp"].append(gdn_new)
        outs["cp"].append(proj3[:, seq - (CONV_W - 1):, :3 * BRANCH_W])
        outs["rp"].append(ret_new)

        mod_s = [jnp.repeat(t[n_b:n_c], dec_seq, axis=0) for t in (scale, shift, gate)]
        proj_s, ba_s = _inproj_call(xs, mod_s[0], mod_s[1], norm_in[l], w_main[l], w_ba[l],
                                    tm=n_db * dec_seq, rows_per_mod=dec_seq)
        proj_sp = _pad_rows(proj_s, n_db, dec_seq, t_pad)
        ba_sp = _pad_rows(ba_s, n_db, dec_seq, t_pad)
        hist = jnp.pad(state_conv[l], ((0, 0), (SUBLANES - (CONV_W - 1), 0), (0, 0)))
        q_rot_s, k_rot_s = _rope_b_call(proj_sp, tab_b_s, tr=t_pad, t_rows=t_pad, cache_layout=False)
        y_a_s, gdn_new_s = _gdn_call(proj_sp, ba_sp, hist, state_gdn[l], conv_w[l], alog, dtb, na,
                                     n_b=n_db, t_rows=t_pad, C=t_pad, n_valid=dec_seq, n_sub=1)
        y_c_s, ret_new_s = _ret_call(proj_sp, tab_c_s[0], tab_c_s[1], state_ret[l], nc,
                                     n_b=n_db, t_rows=t_pad, C=t_pad, n_valid=dec_seq, n_sub=1)
        y_b_s = _moba_s_call(page_table, q_rot_s, k_rot_s, proj_sp, cache_k, cache_v, l, n_valid=dec_seq)
        unpad = lambda y: y.reshape(n_db, t_pad, BRANCH_W)[:, :dec_seq].reshape(n_db * dec_seq, BRANCH_W)
        res_s = _merge_call(unpad(y_a_s), unpad(y_b_s), unpad(y_c_s), proj_s, xs, mod_s[2],
                            w_branch_b[l], w_out_b[l], norm_f, tm=n_db * dec_seq, rows_per_mod=dec_seq, final=final)
        xs = res_s[0]
        if final:
            y_s = res_s[1]
        proj_s3 = proj_s.reshape(n_db, dec_seq, MAIN_W)
        outs["ks"].append(k_rot_s.reshape(n_db, t_pad, N_HEADS, HEAD_DIM)[:, :dec_seq])
        outs["vs"].append(proj_s3[:, :, COL_V_B * BRANCH_W:(COL_V_B + 1) * BRANCH_W]
                          .reshape(n_db, dec_seq, N_HEADS, HEAD_DIM))
        outs["gs"].append(gdn_new_s)
        outs["cs"].append(proj_s3[:, dec_seq - (CONV_W - 1):, :3 * BRANCH_W])
        outs["rs"].append(ret_new_s)

    st = {k: jnp.stack(v) for k, v in outs.items()}
    return (y_p.reshape(n_b, seq, d), y_s.reshape(n_db, dec_seq, d),
            st["kp"], st["vp"], st["ks"], st["vs"], st["gp"], st["gs"],
            st["cp"], st["cs"], st["rp"], st["rs"])
```

```python
import functools
import math

import jax
import jax.numpy as jnp
from jax import lax
from jax.experimental import pallas as pl
from jax.experimental.pallas import tpu as pltpu

F32 = jnp.float32
BF16 = jnp.bfloat16
HI = lax.Precision.HIGHEST

HEAD_DIM = 128
N_HEADS = 4
BRANCH_W = N_HEADS * HEAD_DIM
N_BRANCH = 3
CONV_W = 4
GDN_CHUNK = 64
RET_CHUNK = 64
MOBA_BLOCK = 256
MOBA_TOPK = 3
ROPE_THETA = 500000.0
ROPE_DIMS = HEAD_DIM // 4
RET_THETA = 10000.0
EPS = 1e-6
NEG = -1e30
SUBLANES = 8
QK_SCALE = HEAD_DIM ** -0.5
LOG2_E = math.log2(math.e)

COL_QKV_A, COL_Z_A, COL_Q_B, COL_K_B, COL_V_B, COL_Z_B = 0, 3, 4, 5, 6, 7
COL_Q_C, COL_K_C, COL_V_C, COL_Z_C, COL_MERGE = 8, 9, 10, 11, 12
MAIN_W = 18 * BRANCH_W
BA_W = 128

NT = (((1,), (1,)), ((), ()))
TN = (((0,), (0,)), ((), ()))

VMEM_LIMIT = 48 * 1024 * 1024


def _params(*sem):
    return pltpu.CompilerParams(dimension_semantics=sem, vmem_limit_bytes=VMEM_LIMIT)


def _silu(x):
    return x * jax.nn.sigmoid(x)


def _bdot(a, b):
    return jnp.dot(a.astype(BF16), b.astype(BF16), preferred_element_type=F32)


def _bdot_g(a, b, dims):
    return lax.dot_general(a.astype(BF16), b.astype(BF16), dims, preferred_element_type=F32)


def _hdot(a, b):
    return jnp.dot(a, b, precision=HI, preferred_element_type=F32)


def _mod_kernel(c_ref, w_ref, b_ref, o_ref):
    o_ref[...] = _bdot(_silu(c_ref[...]), w_ref[...]) + b_ref[...]


def _mod_call(c_all, w_ada, b_ada):
    depth, d, d3 = w_ada.shape
    rows = c_all.shape[0]
    tn = d
    return pl.pallas_call(
        _mod_kernel,
        grid=(depth, d3 // tn),
        in_specs=[pl.BlockSpec((rows, d), lambda l, j: (0, 0)),
                  pl.BlockSpec((None, d, tn), lambda l, j: (l, 0, j)),
                  pl.BlockSpec((None, 1, tn), lambda l, j: (l, 0, j))],
        out_specs=pl.BlockSpec((None, rows, tn), lambda l, j: (l, 0, j)),
        out_shape=jax.ShapeDtypeStruct((depth, rows, d3), F32),
        compiler_params=_params("arbitrary", "arbitrary"),
        name="mod",
    )(c_all, w_ada, b_ada.reshape(depth, 1, d3))


def _inproj_kernel(x_ref, sc_ref, sh_ref, g_ref, w_ref, wba_ref, o_ref, ba_ref, h_scr):
    @pl.when(pl.program_id(1) == 0)
    def _():
        x = x_ref[...]
        y = x * lax.rsqrt(jnp.mean(x * x, axis=-1, keepdims=True) + EPS) * g_ref[...]
        h = (y * (1.0 + sc_ref[...]) + sh_ref[...]).astype(BF16)
        h_scr[...] = h
        ba_ref[...] = jnp.dot(h, wba_ref[...], preferred_element_type=F32)

    o_ref[...] = jnp.dot(h_scr[...], w_ref[...], preferred_element_type=F32)


def _inproj_call(x2d, scale, shift, norm_g, w_main, w_ba, *, tm, rows_per_mod):
    m, d = x2d.shape
    tn = 1536
    assert m % tm == 0 and MAIN_W % tn == 0
    if scale.ndim == 3:
        assert rows_per_mod % tm == 0
        mod_spec = pl.BlockSpec((None, 1, d), lambda i, j: ((i * tm) // rows_per_mod, 0, 0))
    else:
        mod_spec = pl.BlockSpec((tm, d), lambda i, j: (i, 0))
    return pl.pallas_call(
        _inproj_kernel,
        grid=(m // tm, MAIN_W // tn),
        in_specs=[pl.BlockSpec((tm, d), lambda i, j: (i, 0)),
                  mod_spec, mod_spec,
                  pl.BlockSpec((1, d), lambda i, j: (0, 0)),
                  pl.BlockSpec((d, tn), lambda i, j: (0, j)),
                  pl.BlockSpec((d, BA_W), lambda i, j: (0, 0))],
        out_specs=[pl.BlockSpec((tm, tn), lambda i, j: (i, j)),
                   pl.BlockSpec((tm, BA_W), lambda i, j: (i, 0))],
        out_shape=[jax.ShapeDtypeStruct((m, MAIN_W), F32),
                   jax.ShapeDtypeStruct((m, BA_W), F32)],
        scratch_shapes=[pltpu.VMEM((tm, d), BF16)],
        compiler_params=_params("arbitrary", "arbitrary"),
        name="in_proj",
    )(x2d, scale, shift, norm_g.reshape(1, d), w_main, w_ba)


def _rope_b_kernel(q_ref, k_ref, v_ref, c_ref, s1_ref, s2_ref, qo_ref, ko_ref, *cache_refs):
    c, s1, s2 = c_ref[...], s1_ref[...], s2_ref[...]
    half = ROPE_DIMS // 2
    tr = q_ref.shape[0]
    for h in range(N_HEADS):
        sl = slice(h * HEAD_DIM, (h + 1) * HEAD_DIM)
        q = q_ref[:, sl]
        k = k_ref[:, sl]
        qr = q * c + pltpu.roll(q, half, 1) * s1 + pltpu.roll(q, HEAD_DIM - half, 1) * s2
        kr = k * c + pltpu.roll(k, half, 1) * s1 + pltpu.roll(k, HEAD_DIM - half, 1) * s2
        qo_ref[:, sl] = qr * QK_SCALE
        ko_ref[:, sl] = kr
        if cache_refs:
            kc_ref, vc_ref = cache_refs
            kc_ref[pl.ds(h, tr, stride=N_HEADS), :] = kr
            vc_ref[pl.ds(h, tr, stride=N_HEADS), :] = v_ref[:, sl]


def _rope_b_call(proj, tabs, *, tr, t_rows, cache_layout):
    m = proj.shape[0]
    nt = t_rows // tr
    assert t_rows % tr == 0 and m % t_rows == 0
    tab_spec = pl.BlockSpec((tr, HEAD_DIM), lambda i: (i % nt, 0))
    col_spec = lambda col: pl.BlockSpec((tr, BRANCH_W), lambda i: (i, col))
    out_specs = [col_spec(0), col_spec(0)]
    out_shape = [jax.ShapeDtypeStruct((m, BRANCH_W), F32)] * 2
    if cache_layout:
        out_specs += [pl.BlockSpec((tr * N_HEADS, HEAD_DIM), lambda i: (i, 0))] * 2
        out_shape += [jax.ShapeDtypeStruct((m * N_HEADS, HEAD_DIM), F32)] * 2
    return pl.pallas_call(
        _rope_b_kernel,
        grid=(m // tr,),
        in_specs=[col_spec(COL_Q_B), col_spec(COL_K_B), col_spec(COL_V_B), tab_spec, tab_spec, tab_spec],
        out_specs=out_specs,
        out_shape=out_shape,
        compiler_params=_params("arbitrary"),
        name="rope_b",
    )(proj, proj, proj, *tabs)


INV_BASE = 8
GDN_CHUNKS_PER_STEP = 4


def _bdot_each(xs, ys):
    return [_bdot(x, y) for x, y in zip(xs, ys)]


def _unit_lower_inverses(mats, ri, ci, n):
    base = min(INV_BASE, n)
    eye = (ri == ci).astype(F32)
    dpows = [jnp.where((ri // base) == (ci // base), a, 0.0) for a in mats]
    invs = [eye + d for d in dpows]
    for _ in range(int(math.log2(base)) - 1):
        dpows = _bdot_each(dpows, dpows)
        invs = [inv + t for inv, t in zip(invs, _bdot_each(invs, dpows))]
    size = base
    while size < n:
        off = ((ri // (2 * size)) == (ci // (2 * size))) & ((ri // size) != (ci // size))
        es = [jnp.where(off, a, 0.0) for a in mats]
        invs = [inv + t for inv, t in zip(invs, _bdot_each(invs, _bdot_each(es, invs)))]
        size *= 2
    return invs


def _gdn_kernel(qkv_ref, hist_ref, ba_ref, z_ref, s0_ref, cw_ref, alog_ref, dtb_ref, na_ref,
                y_ref, sout_ref, s_scr, tail_scr, *, C, n_valid, n_sub):
    c_idx = pl.program_id(1)
    rows = n_sub * C

    @pl.when(c_idx == 0)
    def _():
        s_scr[...] = s0_ref[...]
        tail_scr[...] = hist_ref[...]

    u = qkv_ref[...]
    prev = tail_scr[...]
    w = cw_ref[...]
    acc = u * w[CONV_W - 1:CONV_W, :]
    row8 = lax.broadcasted_iota(jnp.int32, (SUBLANES, u.shape[1]), 0)
    for s in range(1, CONV_W):
        rolled = pltpu.roll(u, s, 0)
        first = jnp.where(row8 < s, pltpu.roll(prev, s, 0), rolled[:SUBLANES])
        shifted = first if rows == SUBLANES else jnp.concatenate([first, rolled[SUBLANES:]], axis=0)
        acc = acc + shifted * w[CONV_W - 1 - s:CONV_W - s, :]
    tail_scr[...] = u[rows - SUBLANES:, :]
    conv = _silu(acc)

    ba = ba_ref[...]
    beta_t = jax.nn.sigmoid(ba)
    xg = ba + dtb_ref[...]
    g_t = -jnp.exp(alog_ref[...]) * (jnp.maximum(xg, 0.0) + jnp.log1p(jnp.exp(-jnp.abs(xg))))
    row_t = lax.broadcasted_iota(jnp.int32, ba.shape, 0) % C
    if n_valid < C:
        beta_t = jnp.where(row_t < n_valid, beta_t, 0.0)
        g_t = jnp.where(row_t < n_valid, g_t, 0.0)
    gc_t = g_t
    shift = 1
    while shift < C:
        gc_t = gc_t + jnp.where(row_t >= shift, pltpu.roll(gc_t, shift, 0), 0.0)
        shift *= 2

    cs = N_HEADS * C
    subs = range(n_sub)
    stack = lambda f: jnp.concatenate([f(h) for h in range(N_HEADS)], axis=0)
    ri = lax.broadcasted_iota(jnp.int32, (cs, cs), 0)
    ci = lax.broadcasted_iota(jnp.int32, (cs, cs), 1)
    same_head = (ri // C) == (ci // C)
    tri = same_head & (ri >= ci)
    strict = same_head & (ri > ci)

    qn, kn, kb, vb, gc, g_last, decay = [], [], [], [], [], [], []
    for j in subs:
        r0 = j * C
        lanes = lambda t, col: jnp.broadcast_to(t[r0:r0 + C, col:col + 1], (C, HEAD_DIM))
        head_cols = lambda base: stack(lambda h: conv[r0:r0 + C, base + h * HEAD_DIM:base + (h + 1) * HEAD_DIM])
        q, k, v = head_cols(0), head_cols(BRANCH_W), head_cols(2 * BRANCH_W)
        beta = stack(lambda h: lanes(beta_t, h))
        gc_j = stack(lambda h: lanes(gc_t, N_HEADS + h))
        if cs % HEAD_DIM == 0:
            gc_row = jnp.concatenate([gc_j.T] * (cs // HEAD_DIM), axis=0)
            gc_col = jnp.concatenate([gc_j] * (cs // HEAD_DIM), axis=1)
        else:
            g_st = stack(lambda h: lanes(g_t, N_HEADS + h))
            upper = (same_head & (ri <= ci)).astype(F32)
            gc_row = _hdot(jnp.ones((cs, cs), F32), g_st[:, :cs] * upper)
            gc_col = gc_j[:, :cs]
        kn_j = k * lax.rsqrt(jnp.sum(k * k, axis=-1, keepdims=True) + EPS)
        qn.append(q * lax.rsqrt(jnp.sum(q * q, axis=-1, keepdims=True) + EPS) * QK_SCALE)
        kn.append(kn_j)
        kb.append(kn_j * beta)
        vb.append(v * beta)
        gc.append(gc_j)
        g_last.append(stack(lambda h: jnp.broadcast_to(
            gc_t[r0 + C - 1:r0 + C, N_HEADS + h:N_HEADS + h + 1], (C, HEAD_DIM))))
        decay.append(jnp.where(tri, jnp.exp(jnp.where(tri, gc_col - gc_row, 0.0)), 0.0))

    kk = [_bdot_g(kb[j], kn[j], NT) for j in subs]
    qk = [_bdot_g(qn[j], kn[j], NT) for j in subs]
    invs = _unit_lower_inverses([-jnp.where(strict, kk[j] * decay[j], 0.0) for j in subs], ri, ci, C)
    eg = [jnp.exp(gc[j]) for j in subs]
    sols = _bdot_each(invs, [jnp.concatenate([vb[j], kb[j] * eg[j]], axis=1) for j in subs])
    attn = [qk[j] * decay[j] for j in subs]
    q_dec = [qn[j] * eg[j] for j in subs]
    k_dec = [kn[j] * jnp.exp(g_last[j] - gc[j]) for j in subs]

    hrows = [slice(h * C, (h + 1) * C) for h in range(N_HEADS)]
    states = [s_scr[h] for h in range(N_HEADS)]
    for j in subs:
        u_s, w_s = sols[j][:, :HEAD_DIM], sols[j][:, HEAD_DIM:]
        v_new = jnp.concatenate([u_s[hr] - _bdot(w_s[hr], st) for hr, st in zip(hrows, states)], axis=0)
        o = (jnp.concatenate([_bdot(q_dec[j][hr], st) for hr, st in zip(hrows, states)], axis=0)
             + _bdot(attn[j], v_new))
        states = [st * jnp.exp(g_last[j][h * C:h * C + 1, :]) + _bdot_g(k_dec[j][hr], v_new[hr], TN)
                  for h, (hr, st) in enumerate(zip(hrows, states))]
        on = o * lax.rsqrt(jnp.mean(o * o, axis=-1, keepdims=True) + EPS) * na_ref[...]
        for h, hr in enumerate(hrows):
            sl = slice(h * HEAD_DIM, (h + 1) * HEAD_DIM)
            y_ref[j * C:(j + 1) * C, sl] = (on[hr] * _silu(z_ref[j * C:(j + 1) * C, sl])).astype(y_ref.dtype)
    for h in range(N_HEADS):
        s_scr[h] = states[h]

    @pl.when(c_idx == pl.num_programs(1) - 1)
    def _():
        sout_ref[...] = s_scr[...]


def _gdn_call(proj, ba, hist, s0, conv_w, alog, dtb, norm_a, *, n_b, t_rows, C, n_valid, n_sub, y_dtype):
    tile = n_sub * C
    n_c = t_rows // tile
    assert t_rows % tile == 0
    row = lambda b, c: b * n_c + c
    vec_spec = pl.BlockSpec((1, HEAD_DIM), lambda b, c: (0, 0))
    state_spec = pl.BlockSpec((None, N_HEADS, HEAD_DIM, HEAD_DIM), lambda b, c: (b, 0, 0, 0))
    return pl.pallas_call(
        functools.partial(_gdn_kernel, C=C, n_valid=n_valid, n_sub=n_sub),
        grid=(n_b, n_c),
        in_specs=[pl.BlockSpec((tile, 3 * BRANCH_W), lambda b, c: (row(b, c), 0)),
                  pl.BlockSpec((None, SUBLANES, 3 * BRANCH_W), lambda b, c: (b, 0, 0)),
                  pl.BlockSpec((tile, BA_W), lambda b, c: (row(b, c), 0)),
                  pl.BlockSpec((tile, BRANCH_W), lambda b, c: (row(b, c), COL_Z_A)),
                  state_spec,
                  pl.BlockSpec((CONV_W, 3 * BRANCH_W), lambda b, c: (0, 0)),
                  vec_spec, vec_spec, vec_spec],
        out_specs=[pl.BlockSpec((tile, BRANCH_W), lambda b, c: (row(b, c), 0)), state_spec],
        out_shape=[jax.ShapeDtypeStruct((n_b * t_rows, BRANCH_W), y_dtype),
                   jax.ShapeDtypeStruct((n_b, N_HEADS, HEAD_DIM, HEAD_DIM), F32)],
        scratch_shapes=[pltpu.VMEM((N_HEADS, HEAD_DIM, HEAD_DIM), F32),
                        pltpu.VMEM((SUBLANES, 3 * BRANCH_W), F32)],
        compiler_params=_params("arbitrary", "arbitrary"),
        name="gdn",
    )(proj, hist, ba, proj, s0, conv_w, alog, dtb, norm_a)


RET_CHUNKS_PER_STEP = 4

def _ret_kernel(q_ref, k_ref, v_ref, z_ref, cos_ref, sin_ref, s0_ref, dmat_ref, qdec_ref, kdec_ref,
                cdec_ref, nc_ref, y_ref, sout_ref, s_scr, *, C, n_sub):
    c_idx = pl.program_id(1)

    @pl.when(c_idx == 0)
    def _():
        s_scr[...] = s0_ref[...]

    cos, sin = cos_ref[...], sin_ref[...]
    heads = [slice(h * HEAD_DIM, (h + 1) * HEAD_DIM) for h in range(N_HEADS)]
    units = [(j, h) for j in range(n_sub) for h in range(N_HEADS)]
    rows = lambda j: slice(j * C, (j + 1) * C)
    rope = lambda x: x * cos + pltpu.roll(x, HEAD_DIM // 2, 1) * sin
    qr = [rope(q_ref[:, sl]) for sl in heads]
    kr = [rope(k_ref[:, sl]) * QK_SCALE for sl in heads]
    vs = [v_ref[:, sl] for sl in heads]
    sc = [_bdot_g(qr[h][rows(j)], kr[h][rows(j)], NT) * dmat_ref[h] for j, h in units]
    o_intra = [_bdot(s, vs[h][rows(j)]) for s, (j, h) in zip(sc, units)]
    kv = [_bdot_g(kr[h][rows(j)] * kdec_ref[h], vs[h][rows(j)], TN) for j, h in units]

    states = [s_scr[h] for h in range(N_HEADS)]
    for j in range(n_sub):
        for h, sl in enumerate(heads):
            u = j * N_HEADS + h
            o = o_intra[u] + _bdot(qr[h][rows(j)] * qdec_ref[h], states[h])
            states[h] = states[h] * cdec_ref[h] + kv[u]
            on = o * lax.rsqrt(jnp.mean(o * o, axis=-1, keepdims=True) + EPS) * nc_ref[...]
            y_ref[rows(j), sl] = (on * _silu(z_ref[rows(j), sl])).astype(y_ref.dtype)
    for h in range(N_HEADS):
        s_scr[h] = states[h]

    @pl.when(c_idx == pl.num_programs(1) - 1)
    def _():
        sout_ref[...] = s_scr[...]


def _ret_tables(C, n_valid):
    log_g = jnp.log1p(-(2.0 ** (-5.0 - jnp.arange(N_HEADS, dtype=F32))))
    idx = jnp.arange(C, dtype=F32)
    tri = jnp.tril(jnp.ones((C, C), dtype=bool))
    dmat = jnp.where(tri, jnp.exp(log_g[:, None, None] * jnp.where(tri, idx[:, None] - idx[None, :], 0.0)), 0.0)
    q_dec = jnp.exp(log_g[:, None] * (idx + 1.0))[..., None]
    k_dec = jnp.exp(log_g[:, None] * (n_valid - 1.0 - idx))[..., None]
    c_dec = jnp.exp(log_g * n_valid)[:, None, None]
    bc = lambda t, r: jnp.broadcast_to(t, (N_HEADS, r, HEAD_DIM))
    return dmat, bc(q_dec, C), bc(k_dec, C), bc(c_dec, 1)


def _ret_call(proj, cos2, sin2, s0, norm_c, *, n_b, t_rows, C, n_valid, n_sub, y_dtype):
    tile = n_sub * C
    n_c = t_rows // tile
    assert t_rows % tile == 0
    row = lambda b, c: b * n_c + c
    dmat, q_dec, k_dec, c_dec = _ret_tables(C, n_valid)
    col_spec = lambda col: pl.BlockSpec((tile, BRANCH_W), lambda b, c: (row(b, c), col))
    tab_spec = pl.BlockSpec((tile, HEAD_DIM), lambda b, c: (c, 0))
    state_spec = pl.BlockSpec((None, N_HEADS, HEAD_DIM, HEAD_DIM), lambda b, c: (b, 0, 0, 0))
    const_spec = lambda r, w: pl.BlockSpec((N_HEADS, r, w), lambda b, c: (0, 0, 0))
    return pl.pallas_call(
        functools.partial(_ret_kernel, C=C, n_sub=n_sub),
        grid=(n_b, n_c),
        in_specs=[col_spec(COL_Q_C), col_spec(COL_K_C), col_spec(COL_V_C), col_spec(COL_Z_C),
                  tab_spec, tab_spec, state_spec,
                  const_spec(C, C), const_spec(C, HEAD_DIM), const_spec(C, HEAD_DIM), const_spec(1, HEAD_DIM),
                  pl.BlockSpec((1, HEAD_DIM), lambda b, c: (0, 0))],
        out_specs=[pl.BlockSpec((tile, BRANCH_W), lambda b, c: (row(b, c), 0)), state_spec],
        out_shape=[jax.ShapeDtypeStruct((n_b * t_rows, BRANCH_W), y_dtype),
                   jax.ShapeDtypeStruct((n_b, N_HEADS, HEAD_DIM, HEAD_DIM), F32)],
        scratch_shapes=[pltpu.VMEM((N_HEADS, HEAD_DIM, HEAD_DIM), F32)],
        compiler_params=_params("arbitrary", "arbitrary"),
        name="ret",
    )(proj, proj, proj, proj, cos2, sin2, s0, dmat, q_dec, k_dec, c_dec, norm_c)


def _topk_mask(gate, valid, axis):
    idx = lax.broadcasted_iota(jnp.int32, gate.shape, axis)
    gm = jnp.where(valid, gate, -jnp.inf)
    rank = jnp.zeros(gate.shape, jnp.int32)
    for m in range(gate.shape[axis]):
        gmm = gm[m:m + 1, :] if axis == 0 else gm[:, m:m + 1]
        rank = rank + jnp.where(gmm > gm, 1, jnp.where(gmm == gm, jnp.where(idx > m, 1, 0), 0))
    return valid & (rank < MOBA_TOPK)


def _moba_p_kernel(q_ref, k_ref, v_ref, z_ref, y_ref, kmean_scr, kb_scr, vt_scr, bias_scr, acc_scr, *, n_blk):
    qi = pl.program_id(1)
    blk = MOBA_BLOCK
    heads = [slice(h * HEAD_DIM, (h + 1) * HEAD_DIM) for h in range(N_HEADS)]

    @pl.when(qi == 0)
    def _():
        def prep(n, carry):
            rows = pl.ds(pl.multiple_of(n * blk, blk), blk)
            for h, sl in enumerate(heads):
                kn = k_ref[rows, sl]
                kmean_scr[h, pl.ds(n, 1), :] = jnp.sum(kn, axis=0, keepdims=True) * (1.0 / blk)
                kb_scr[h * n_blk + n] = kn.astype(BF16)
                vt_scr[h * n_blk + n] = v_ref[rows, sl].T.astype(BF16)
            return carry

        lax.fori_loop(0, n_blk, prep, 0)

    key_i = lax.broadcasted_iota(jnp.int32, (blk, blk), 0)
    qry_i = lax.broadcasted_iota(jnp.int32, (blk, blk), 1)
    blk_id = lax.broadcasted_iota(jnp.int32, (n_blk, blk), 0)
    qs = [q_ref[:, sl] for sl in heads]
    qbs = [(q * LOG2_E).astype(BF16) for q in qs]
    ss = [jnp.where(key_i <= qry_i,
                    lax.dot_general(kb_scr[h * n_blk + qi], qbs[h], NT, preferred_element_type=F32), NEG)
          for h in range(N_HEADS)]
    ms = [jnp.max(s, axis=0, keepdims=True) for s in ss]
    ps = [jnp.exp2(ss[h] - ms[h]) for h in range(N_HEADS)]
    ls = [jnp.sum(p, axis=0, keepdims=True) for p in ps]
    for h in range(N_HEADS):
        acc_scr[h] = jnp.dot(vt_scr[h * n_blk + qi], ps[h].astype(BF16), preferred_element_type=F32)
    for h in range(N_HEADS):
        gate = lax.dot_general(kmean_scr[h], qs[h], NT, precision=HI, preferred_element_type=F32)
        bias_scr[h] = jnp.where(_topk_mask(gate, blk_id < qi, 0), 0.0, NEG)

    hs = range(N_HEADS)
    HEAD_GROUPS = (tuple(hs),)

    def body(i, carry):
        ms, ls = carry
        ns = (2 * i, 2 * i + 1)
        ms_new, ls_new = list(ms), list(ls)
        for grp in HEAD_GROUPS:
            ss = {h: [lax.dot_general(kb_scr[h * n_blk + n], qbs[h], NT, preferred_element_type=F32)
                      + bias_scr[h, pl.ds(n, 1), :] for n in ns] for h in grp}
            for h in grp:
                ms_new[h] = jnp.maximum(ms[h], jnp.max(jnp.maximum(ss[h][0], ss[h][1]), axis=0, keepdims=True))
            ps = {h: [jnp.exp2(s - ms_new[h]) for s in ss[h]] for h in grp}
            alphas = {h: jnp.exp2(ms[h] - ms_new[h]) for h in grp}
            pvs = {h: [jnp.dot(vt_scr[h * n_blk + n], p.astype(BF16), preferred_element_type=F32)
                       for n, p in zip(ns, ps[h])] for h in grp}
            for h in grp:
                ls_new[h] = alphas[h] * ls[h] + jnp.sum(ps[h][0] + ps[h][1], axis=0, keepdims=True)
                acc_scr[h] = alphas[h] * acc_scr[h] + (pvs[h][0] + pvs[h][1])
        return tuple(ms_new), tuple(ls_new)

    _, ls = lax.fori_loop(0, (qi + 1) // 2, body, (tuple(ms), tuple(ls)))
    for h, sl in enumerate(heads):
        y_ref[:, sl] = ((acc_scr[h] / ls[h]).T * _silu(z_ref[:, sl])).astype(y_ref.dtype)


def _moba_p_call(q_rot, k_rot, proj, *, n_b, t_rows):
    n_blk = t_rows // MOBA_BLOCK
    assert t_rows % MOBA_BLOCK == 0 and n_blk >= MOBA_TOPK
    blk = MOBA_BLOCK
    seq_spec = lambda col: pl.BlockSpec((t_rows, BRANCH_W), lambda b, i: (b, col), pipeline_mode=pl.Buffered(1))
    tile_spec = lambda col: pl.BlockSpec((blk, BRANCH_W), lambda b, i: (b * n_blk + i, col))
    return pl.pallas_call(
        functools.partial(_moba_p_kernel, n_blk=n_blk),
        grid=(n_b, n_blk),
        in_specs=[tile_spec(0), seq_spec(0), seq_spec(COL_V_B), tile_spec(COL_Z_B)],
        out_specs=tile_spec(0),
        out_shape=jax.ShapeDtypeStruct((n_b * t_rows, BRANCH_W), BF16),
        scratch_shapes=[pltpu.VMEM((N_HEADS, n_blk, HEAD_DIM), F32),
                        pltpu.VMEM((N_HEADS * n_blk, blk, HEAD_DIM), BF16),
                        pltpu.VMEM((N_HEADS * n_blk, HEAD_DIM, blk), BF16),
                        pltpu.VMEM((N_HEADS, n_blk, blk), F32),
                        pltpu.VMEM((N_HEADS, HEAD_DIM, blk), F32)],
        compiler_params=_params("arbitrary", "arbitrary"),
        name="moba_p",
    )(q_rot, k_rot, proj, proj)


PAGES_PER_STEP = 16


def _moba_s_kernel(pt_ref, q_ref, kn_ref, vn_ref, z_ref, *refs, n_valid, pages_per_blk, pps):
    del pt_ref
    k_refs, v_refs = refs[:pps], refs[pps:2 * pps]
    y_ref, q_scr, ksum_scr, m_scr, l_scr, o_scr = refs[2 * pps:]
    step = pl.program_id(1)
    rows = N_HEADS * SUBLANES
    page_rows = k_refs[0].shape[0]
    heads = [slice(h * HEAD_DIM, (h + 1) * HEAD_DIM) for h in range(N_HEADS)]
    hrows = [slice(h * SUBLANES, (h + 1) * SUBLANES) for h in range(N_HEADS)]

    @pl.when(step == 0)
    def _():
        q = q_ref[...]
        q_scr[...] = jnp.concatenate([q[:, sl] for sl in heads], axis=0)

    q32 = q_scr[...]
    qb = q32.astype(BF16)
    row_head = lax.broadcasted_iota(jnp.int32, (rows, page_rows), 0) // SUBLANES
    col_head = lax.broadcasted_iota(jnp.int32, (rows, page_rows), 1) % N_HEADS
    head_bias = jnp.where(row_head == col_head, 0.0, NEG)

    blks_per_step = pps // pages_per_blk
    kps = [k_refs[j][...] for j in range(pps)]
    ss = [_bdot_g(qb, kp, NT) + head_bias for kp in kps]
    ms = [jnp.max(s, axis=1, keepdims=True) for s in ss]
    es = [jnp.exp(s - m) for s, m in zip(ss, ms)]
    os_ = [_bdot(e, v_refs[j][...]) for j, e in enumerate(es)]
    for j in range(pps):
        pg = step * pps + j
        m_scr[pg] = jnp.broadcast_to(ms[j], (rows, HEAD_DIM))
        l_scr[pg] = jnp.broadcast_to(jnp.sum(es[j], axis=1, keepdims=True), (rows, HEAD_DIM))
        o_scr[pg] = os_[j]
    parts = [jnp.sum(kp.reshape(page_rows // SUBLANES, SUBLANES, HEAD_DIM), axis=0) for kp in kps]
    for bl in range(blks_per_step):
        ksum = parts[bl * pages_per_blk]
        for j in range(bl * pages_per_blk + 1, (bl + 1) * pages_per_blk):
            ksum = ksum + parts[j]
        ksum_scr[step * blks_per_step + bl] = ksum

    @pl.when(step == pl.num_programs(1) - 1)
    def _():
        n_pg = m_scr.shape[0]
        n_blk = ksum_scr.shape[0]
        kflat = ksum_scr[...].reshape(n_blk * SUBLANES, HEAD_DIM) * (1.0 / MOBA_BLOCK)
        g_all = lax.dot_general(q32, kflat, NT, precision=HI, preferred_element_type=F32)
        rh = lax.broadcasted_iota(jnp.int32, g_all.shape, 0) // SUBLANES
        ch = lax.broadcasted_iota(jnp.int32, g_all.shape, 1) % N_HEADS
        pool = (lax.broadcasted_iota(jnp.int32, (n_blk * SUBLANES, n_blk), 0) // SUBLANES
                == lax.broadcasted_iota(jnp.int32, (n_blk * SUBLANES, n_blk), 1)).astype(F32)
        gate = _hdot(jnp.where(rh == ch, g_all, 0.0), pool)
        sel_f = jnp.where(_topk_mask(gate, jnp.ones(gate.shape, jnp.bool_), 1), 1.0, 0.0)
        selw = [jnp.broadcast_to(sel_f[:, n:n + 1], (rows, HEAD_DIM)) > 0.5 for n in range(n_blk)]

        kn, vn = kn_ref[...], vn_ref[...]
        s_own = jnp.concatenate([_bdot_g(q32[hr], kn[:, sl], NT) for hr, sl in zip(hrows, heads)], axis=0)
        rq = lax.broadcasted_iota(jnp.int32, s_own.shape, 0) % SUBLANES
        cj = lax.broadcasted_iota(jnp.int32, s_own.shape, 1)
        own_ok = (cj <= rq) & (cj < n_valid)
        s_own = jnp.where(own_ok, s_own, NEG)
        mx = jnp.broadcast_to(jnp.max(s_own, axis=1, keepdims=True), (rows, HEAD_DIM))
        for pg in range(n_pg):
            mx = jnp.maximum(mx, jnp.where(selw[pg // pages_per_blk], m_scr[pg], NEG))
        e_own = jnp.where(own_ok, jnp.exp(s_own - mx[:, :SUBLANES]), 0.0)
        l_tot = jnp.broadcast_to(jnp.sum(e_own, axis=1, keepdims=True), (rows, HEAD_DIM))
        o_tot = jnp.concatenate([_bdot(e_own[hr], vn[:, sl]) for hr, sl in zip(hrows, heads)], axis=0)
        for pg in range(n_pg):
            wgt = jnp.where(selw[pg // pages_per_blk], jnp.exp(jnp.minimum(m_scr[pg] - mx, 0.0)), 0.0)
            l_tot = l_tot + wgt * l_scr[pg]
            o_tot = o_tot + wgt * o_scr[pg]
        o = o_tot / l_tot
        y_ref[...] = jnp.concatenate([o[hr] for hr in hrows], axis=1) * _silu(z_ref[...])


def _moba_s_call(page_table, q_rot, k_rot, proj, cache_k, cache_v, layer, *, n_valid):
    n_b, n_pages = page_table.shape
    depth, n_pool, page = cache_k.shape[:3]
    assert MOBA_BLOCK % page == 0
    pages_per_blk = MOBA_BLOCK // page
    pps = PAGES_PER_STEP
    assert pps % pages_per_blk == 0 and n_pages % pps == 0 and n_pages // pages_per_blk >= MOBA_TOPK
    assert (page * N_HEADS) % SUBLANES == 0 and SUBLANES % N_HEADS == 0
    ck = cache_k.reshape(depth, n_pool, page * N_HEADS, HEAD_DIM)
    cv = cache_v.reshape(depth, n_pool, page * N_HEADS, HEAD_DIM)
    rows = N_HEADS * SUBLANES
    row_spec = lambda col: pl.BlockSpec((SUBLANES, BRANCH_W), lambda b, p, pt: (b, col))

    def page_spec(j):
        return pl.BlockSpec((None, None, page * N_HEADS, HEAD_DIM),
                            lambda b, p, pt: (layer, pt[b, p * pps + j], 0, 0))

    page_specs = [page_spec(j) for j in range(pps)]
    grid_spec = pltpu.PrefetchScalarGridSpec(
        num_scalar_prefetch=1,
        grid=(n_b, n_pages // pps),
        in_specs=[row_spec(0), row_spec(0), row_spec(COL_V_B), row_spec(COL_Z_B)] + page_specs + page_specs,
        out_specs=pl.BlockSpec((SUBLANES, BRANCH_W), lambda b, p, pt: (b, 0)),
        scratch_shapes=[pltpu.VMEM((rows, HEAD_DIM), F32),
                        pltpu.VMEM((n_pages // pages_per_blk, SUBLANES, HEAD_DIM), F32),
                        pltpu.VMEM((n_pages, rows, HEAD_DIM), F32),
                        pltpu.VMEM((n_pages, rows, HEAD_DIM), F32),
                        pltpu.VMEM((n_pages, rows, HEAD_DIM), F32)])
    return pl.pallas_call(
        functools.partial(_moba_s_kernel, n_valid=n_valid, pages_per_blk=pages_per_blk, pps=pps),
        grid_spec=grid_spec,
        out_shape=jax.ShapeDtypeStruct((n_b * SUBLANES, BRANCH_W), F32),
        compiler_params=_params("arbitrary", "arbitrary"),
        name="moba_s",
    )(page_table, q_rot, k_rot, proj, proj, *([ck] * pps), *([cv] * pps))


def _merge_kernel(ya_ref, yb_ref, yc_ref, mg_ref, x_ref, gate_ref, wb_ref, wo_ref, nf_ref, *out_refs, final):
    d = x_ref.shape[1]
    mixed = None
    for n, y_ref in enumerate((ya_ref, yb_ref, yc_ref)):
        per_branch = _bdot(y_ref[...], wb_ref[n])
        term = jax.nn.sigmoid(mg_ref[:, n * d:(n + 1) * d]) * per_branch
        mixed = term if mixed is None else mixed + term
    x_out = x_ref[...] + gate_ref[...] * _bdot(mixed, wo_ref[...])
    out_refs[0][...] = x_out
    if final:
        out_refs[1][...] = (x_out * lax.rsqrt(jnp.mean(x_out * x_out, axis=-1, keepdims=True) + EPS)
                            * nf_ref[...])


def _merge_call(y_a, y_b, y_c, proj, x2d, gate, w_branch, w_out, norm_f, *, tm, rows_per_mod, final):
    m, d = x2d.shape
    assert m % tm == 0
    if gate.ndim == 3:
        assert rows_per_mod % tm == 0
        gate_spec = pl.BlockSpec((None, 1, d), lambda i: ((i * tm) // rows_per_mod, 0, 0))
    else:
        gate_spec = pl.BlockSpec((tm, d), lambda i: (i, 0))
    y_spec = pl.BlockSpec((tm, BRANCH_W), lambda i: (i, 0))
    x_spec = pl.BlockSpec((tm, d), lambda i: (i, 0))
    n_out = 2 if final else 1
    outs = pl.pallas_call(
        functools.partial(_merge_kernel, final=final),
        grid=(m // tm,),
        in_specs=[y_spec, y_spec, y_spec,
                  pl.BlockSpec((tm, N_BRANCH * d), lambda i: (i, (COL_MERGE * BRANCH_W) // (N_BRANCH * d))),
                  x_spec, gate_spec,
                  pl.BlockSpec((N_BRANCH, BRANCH_W, d), lambda i: (0, 0, 0)),
                  pl.BlockSpec((d, d), lambda i: (0, 0)),
                  pl.BlockSpec((1, d), lambda i: (0, 0))],
        out_specs=[x_spec] * n_out,
        out_shape=[jax.ShapeDtypeStruct((m, d), F32)] * n_out,
        compiler_params=_params("arbitrary"),
        name="merge",
    )(y_a, y_b, y_c, proj, x2d, gate, w_branch, w_out, norm_f.reshape(1, d))
    return outs


def _rope_tables(pos):
    t = pos.shape[0]
    posf = pos.astype(F32)

    def cos_sin(n_rot, theta):
        half = n_rot // 2
        inv = theta ** (-jnp.arange(half, dtype=F32) / half)
        ang = posf[:, None] * inv[None, :]
        return jnp.cos(ang), jnp.sin(ang)

    cb, sb = cos_sin(ROPE_DIMS, ROPE_THETA)
    hb = ROPE_DIMS // 2
    tab_b = (jnp.concatenate([cb, cb, jnp.ones((t, HEAD_DIM - ROPE_DIMS), F32)], axis=1),
             jnp.concatenate([jnp.zeros((t, hb), F32), sb, jnp.zeros((t, HEAD_DIM - ROPE_DIMS), F32)], axis=1),
             jnp.concatenate([-sb, jnp.zeros((t, HEAD_DIM - hb), F32)], axis=1))
    cc, sc = cos_sin(HEAD_DIM, RET_THETA)
    tab_c = (jnp.concatenate([cc, cc], axis=1), jnp.concatenate([-sc, sc], axis=1))
    return tab_b, tab_c


def _pad_rows(a2d, n_b, t, t_pad):
    w = a2d.shape[1]
    return jnp.pad(a2d.reshape(n_b, t, w), ((0, 0), (0, t_pad - t), (0, 0))).reshape(n_b * t_pad, w)


def _alpha_lanes(v):
    return jnp.pad(v.astype(F32), (N_HEADS, BA_W - 2 * N_HEADS)).reshape(1, BA_W)


def kernel(x_prompt, x_sample, cache_k, cache_v, state_gdn, state_conv, state_ret, page_table, c_prompt, c_sample,
           norm_in, w_ada, b_ada, w_in, conv_w, a_log, dt_bias, norm_a, norm_c, w_branch, w_out, norm_f):
    n_b, seq, d = x_prompt.shape
    n_db, dec_seq, _ = x_sample.shape
    depth = w_in.shape[0]
    n_pages = page_table.shape[1]
    past_len = n_pages * cache_k.shape[2]
    assert d == 2 * BRANCH_W and dec_seq <= SUBLANES and dec_seq >= CONV_W - 1
    assert seq % GDN_CHUNK == 0 and seq % RET_CHUNK == 0
    t_pad = SUBLANES

    ba0 = 4 * BRANCH_W
    w_main = jnp.concatenate([w_in[:, :, :ba0], w_in[:, :, ba0 + 2 * N_HEADS:]], axis=2).astype(BF16)
    w_ba = jnp.pad(w_in[:, :, ba0:ba0 + 2 * N_HEADS], ((0, 0), (0, 0), (0, BA_W - 2 * N_HEADS))).astype(BF16)
    w_ada_b = w_ada.astype(BF16)
    w_branch_b = w_branch.astype(BF16)
    w_out_b = w_out.astype(BF16)

    n_c = n_b + n_db
    c_rows = -(-n_c // SUBLANES) * SUBLANES
    c_all = jnp.pad(jnp.concatenate([c_prompt, c_sample], axis=0), ((0, c_rows - n_c), (0, 0)))
    mods = _mod_call(c_all, w_ada_b, b_ada)

    tab_b_p, tab_c_p = _rope_tables(jnp.arange(seq, dtype=jnp.int32))
    tab_b_s, tab_c_s = _rope_tables(past_len + jnp.arange(t_pad, dtype=jnp.int32))

    zeros_state = jnp.zeros((n_b, N_HEADS, HEAD_DIM, HEAD_DIM), F32)
    zeros_hist = jnp.zeros((n_b, SUBLANES, 3 * BRANCH_W), F32)

    xp = x_prompt.reshape(n_b * seq, d)
    xs = x_sample.reshape(n_db * dec_seq, d)
    outs = {k: [] for k in ("kp", "vp", "ks", "vs", "gp", "gs", "cp", "cs", "rp", "rs")}
    y_p = y_s = None
    for l in range(depth):
        final = l == depth - 1
        alog, dtb = _alpha_lanes(a_log[l]), _alpha_lanes(dt_bias[l])
        na, nc = norm_a[l].reshape(1, HEAD_DIM), norm_c[l].reshape(1, HEAD_DIM)
        shift, scale, gate = jnp.split(mods[l], 3, axis=-1)

        mod_p = [t[:n_b].reshape(n_b, 1, d) for t in (scale, shift, gate)]
        proj, ba = _inproj_call(xp, mod_p[0], mod_p[1], norm_in[l], w_main[l], w_ba[l],
                                tm=min(seq, 1024), rows_per_mod=seq)
        q_rot, k_rot, k_out, v_out = _rope_b_call(proj, tab_b_p, tr=min(seq, 512), t_rows=seq, cache_layout=True)
        y_a, gdn_new = _gdn_call(proj, ba, zeros_hist, zeros_state, conv_w[l], alog, dtb, na,
                                 n_b=n_b, t_rows=seq, C=GDN_CHUNK, n_valid=GDN_CHUNK, n_sub=GDN_CHUNKS_PER_STEP,
                                 y_dtype=BF16)
        y_c, ret_new = _ret_call(proj, tab_c_p[0], tab_c_p[1], zeros_state, nc,
                                 n_b=n_b, t_rows=seq, C=RET_CHUNK, n_valid=RET_CHUNK, n_sub=RET_CHUNKS_PER_STEP,
                                 y_dtype=BF16)
        y_b = _moba_p_call(q_rot, k_rot, proj, n_b=n_b, t_rows=seq)
        res = _merge_call(y_a, y_b, y_c, proj, xp, mod_p[2], w_branch_b[l], w_out_b[l], norm_f,
                          tm=min(seq, 256), rows_per_mod=seq, final=final)
        xp = res[0]
        if final:
            y_p = res[1]
        proj3 = proj.reshape(n_b, seq, MAIN_W)
        outs["kp"].append(k_out.reshape(n_b, seq, N_HEADS, HEAD_DIM))
        outs["vp"].append(v_out.reshape(n_b, seq, N_HEADS, HEAD_DIM))
        outs["gp"].append(gdn_new)
        outs["cp"].append(proj3[:, seq - (CONV_W - 1):, :3 * BRANCH_W])
        outs["rp"].append(ret_new)

        mod_s = [jnp.repeat(t[n_b:n_c], dec_seq, axis=0) for t in (scale, shift, gate)]
        proj_s, ba_s = _inproj_call(xs, mod_s[0], mod_s[1], norm_in[l], w_main[l], w_ba[l],
                                    tm=n_db * dec_seq, rows_per_mod=dec_seq)
        proj_sp = _pad_rows(proj_s, n_db, dec_seq, t_pad)
        ba_sp = _pad_rows(ba_s, n_db, dec_seq, t_pad)
        hist = jnp.pad(state_conv[l], ((0, 0), (SUBLANES - (CONV_W - 1), 0), (0, 0)))
        q_rot_s, k_rot_s = _rope_b_call(proj_sp, tab_b_s, tr=t_pad, t_rows=t_pad, cache_layout=False)
        y_a_s, gdn_new_s = _gdn_call(proj_sp, ba_sp, hist, state_gdn[l], conv_w[l], alog, dtb, na,
                                     n_b=n_db, t_rows=t_pad, C=t_pad, n_valid=dec_seq, n_sub=1, y_dtype=F32)
        y_c_s, ret_new_s = _ret_call(proj_sp, tab_c_s[0], tab_c_s[1], state_ret[l], nc,
                                     n_b=n_db, t_rows=t_pad, C=t_pad, n_valid=dec_seq, n_sub=1, y_dtype=F32)
        y_b_s = _moba_s_call(page_table, q_rot_s, k_rot_s, proj_sp, cache_k, cache_v, l, n_valid=dec_seq)
        unpad = lambda y: y.reshape(n_db, t_pad, BRANCH_W)[:, :dec_seq].reshape(n_db * dec_seq, BRANCH_W)
        res_s = _merge_call(unpad(y_a_s), unpad(y_b_s), unpad(y_c_s), proj_s, xs, mod_s[2],
                            w_branch_b[l], w_out_b[l], norm_f, tm=n_db * dec_seq, rows_per_mod=dec_seq, final=final)
        xs = res_s[0]
        if final:
            y_s = res_s[1]
        proj_s3 = proj_s.reshape(n_db, dec_seq, MAIN_W)
        outs["ks"].append(k_rot_s.reshape(n_db, t_pad, N_HEADS, HEAD_DIM)[:, :dec_seq])
        outs["vs"].append(proj_s3[:, :, COL_V_B * BRANCH_W:(COL_V_B + 1) * BRANCH_W]
                          .reshape(n_db, dec_seq, N_HEADS, HEAD_DIM))
        outs["gs"].append(gdn_new_s)
        outs["cs"].append(proj_s3[:, dec_seq - (CONV_W - 1):, :3 * BRANCH_W])
        outs["rs"].append(ret_new_s)

    st = {k: jnp.stack(v) for k, v in outs.items()}
    return (y_p.reshape(n_b, seq, d), y_s.reshape(n_db, dec_seq, d),
            st["kp"], st["vp"], st["ks"], st["vs"], st["gp"], st["gs"],
            st["cp"], st["cs"], st["rp"], st["rs"])
```

```python
import functools
import math

import jax
import jax.numpy as jnp
from jax import lax
from jax.experimental import pallas as pl
from jax.experimental.pallas import tpu as pltpu

F32 = jnp.float32
BF16 = jnp.bfloat16
HI = lax.Precision.HIGHEST

HEAD_DIM = 128
N_HEADS = 4
BRANCH_W = N_HEADS * HEAD_DIM
N_BRANCH = 3
CONV_W = 4
GDN_CHUNK = 64
RET_CHUNK = 64
MOBA_BLOCK = 256
MOBA_TOPK = 3
ROPE_THETA = 500000.0
ROPE_DIMS = HEAD_DIM // 4
RET_THETA = 10000.0
EPS = 1e-6
NEG = -1e30
SUBLANES = 8
QK_SCALE = HEAD_DIM ** -0.5
LOG2_E = math.log2(math.e)

COL_QKV_A, COL_Z_A, COL_Q_B, COL_K_B, COL_V_B, COL_Z_B = 0, 3, 4, 5, 6, 7
COL_Q_C, COL_K_C, COL_V_C, COL_Z_C, COL_MERGE = 8, 9, 10, 11, 12
MAIN_W = 18 * BRANCH_W
BA_W = 128

NT = (((1,), (1,)), ((), ()))
TN = (((0,), (0,)), ((), ()))

VMEM_LIMIT = 48 * 1024 * 1024


def _params(*sem):
    return pltpu.CompilerParams(dimension_semantics=sem, vmem_limit_bytes=VMEM_LIMIT)


def _silu(x):
    return x * jax.nn.sigmoid(x)


def _bdot(a, b):
    return jnp.dot(a.astype(BF16), b.astype(BF16), preferred_element_type=F32)


def _bdot_g(a, b, dims):
    return lax.dot_general(a.astype(BF16), b.astype(BF16), dims, preferred_element_type=F32)


def _hdot(a, b):
    return jnp.dot(a, b, precision=HI, preferred_element_type=F32)


def _mod_kernel(c_ref, w_ref, b_ref, o_ref):
    o_ref[...] = _bdot(_silu(c_ref[...]), w_ref[...]) + b_ref[...]


def _mod_call(c_all, w_ada, b_ada):
    depth, d, d3 = w_ada.shape
    rows = c_all.shape[0]
    tn = d
    return pl.pallas_call(
        _mod_kernel,
        grid=(depth, d3 // tn),
        in_specs=[pl.BlockSpec((rows, d), lambda l, j: (0, 0)),
                  pl.BlockSpec((None, d, tn), lambda l, j: (l, 0, j)),
                  pl.BlockSpec((None, 1, tn), lambda l, j: (l, 0, j))],
        out_specs=pl.BlockSpec((None, rows, tn), lambda l, j: (l, 0, j)),
        out_shape=jax.ShapeDtypeStruct((depth, rows, d3), F32),
        compiler_params=_params("arbitrary", "arbitrary"),
        name="mod",
    )(c_all, w_ada, b_ada.reshape(depth, 1, d3))


def _inproj_kernel(x_ref, sc_ref, sh_ref, g_ref, w_ref, wba_ref, o_ref, ba_ref, h_scr):
    @pl.when(pl.program_id(1) == 0)
    def _():
        x = x_ref[...]
        y = x * lax.rsqrt(jnp.mean(x * x, axis=-1, keepdims=True) + EPS) * g_ref[...]
        h = (y * (1.0 + sc_ref[...]) + sh_ref[...]).astype(BF16)
        h_scr[...] = h
        ba_ref[...] = jnp.dot(h, wba_ref[...], preferred_element_type=F32)

    o_ref[...] = jnp.dot(h_scr[...], w_ref[...], preferred_element_type=F32)


def _inproj_call(x2d, scale, shift, norm_g, w_main, w_ba, *, tm, rows_per_mod):
    m, d = x2d.shape
    tn = 1536
    assert m % tm == 0 and MAIN_W % tn == 0
    if scale.ndim == 3:
        assert rows_per_mod % tm == 0
        mod_spec = pl.BlockSpec((None, 1, d), lambda i, j: ((i * tm) // rows_per_mod, 0, 0))
    else:
        mod_spec = pl.BlockSpec((tm, d), lambda i, j: (i, 0))
    return pl.pallas_call(
        _inproj_kernel,
        grid=(m // tm, MAIN_W // tn),
        in_specs=[pl.BlockSpec((tm, d), lambda i, j: (i, 0)),
                  mod_spec, mod_spec,
                  pl.BlockSpec((1, d), lambda i, j: (0, 0)),
                  pl.BlockSpec((d, tn), lambda i, j: (0, j)),
                  pl.BlockSpec((d, BA_W), lambda i, j: (0, 0))],
        out_specs=[pl.BlockSpec((tm, tn), lambda i, j: (i, j)),
                   pl.BlockSpec((tm, BA_W), lambda i, j: (i, 0))],
        out_shape=[jax.ShapeDtypeStruct((m, MAIN_W), F32),
                   jax.ShapeDtypeStruct((m, BA_W), F32)],
        scratch_shapes=[pltpu.VMEM((tm, d), BF16)],
        compiler_params=_params("arbitrary", "arbitrary"),
        name="in_proj",
    )(x2d, scale, shift, norm_g.reshape(1, d), w_main, w_ba)


def _rope_b_kernel(q_ref, k_ref, v_ref, c_ref, s1_ref, s2_ref, qo_ref, ko_ref, *cache_refs):
    c, s1, s2 = c_ref[...], s1_ref[...], s2_ref[...]
    half = ROPE_DIMS // 2
    tr = q_ref.shape[0]
    for h in range(N_HEADS):
        sl = slice(h * HEAD_DIM, (h + 1) * HEAD_DIM)
        q = q_ref[:, sl]
        k = k_ref[:, sl]
        qr = q * c + pltpu.roll(q, half, 1) * s1 + pltpu.roll(q, HEAD_DIM - half, 1) * s2
        kr = k * c + pltpu.roll(k, half, 1) * s1 + pltpu.roll(k, HEAD_DIM - half, 1) * s2
        qo_ref[:, sl] = qr * QK_SCALE
        ko_ref[:, sl] = kr
        if cache_refs:
            kc_ref, vc_ref = cache_refs
            kc_ref[pl.ds(h, tr, stride=N_HEADS), :] = kr
            vc_ref[pl.ds(h, tr, stride=N_HEADS), :] = v_ref[:, sl]


def _rope_b_call(proj, tabs, *, tr, t_rows, cache_layout):
    m = proj.shape[0]
    nt = t_rows // tr
    assert t_rows % tr == 0 and m % t_rows == 0
    tab_spec = pl.BlockSpec((tr, HEAD_DIM), lambda i: (i % nt, 0))
    col_spec = lambda col: pl.BlockSpec((tr, BRANCH_W), lambda i: (i, col))
    out_specs = [col_spec(0), col_spec(0)]
    out_shape = [jax.ShapeDtypeStruct((m, BRANCH_W), F32)] * 2
    if cache_layout:
        out_specs += [pl.BlockSpec((tr * N_HEADS, HEAD_DIM), lambda i: (i, 0))] * 2
        out_shape += [jax.ShapeDtypeStruct((m * N_HEADS, HEAD_DIM), F32)] * 2
    return pl.pallas_call(
        _rope_b_kernel,
        grid=(m // tr,),
        in_specs=[col_spec(COL_Q_B), col_spec(COL_K_B), col_spec(COL_V_B), tab_spec, tab_spec, tab_spec],
        out_specs=out_specs,
        out_shape=out_shape,
        compiler_params=_params("arbitrary"),
        name="rope_b",
    )(proj, proj, proj, *tabs)


INV_BASE = 8
SAMPLE_SEQS_PER_STEP = 4
GDN_CHUNKS_PER_STEP = 4


def _bdot_each(xs, ys):
    return [_bdot(x, y) for x, y in zip(xs, ys)]


def _unit_lower_inverses(mats, ri, ci, n):
    base = min(INV_BASE, n)
    eye = (ri == ci).astype(F32)
    dpows = [jnp.where((ri // base) == (ci // base), a, 0.0) for a in mats]
    invs = [eye + d for d in dpows]
    for _ in range(int(math.log2(base)) - 1):
        dpows = _bdot_each(dpows, dpows)
        invs = [inv + t for inv, t in zip(invs, _bdot_each(invs, dpows))]
    size = base
    while size < n:
        off = ((ri // (2 * size)) == (ci // (2 * size))) & ((ri // size) != (ci // size))
        es = [jnp.where(off, a, 0.0) for a in mats]
        invs = [inv + t for inv, t in zip(invs, _bdot_each(invs, _bdot_each(es, invs)))]
        size *= 2
    return invs


def _gdn_kernel(qkv_ref, hist_ref, ba_ref, z_ref, s0_ref, cw_ref, alog_ref, dtb_ref, na_ref,
                y_ref, sout_ref, s_scr, tail_scr, *, C, n_valid, n_sub, carry):
    c_idx = pl.program_id(1)
    rows = n_sub * C
    u = qkv_ref[...]
    w = cw_ref[...]

    acc = u * w[CONV_W - 1:CONV_W, :]
    if carry:
        @pl.when(c_idx == 0)
        def _():
            s_scr[...] = s0_ref[...]
            tail_scr[...] = hist_ref[...]

        prev = tail_scr[...]
        row8 = lax.broadcasted_iota(jnp.int32, (SUBLANES, u.shape[1]), 0)
        for s in range(1, CONV_W):
            rolled = pltpu.roll(u, s, 0)
            first = jnp.where(row8 < s, pltpu.roll(prev, s, 0), rolled[:SUBLANES])
            shifted = first if rows == SUBLANES else jnp.concatenate([first, rolled[SUBLANES:]], axis=0)
            acc = acc + shifted * w[CONV_W - 1 - s:CONV_W - s, :]
        tail_scr[...] = u[rows - SUBLANES:, :]
    else:
        prev = hist_ref[...]
        row_in_seq = lax.broadcasted_iota(jnp.int32, u.shape, 0) % SUBLANES
        for s in range(1, CONV_W):
            shifted = jnp.where(row_in_seq < s, pltpu.roll(prev, (rows + s - SUBLANES) % rows, 0),
                                pltpu.roll(u, s, 0))
            acc = acc + shifted * w[CONV_W - 1 - s:CONV_W - s, :]
    conv = _silu(acc)

    ba = ba_ref[...]
    beta_t = jax.nn.sigmoid(ba)
    xg = ba + dtb_ref[...]
    g_t = -jnp.exp(alog_ref[...]) * (jnp.maximum(xg, 0.0) + jnp.log1p(jnp.exp(-jnp.abs(xg))))
    row_t = lax.broadcasted_iota(jnp.int32, ba.shape, 0) % C
    if n_valid < C:
        beta_t = jnp.where(row_t < n_valid, beta_t, 0.0)
        g_t = jnp.where(row_t < n_valid, g_t, 0.0)
    gc_t = g_t
    shift = 1
    while shift < C:
        gc_t = gc_t + jnp.where(row_t >= shift, pltpu.roll(gc_t, shift, 0), 0.0)
        shift *= 2

    cs = N_HEADS * C
    subs = range(n_sub)
    stack = lambda f: jnp.concatenate([f(h) for h in range(N_HEADS)], axis=0)
    ri = lax.broadcasted_iota(jnp.int32, (cs, cs), 0)
    ci = lax.broadcasted_iota(jnp.int32, (cs, cs), 1)
    same_head = (ri // C) == (ci // C)
    tri = same_head & (ri >= ci)
    strict = same_head & (ri > ci)

    qn, kn, kb, vb, gc, g_last, decay = [], [], [], [], [], [], []
    for j in subs:
        r0 = j * C
        lanes = lambda t, col: jnp.broadcast_to(t[r0:r0 + C, col:col + 1], (C, HEAD_DIM))
        head_cols = lambda base: stack(lambda h: conv[r0:r0 + C, base + h * HEAD_DIM:base + (h + 1) * HEAD_DIM])
        q, k, v = head_cols(0), head_cols(BRANCH_W), head_cols(2 * BRANCH_W)
        beta = stack(lambda h: lanes(beta_t, h))
        gc_j = stack(lambda h: lanes(gc_t, N_HEADS + h))
        if cs % HEAD_DIM == 0:
            gc_row = jnp.concatenate([gc_j.T] * (cs // HEAD_DIM), axis=0)
            gc_col = jnp.concatenate([gc_j] * (cs // HEAD_DIM), axis=1)
        else:
            g_st = stack(lambda h: lanes(g_t, N_HEADS + h))
            upper = (same_head & (ri <= ci)).astype(F32)
            gc_row = _hdot(jnp.ones((cs, cs), F32), g_st[:, :cs] * upper)
            gc_col = gc_j[:, :cs]
        kn_j = k * lax.rsqrt(jnp.sum(k * k, axis=-1, keepdims=True) + EPS)
        qn.append(q * lax.rsqrt(jnp.sum(q * q, axis=-1, keepdims=True) + EPS) * QK_SCALE)
        kn.append(kn_j)
        kb.append(kn_j * beta)
        vb.append(v * beta)
        gc.append(gc_j)
        g_last.append(stack(lambda h: jnp.broadcast_to(
            gc_t[r0 + C - 1:r0 + C, N_HEADS + h:N_HEADS + h + 1], (C, HEAD_DIM))))
        decay.append(jnp.where(tri, jnp.exp(jnp.where(tri, gc_col - gc_row, 0.0)), 0.0))

    kk = [_bdot_g(kb[j], kn[j], NT) for j in subs]
    qk = [_bdot_g(qn[j], kn[j], NT) for j in subs]
    invs = _unit_lower_inverses([-jnp.where(strict, kk[j] * decay[j], 0.0) for j in subs], ri, ci, C)
    eg = [jnp.exp(gc[j]) for j in subs]
    sols = _bdot_each(invs, [jnp.concatenate([vb[j], kb[j] * eg[j]], axis=1) for j in subs])
    attn = [qk[j] * decay[j] for j in subs]
    q_dec = [qn[j] * eg[j] for j in subs]
    k_dec = [kn[j] * jnp.exp(g_last[j] - gc[j]) for j in subs]

    hrows = [slice(h * C, (h + 1) * C) for h in range(N_HEADS)]
    states = [s_scr[h] for h in range(N_HEADS)] if carry else None
    for j in subs:
        if not carry:
            states = [s0_ref[j, h] for h in range(N_HEADS)]
        u_s, w_s = sols[j][:, :HEAD_DIM], sols[j][:, HEAD_DIM:]
        v_new = jnp.concatenate([u_s[hr] - _bdot(w_s[hr], st) for hr, st in zip(hrows, states)], axis=0)
        o = (jnp.concatenate([_bdot(q_dec[j][hr], st) for hr, st in zip(hrows, states)], axis=0)
             + _bdot(attn[j], v_new))
        states = [st * jnp.exp(g_last[j][h * C:h * C + 1, :]) + _bdot_g(k_dec[j][hr], v_new[hr], TN)
                  for h, (hr, st) in enumerate(zip(hrows, states))]
        on = o * lax.rsqrt(jnp.mean(o * o, axis=-1, keepdims=True) + EPS) * na_ref[...]
        for h, hr in enumerate(hrows):
            sl = slice(h * HEAD_DIM, (h + 1) * HEAD_DIM)
            y_ref[j * C:(j + 1) * C, sl] = (on[hr] * _silu(z_ref[j * C:(j + 1) * C, sl])).astype(y_ref.dtype)
        if not carry:
            for h in range(N_HEADS):
                sout_ref[j, h] = states[h]
    if carry:
        for h in range(N_HEADS):
            s_scr[h] = states[h]

        @pl.when(c_idx == pl.num_programs(1) - 1)
        def _():
            sout_ref[...] = s_scr[...]


def _gdn_call(proj, ba, hist, s0, conv_w, alog, dtb, norm_a, *, n_b, t_rows, C, n_valid, n_sub, carry, y_dtype):
    tile = n_sub * C
    if carry:
        n_c = t_rows // tile
        assert t_rows % tile == 0
        n_groups = n_b
        hist_spec = pl.BlockSpec((None, SUBLANES, 3 * BRANCH_W), lambda b, c: (b, 0, 0))
        state_spec = pl.BlockSpec((None, N_HEADS, HEAD_DIM, HEAD_DIM), lambda b, c: (b, 0, 0, 0))
    else:
        n_c = 1
        assert t_rows == C == SUBLANES and n_b % n_sub == 0
        n_groups = n_b // n_sub
        hist_spec = pl.BlockSpec((tile, 3 * BRANCH_W), lambda b, c: (b, 0))
        state_spec = pl.BlockSpec((n_sub, N_HEADS, HEAD_DIM, HEAD_DIM), lambda b, c: (b, 0, 0, 0))
    row = lambda b, c: b * n_c + c
    vec_spec = pl.BlockSpec((1, HEAD_DIM), lambda b, c: (0, 0))
    return pl.pallas_call(
        functools.partial(_gdn_kernel, C=C, n_valid=n_valid, n_sub=n_sub, carry=carry),
        grid=(n_groups, n_c),
        in_specs=[pl.BlockSpec((tile, 3 * BRANCH_W), lambda b, c: (row(b, c), 0)),
                  hist_spec,
                  pl.BlockSpec((tile, BA_W), lambda b, c: (row(b, c), 0)),
                  pl.BlockSpec((tile, BRANCH_W), lambda b, c: (row(b, c), COL_Z_A)),
                  state_spec,
                  pl.BlockSpec((CONV_W, 3 * BRANCH_W), lambda b, c: (0, 0)),
                  vec_spec, vec_spec, vec_spec],
        out_specs=[pl.BlockSpec((tile, BRANCH_W), lambda b, c: (row(b, c), 0)), state_spec],
        out_shape=[jax.ShapeDtypeStruct((n_b * t_rows, BRANCH_W), y_dtype),
                   jax.ShapeDtypeStruct((n_b, N_HEADS, HEAD_DIM, HEAD_DIM), F32)],
        scratch_shapes=[pltpu.VMEM((N_HEADS, HEAD_DIM, HEAD_DIM), F32),
                        pltpu.VMEM((SUBLANES, 3 * BRANCH_W), F32)],
        compiler_params=_params("arbitrary", "arbitrary"),
        name="gdn",
    )(proj, hist, ba, proj, s0, conv_w, alog, dtb, norm_a)


RET_CHUNKS_PER_STEP = 4

def _ret_kernel(q_ref, k_ref, v_ref, z_ref, cos_ref, sin_ref, s0_ref, dmat_ref, qdec_ref, kdec_ref,
                cdec_ref, nc_ref, y_ref, sout_ref, s_scr, *, C, n_sub, carry):
    c_idx = pl.program_id(1)

    if carry:
        @pl.when(c_idx == 0)
        def _():
            s_scr[...] = s0_ref[...]

    cos, sin = cos_ref[...], sin_ref[...]
    heads = [slice(h * HEAD_DIM, (h + 1) * HEAD_DIM) for h in range(N_HEADS)]
    units = [(j, h) for j in range(n_sub) for h in range(N_HEADS)]
    rows = lambda j: slice(j * C, (j + 1) * C)
    rope = lambda x: x * cos + pltpu.roll(x, HEAD_DIM // 2, 1) * sin
    qr = [rope(q_ref[:, sl]) for sl in heads]
    kr = [rope(k_ref[:, sl]) * QK_SCALE for sl in heads]
    vs = [v_ref[:, sl] for sl in heads]
    sc = [_bdot_g(qr[h][rows(j)], kr[h][rows(j)], NT) * dmat_ref[h] for j, h in units]
    o_intra = [_bdot(s, vs[h][rows(j)]) for s, (j, h) in zip(sc, units)]
    kv = [_bdot_g(kr[h][rows(j)] * kdec_ref[h], vs[h][rows(j)], TN) for j, h in units]

    states = [s_scr[h] for h in range(N_HEADS)] if carry else None
    for j in range(n_sub):
        if not carry:
            states = [s0_ref[j, h] for h in range(N_HEADS)]
        for h, sl in enumerate(heads):
            u = j * N_HEADS + h
            o = o_intra[u] + _bdot(qr[h][rows(j)] * qdec_ref[h], states[h])
            states[h] = states[h] * cdec_ref[h] + kv[u]
            on = o * lax.rsqrt(jnp.mean(o * o, axis=-1, keepdims=True) + EPS) * nc_ref[...]
            y_ref[rows(j), sl] = (on * _silu(z_ref[rows(j), sl])).astype(y_ref.dtype)
            if not carry:
                sout_ref[j, h] = states[h]
    if carry:
        for h in range(N_HEADS):
            s_scr[h] = states[h]

        @pl.when(c_idx == pl.num_programs(1) - 1)
        def _():
            sout_ref[...] = s_scr[...]


def _ret_tables(C, n_valid):
    log_g = jnp.log1p(-(2.0 ** (-5.0 - jnp.arange(N_HEADS, dtype=F32))))
    idx = jnp.arange(C, dtype=F32)
    tri = jnp.tril(jnp.ones((C, C), dtype=bool))
    dmat = jnp.where(tri, jnp.exp(log_g[:, None, None] * jnp.where(tri, idx[:, None] - idx[None, :], 0.0)), 0.0)
    q_dec = jnp.exp(log_g[:, None] * (idx + 1.0))[..., None]
    k_dec = jnp.exp(log_g[:, None] * (n_valid - 1.0 - idx))[..., None]
    c_dec = jnp.exp(log_g * n_valid)[:, None, None]
    bc = lambda t, r: jnp.broadcast_to(t, (N_HEADS, r, HEAD_DIM))
    return dmat, bc(q_dec, C), bc(k_dec, C), bc(c_dec, 1)


def _ret_call(proj, cos2, sin2, s0, norm_c, *, n_b, t_rows, C, n_valid, n_sub, carry, y_dtype):
    tile = n_sub * C
    if carry:
        n_c = t_rows // tile
        assert t_rows % tile == 0
        n_groups = n_b
        state_spec = pl.BlockSpec((None, N_HEADS, HEAD_DIM, HEAD_DIM), lambda b, c: (b, 0, 0, 0))
    else:
        n_c = 1
        assert t_rows == C and n_b % n_sub == 0
        n_groups = n_b // n_sub
        cos2, sin2 = jnp.tile(cos2, (n_sub, 1)), jnp.tile(sin2, (n_sub, 1))
        state_spec = pl.BlockSpec((n_sub, N_HEADS, HEAD_DIM, HEAD_DIM), lambda b, c: (b, 0, 0, 0))
    row = lambda b, c: b * n_c + c
    dmat, q_dec, k_dec, c_dec = _ret_tables(C, n_valid)
    col_spec = lambda col: pl.BlockSpec((tile, BRANCH_W), lambda b, c: (row(b, c), col))
    tab_spec = pl.BlockSpec((tile, HEAD_DIM), lambda b, c: (c, 0))
    const_spec = lambda r, w: pl.BlockSpec((N_HEADS, r, w), lambda b, c: (0, 0, 0))
    return pl.pallas_call(
        functools.partial(_ret_kernel, C=C, n_sub=n_sub, carry=carry),
        grid=(n_groups, n_c),
        in_specs=[col_spec(COL_Q_C), col_spec(COL_K_C), col_spec(COL_V_C), col_spec(COL_Z_C),
                  tab_spec, tab_spec, state_spec,
                  const_spec(C, C), const_spec(C, HEAD_DIM), const_spec(C, HEAD_DIM), const_spec(1, HEAD_DIM),
                  pl.BlockSpec((1, HEAD_DIM), lambda b, c: (0, 0))],
        out_specs=[pl.BlockSpec((tile, BRANCH_W), lambda b, c: (row(b, c), 0)), state_spec],
        out_shape=[jax.ShapeDtypeStruct((n_b * t_rows, BRANCH_W), y_dtype),
                   jax.ShapeDtypeStruct((n_b, N_HEADS, HEAD_DIM, HEAD_DIM), F32)],
        scratch_shapes=[pltpu.VMEM((N_HEADS, HEAD_DIM, HEAD_DIM), F32)],
        compiler_params=_params("arbitrary", "arbitrary"),
        name="ret",
    )(proj, proj, proj, proj, cos2, sin2, s0, dmat, q_dec, k_dec, c_dec, norm_c)


def _topk_mask(gate, valid, axis):
    idx = lax.broadcasted_iota(jnp.int32, gate.shape, axis)
    gm = jnp.where(valid, gate, -jnp.inf)
    rank = jnp.zeros(gate.shape, jnp.int32)
    for m in range(gate.shape[axis]):
        gmm = gm[m:m + 1, :] if axis == 0 else gm[:, m:m + 1]
        rank = rank + jnp.where(gmm > gm, 1, jnp.where(gmm == gm, jnp.where(idx > m, 1, 0), 0))
    return valid & (rank < MOBA_TOPK)


def _moba_p_kernel(q_ref, k_ref, v_ref, z_ref, y_ref, kmean_scr, kb_scr, vt_scr, bias_scr, acc_scr, *, n_blk):
    qi = pl.program_id(1)
    blk = MOBA_BLOCK
    heads = [slice(h * HEAD_DIM, (h + 1) * HEAD_DIM) for h in range(N_HEADS)]

    @pl.when(qi == 0)
    def _():
        def prep(n, carry):
            rows = pl.ds(pl.multiple_of(n * blk, blk), blk)
            for h, sl in enumerate(heads):
                kn = k_ref[rows, sl]
                kmean_scr[h, pl.ds(n, 1), :] = jnp.sum(kn, axis=0, keepdims=True) * (1.0 / blk)
                kb_scr[h * n_blk + n] = kn.astype(BF16)
                vt_scr[h * n_blk + n] = v_ref[rows, sl].T.astype(BF16)
            return carry

        lax.fori_loop(0, n_blk, prep, 0)

    key_i = lax.broadcasted_iota(jnp.int32, (blk, blk), 0)
    qry_i = lax.broadcasted_iota(jnp.int32, (blk, blk), 1)
    blk_id = lax.broadcasted_iota(jnp.int32, (n_blk, blk), 0)
    qs = [q_ref[:, sl] for sl in heads]
    qbs = [(q * LOG2_E).astype(BF16) for q in qs]
    ss = [jnp.where(key_i <= qry_i,
                    lax.dot_general(kb_scr[h * n_blk + qi], qbs[h], NT, preferred_element_type=F32), NEG)
          for h in range(N_HEADS)]
    ms = [jnp.max(s, axis=0, keepdims=True) for s in ss]
    ps = [jnp.exp2(ss[h] - ms[h]) for h in range(N_HEADS)]
    ls = [jnp.sum(p, axis=0, keepdims=True) for p in ps]
    for h in range(N_HEADS):
        acc_scr[h] = jnp.dot(vt_scr[h * n_blk + qi], ps[h].astype(BF16), preferred_element_type=F32)
    for h in range(N_HEADS):
        gate = lax.dot_general(kmean_scr[h], qs[h], NT, precision=HI, preferred_element_type=F32)
        bias_scr[h] = jnp.where(_topk_mask(gate, blk_id < qi, 0), 0.0, NEG)

    hs = range(N_HEADS)
    HEAD_GROUPS = (tuple(hs),)

    def body(i, carry):
        ms, ls = carry
        ns = (2 * i, 2 * i + 1)
        ms_new, ls_new = list(ms), list(ls)
        for grp in HEAD_GROUPS:
            ss = {h: [lax.dot_general(kb_scr[h * n_blk + n], qbs[h], NT, preferred_element_type=F32)
                      + bias_scr[h, pl.ds(n, 1), :] for n in ns] for h in grp}
            for h in grp:
                ms_new[h] = jnp.maximum(ms[h], jnp.max(jnp.maximum(ss[h][0], ss[h][1]), axis=0, keepdims=True))
            ps = {h: [jnp.exp2(s - ms_new[h]) for s in ss[h]] for h in grp}
            alphas = {h: jnp.exp2(ms[h] - ms_new[h]) for h in grp}
            pvs = {h: [jnp.dot(vt_scr[h * n_blk + n], p.astype(BF16), preferred_element_type=F32)
                       for n, p in zip(ns, ps[h])] for h in grp}
            for h in grp:
                ls_new[h] = alphas[h] * ls[h] + jnp.sum(ps[h][0] + ps[h][1], axis=0, keepdims=True)
                acc_scr[h] = alphas[h] * acc_scr[h] + (pvs[h][0] + pvs[h][1])
        return tuple(ms_new), tuple(ls_new)

    _, ls = lax.fori_loop(0, (qi + 1) // 2, body, (tuple(ms), tuple(ls)))
    for h, sl in enumerate(heads):
        y_ref[:, sl] = ((acc_scr[h] / ls[h]).T * _silu(z_ref[:, sl])).astype(y_ref.dtype)


def _moba_p_call(q_rot, k_rot, proj, *, n_b, t_rows):
    n_blk = t_rows // MOBA_BLOCK
    assert t_rows % MOBA_BLOCK == 0 and n_blk >= MOBA_TOPK
    blk = MOBA_BLOCK
    seq_spec = lambda col: pl.BlockSpec((t_rows, BRANCH_W), lambda b, i: (b, col), pipeline_mode=pl.Buffered(1))
    tile_spec = lambda col: pl.BlockSpec((blk, BRANCH_W), lambda b, i: (b * n_blk + i, col))
    return pl.pallas_call(
        functools.partial(_moba_p_kernel, n_blk=n_blk),
        grid=(n_b, n_blk),
        in_specs=[tile_spec(0), seq_spec(0), seq_spec(COL_V_B), tile_spec(COL_Z_B)],
        out_specs=tile_spec(0),
        out_shape=jax.ShapeDtypeStruct((n_b * t_rows, BRANCH_W), BF16),
        scratch_shapes=[pltpu.VMEM((N_HEADS, n_blk, HEAD_DIM), F32),
                        pltpu.VMEM((N_HEADS * n_blk, blk, HEAD_DIM), BF16),
                        pltpu.VMEM((N_HEADS * n_blk, HEAD_DIM, blk), BF16),
                        pltpu.VMEM((N_HEADS, n_blk, blk), F32),
                        pltpu.VMEM((N_HEADS, HEAD_DIM, blk), F32)],
        compiler_params=_params("arbitrary", "arbitrary"),
        name="moba_p",
    )(q_rot, k_rot, proj, proj)


PAGES_PER_STEP = 16


def _moba_s_kernel(pt_ref, q_ref, kn_ref, vn_ref, z_ref, *refs, n_valid, pages_per_blk, pps):
    del pt_ref
    k_refs, v_refs = refs[:pps], refs[pps:2 * pps]
    y_ref, q_scr, ksum_scr, m_scr, l_scr, o_scr = refs[2 * pps:]
    step = pl.program_id(1)
    rows = N_HEADS * SUBLANES
    page_rows = k_refs[0].shape[0]
    heads = [slice(h * HEAD_DIM, (h + 1) * HEAD_DIM) for h in range(N_HEADS)]
    hrows = [slice(h * SUBLANES, (h + 1) * SUBLANES) for h in range(N_HEADS)]

    @pl.when(step == 0)
    def _():
        q = q_ref[...]
        q_scr[...] = jnp.concatenate([q[:, sl] for sl in heads], axis=0)

    q32 = q_scr[...]
    qb = q32.astype(BF16)
    row_head = lax.broadcasted_iota(jnp.int32, (rows, page_rows), 0) // SUBLANES
    col_head = lax.broadcasted_iota(jnp.int32, (rows, page_rows), 1) % N_HEADS
    head_bias = jnp.where(row_head == col_head, 0.0, NEG)

    blks_per_step = pps // pages_per_blk
    kps = [k_refs[j][...] for j in range(pps)]
    ss = [_bdot_g(qb, kp, NT) + head_bias for kp in kps]
    ms = [jnp.max(s, axis=1, keepdims=True) for s in ss]
    es = [jnp.exp(s - m) for s, m in zip(ss, ms)]
    os_ = [_bdot(e, v_refs[j][...]) for j, e in enumerate(es)]
    for j in range(pps):
        pg = step * pps + j
        m_scr[pg] = jnp.broadcast_to(ms[j], (rows, HEAD_DIM))
        l_scr[pg] = jnp.broadcast_to(jnp.sum(es[j], axis=1, keepdims=True), (rows, HEAD_DIM))
        o_scr[pg] = os_[j]
    parts = [jnp.sum(kp.reshape(page_rows // SUBLANES, SUBLANES, HEAD_DIM), axis=0) for kp in kps]
    for bl in range(blks_per_step):
        ksum = parts[bl * pages_per_blk]
        for j in range(bl * pages_per_blk + 1, (bl + 1) * pages_per_blk):
            ksum = ksum + parts[j]
        ksum_scr[step * blks_per_step + bl] = ksum

    @pl.when(step == pl.num_programs(1) - 1)
    def _():
        n_pg = m_scr.shape[0]
        n_blk = ksum_scr.shape[0]
        kflat = ksum_scr[...].reshape(n_blk * SUBLANES, HEAD_DIM) * (1.0 / MOBA_BLOCK)
        g_all = lax.dot_general(q32, kflat, NT, precision=HI, preferred_element_type=F32)
        rh = lax.broadcasted_iota(jnp.int32, g_all.shape, 0) // SUBLANES
        ch = lax.broadcasted_iota(jnp.int32, g_all.shape, 1) % N_HEADS
        pool = (lax.broadcasted_iota(jnp.int32, (n_blk * SUBLANES, n_blk), 0) // SUBLANES
                == lax.broadcasted_iota(jnp.int32, (n_blk * SUBLANES, n_blk), 1)).astype(F32)
        gate = _hdot(jnp.where(rh == ch, g_all, 0.0), pool)
        sel_f = jnp.where(_topk_mask(gate, jnp.ones(gate.shape, jnp.bool_), 1), 1.0, 0.0)
        selw = [jnp.broadcast_to(sel_f[:, n:n + 1], (rows, HEAD_DIM)) > 0.5 for n in range(n_blk)]

        kn, vn = kn_ref[...], vn_ref[...]
        s_own = jnp.concatenate([_bdot_g(q32[hr], kn[:, sl], NT) for hr, sl in zip(hrows, heads)], axis=0)
        rq = lax.broadcasted_iota(jnp.int32, s_own.shape, 0) % SUBLANES
        cj = lax.broadcasted_iota(jnp.int32, s_own.shape, 1)
        own_ok = (cj <= rq) & (cj < n_valid)
        s_own = jnp.where(own_ok, s_own, NEG)
        mx = jnp.broadcast_to(jnp.max(s_own, axis=1, keepdims=True), (rows, HEAD_DIM))
        for pg in range(n_pg):
            mx = jnp.maximum(mx, jnp.where(selw[pg // pages_per_blk], m_scr[pg], NEG))
        e_own = jnp.where(own_ok, jnp.exp(s_own - mx[:, :SUBLANES]), 0.0)
        l_tot = jnp.broadcast_to(jnp.sum(e_own, axis=1, keepdims=True), (rows, HEAD_DIM))
        o_tot = jnp.concatenate([_bdot(e_own[hr], vn[:, sl]) for hr, sl in zip(hrows, heads)], axis=0)
        for pg in range(n_pg):
            wgt = jnp.where(selw[pg // pages_per_blk], jnp.exp(jnp.minimum(m_scr[pg] - mx, 0.0)), 0.0)
            l_tot = l_tot + wgt * l_scr[pg]
            o_tot = o_tot + wgt * o_scr[pg]
        o = o_tot / l_tot
        y_ref[...] = jnp.concatenate([o[hr] for hr in hrows], axis=1) * _silu(z_ref[...])


def _moba_s_call(page_table, q_rot, k_rot, proj, cache_k, cache_v, layer, *, n_valid):
    n_b, n_pages = page_table.shape
    depth, n_pool, page = cache_k.shape[:3]
    assert MOBA_BLOCK % page == 0
    pages_per_blk = MOBA_BLOCK // page
    pps = PAGES_PER_STEP
    assert pps % pages_per_blk == 0 and n_pages % pps == 0 and n_pages // pages_per_blk >= MOBA_TOPK
    assert (page * N_HEADS) % SUBLANES == 0 and SUBLANES % N_HEADS == 0
    ck = cache_k.reshape(depth, n_pool, page * N_HEADS, HEAD_DIM)
    cv = cache_v.reshape(depth, n_pool, page * N_HEADS, HEAD_DIM)
    rows = N_HEADS * SUBLANES
    row_spec = lambda col: pl.BlockSpec((SUBLANES, BRANCH_W), lambda b, p, pt: (b, col))

    def page_spec(j):
        return pl.BlockSpec((None, None, page * N_HEADS, HEAD_DIM),
                            lambda b, p, pt: (layer, pt[b, p * pps + j], 0, 0))

    page_specs = [page_spec(j) for j in range(pps)]
    grid_spec = pltpu.PrefetchScalarGridSpec(
        num_scalar_prefetch=1,
        grid=(n_b, n_pages // pps),
        in_specs=[row_spec(0), row_spec(0), row_spec(COL_V_B), row_spec(COL_Z_B)] + page_specs + page_specs,
        out_specs=pl.BlockSpec((SUBLANES, BRANCH_W), lambda b, p, pt: (b, 0)),
        scratch_shapes=[pltpu.VMEM((rows, HEAD_DIM), F32),
                        pltpu.VMEM((n_pages // pages_per_blk, SUBLANES, HEAD_DIM), F32),
                        pltpu.VMEM((n_pages, rows, HEAD_DIM), F32),
                        pltpu.VMEM((n_pages, rows, HEAD_DIM), F32),
                        pltpu.VMEM((n_pages, rows, HEAD_DIM), F32)])
    return pl.pallas_call(
        functools.partial(_moba_s_kernel, n_valid=n_valid, pages_per_blk=pages_per_blk, pps=pps),
        grid_spec=grid_spec,
        out_shape=jax.ShapeDtypeStruct((n_b * SUBLANES, BRANCH_W), F32),
        compiler_params=_params("arbitrary", "arbitrary"),
        name="moba_s",
    )(page_table, q_rot, k_rot, proj, proj, *([ck] * pps), *([cv] * pps))


def _merge_kernel(ya_ref, yb_ref, yc_ref, mg_ref, x_ref, gate_ref, wb_ref, wo_ref, nf_ref, *out_refs, final):
    d = x_ref.shape[1]
    mixed = None
    for n, y_ref in enumerate((ya_ref, yb_ref, yc_ref)):
        per_branch = _bdot(y_ref[...], wb_ref[n])
        term = jax.nn.sigmoid(mg_ref[:, n * d:(n + 1) * d]) * per_branch
        mixed = term if mixed is None else mixed + term
    x_out = x_ref[...] + gate_ref[...] * _bdot(mixed, wo_ref[...])
    out_refs[0][...] = x_out
    if final:
        out_refs[1][...] = (x_out * lax.rsqrt(jnp.mean(x_out * x_out, axis=-1, keepdims=True) + EPS)
                            * nf_ref[...])


def _merge_call(y_a, y_b, y_c, proj, x2d, gate, w_branch, w_out, norm_f, *, tm, rows_per_mod, final):
    m, d = x2d.shape
    assert m % tm == 0
    if gate.ndim == 3:
        assert rows_per_mod % tm == 0
        gate_spec = pl.BlockSpec((None, 1, d), lambda i: ((i * tm) // rows_per_mod, 0, 0))
    else:
        gate_spec = pl.BlockSpec((tm, d), lambda i: (i, 0))
    y_spec = pl.BlockSpec((tm, BRANCH_W), lambda i: (i, 0))
    x_spec = pl.BlockSpec((tm, d), lambda i: (i, 0))
    n_out = 2 if final else 1
    outs = pl.pallas_call(
        functools.partial(_merge_kernel, final=final),
        grid=(m // tm,),
        in_specs=[y_spec, y_spec, y_spec,
                  pl.BlockSpec((tm, N_BRANCH * d), lambda i: (i, (COL_MERGE * BRANCH_W) // (N_BRANCH * d))),
                  x_spec, gate_spec,
                  pl.BlockSpec((N_BRANCH, BRANCH_W, d), lambda i: (0, 0, 0)),
                  pl.BlockSpec((d, d), lambda i: (0, 0)),
                  pl.BlockSpec((1, d), lambda i: (0, 0))],
        out_specs=[x_spec] * n_out,
        out_shape=[jax.ShapeDtypeStruct((m, d), F32)] * n_out,
        compiler_params=_params("arbitrary"),
        name="merge",
    )(y_a, y_b, y_c, proj, x2d, gate, w_branch, w_out, norm_f.reshape(1, d))
    return outs


def _rope_tables(pos):
    t = pos.shape[0]
    posf = pos.astype(F32)

    def cos_sin(n_rot, theta):
        half = n_rot // 2
        inv = theta ** (-jnp.arange(half, dtype=F32) / half)
        ang = posf[:, None] * inv[None, :]
        return jnp.cos(ang), jnp.sin(ang)

    cb, sb = cos_sin(ROPE_DIMS, ROPE_THETA)
    hb = ROPE_DIMS // 2
    tab_b = (jnp.concatenate([cb, cb, jnp.ones((t, HEAD_DIM - ROPE_DIMS), F32)], axis=1),
             jnp.concatenate([jnp.zeros((t, hb), F32), sb, jnp.zeros((t, HEAD_DIM - ROPE_DIMS), F32)], axis=1),
             jnp.concatenate([-sb, jnp.zeros((t, HEAD_DIM - hb), F32)], axis=1))
    cc, sc = cos_sin(HEAD_DIM, RET_THETA)
    tab_c = (jnp.concatenate([cc, cc], axis=1), jnp.concatenate([-sc, sc], axis=1))
    return tab_b, tab_c


def _pad_rows(a2d, n_b, t, t_pad):
    w = a2d.shape[1]
    return jnp.pad(a2d.reshape(n_b, t, w), ((0, 0), (0, t_pad - t), (0, 0))).reshape(n_b * t_pad, w)


def _alpha_lanes(v):
    return jnp.pad(v.astype(F32), (N_HEADS, BA_W - 2 * N_HEADS)).reshape(1, BA_W)


def kernel(x_prompt, x_sample, cache_k, cache_v, state_gdn, state_conv, state_ret, page_table, c_prompt, c_sample,
           norm_in, w_ada, b_ada, w_in, conv_w, a_log, dt_bias, norm_a, norm_c, w_branch, w_out, norm_f):
    n_b, seq, d = x_prompt.shape
    n_db, dec_seq, _ = x_sample.shape
    depth = w_in.shape[0]
    n_pages = page_table.shape[1]
    past_len = n_pages * cache_k.shape[2]
    assert d == 2 * BRANCH_W and dec_seq <= SUBLANES and dec_seq >= CONV_W - 1
    assert seq % GDN_CHUNK == 0 and seq % RET_CHUNK == 0
    t_pad = SUBLANES

    ba0 = 4 * BRANCH_W
    w_main = jnp.concatenate([w_in[:, :, :ba0], w_in[:, :, ba0 + 2 * N_HEADS:]], axis=2).astype(BF16)
    w_ba = jnp.pad(w_in[:, :, ba0:ba0 + 2 * N_HEADS], ((0, 0), (0, 0), (0, BA_W - 2 * N_HEADS))).astype(BF16)
    w_ada_b = w_ada.astype(BF16)
    w_branch_b = w_branch.astype(BF16)
    w_out_b = w_out.astype(BF16)

    n_c = n_b + n_db
    c_rows = -(-n_c // SUBLANES) * SUBLANES
    c_all = jnp.pad(jnp.concatenate([c_prompt, c_sample], axis=0), ((0, c_rows - n_c), (0, 0)))
    mods = _mod_call(c_all, w_ada_b, b_ada)

    tab_b_p, tab_c_p = _rope_tables(jnp.arange(seq, dtype=jnp.int32))
    tab_b_s, tab_c_s = _rope_tables(past_len + jnp.arange(t_pad, dtype=jnp.int32))

    zeros_state = jnp.zeros((n_b, N_HEADS, HEAD_DIM, HEAD_DIM), F32)
    zeros_hist = jnp.zeros((n_b, SUBLANES, 3 * BRANCH_W), F32)

    xp = x_prompt.reshape(n_b * seq, d)
    xs = x_sample.reshape(n_db * dec_seq, d)
    outs = {k: [] for k in ("kp", "vp", "ks", "vs", "gp", "gs", "cp", "cs", "rp", "rs")}
    y_p = y_s = None
    for l in range(depth):
        final = l == depth - 1
        alog, dtb = _alpha_lanes(a_log[l]), _alpha_lanes(dt_bias[l])
        na, nc = norm_a[l].reshape(1, HEAD_DIM), norm_c[l].reshape(1, HEAD_DIM)
        shift, scale, gate = jnp.split(mods[l], 3, axis=-1)

        mod_p = [t[:n_b].reshape(n_b, 1, d) for t in (scale, shift, gate)]
        proj, ba = _inproj_call(xp, mod_p[0], mod_p[1], norm_in[l], w_main[l], w_ba[l],
                                tm=min(seq, 1024), rows_per_mod=seq)
        q_rot, k_rot, k_out, v_out = _rope_b_call(proj, tab_b_p, tr=min(seq, 512), t_rows=seq, cache_layout=True)
        y_a, gdn_new = _gdn_call(proj, ba, zeros_hist, zeros_state, conv_w[l], alog, dtb, na,
                                 n_b=n_b, t_rows=seq, C=GDN_CHUNK, n_valid=GDN_CHUNK, n_sub=GDN_CHUNKS_PER_STEP,
                                 carry=True, y_dtype=BF16)
        y_c, ret_new = _ret_call(proj, tab_c_p[0], tab_c_p[1], zeros_state, nc,
                                 n_b=n_b, t_rows=seq, C=RET_CHUNK, n_valid=RET_CHUNK, n_sub=RET_CHUNKS_PER_STEP,
                                 carry=True, y_dtype=BF16)
        y_b = _moba_p_call(q_rot, k_rot, proj, n_b=n_b, t_rows=seq)
        res = _merge_call(y_a, y_b, y_c, proj, xp, mod_p[2], w_branch_b[l], w_out_b[l], norm_f,
                          tm=min(seq, 256), rows_per_mod=seq, final=final)
        xp = res[0]
        if final:
            y_p = res[1]
        proj3 = proj.reshape(n_b, seq, MAIN_W)
        outs["kp"].append(k_out.reshape(n_b, seq, N_HEADS, HEAD_DIM))
        outs["vp"].append(v_out.reshape(n_b, seq, N_HEADS, HEAD_DIM))
        outs["gp"].append(gdn_new)
        outs["cp"].append(proj3[:, seq - (CONV_W - 1):, :3 * BRANCH_W])
        outs["rp"].append(ret_new)

        mod_s = [jnp.repeat(t[n_b:n_c], dec_seq, axis=0) for t in (scale, shift, gate)]
        proj_s, ba_s = _inproj_call(xs, mod_s[0], mod_s[1], norm_in[l], w_main[l], w_ba[l],
                                    tm=n_db * dec_seq, rows_per_mod=dec_seq)
        proj_sp = _pad_rows(proj_s, n_db, dec_seq, t_pad)
        ba_sp = _pad_rows(ba_s, n_db, dec_seq, t_pad)
        hist = jnp.pad(state_conv[l], ((0, 0), (SUBLANES - (CONV_W - 1), 0), (0, 0)))
        q_rot_s, k_rot_s = _rope_b_call(proj_sp, tab_b_s, tr=t_pad, t_rows=t_pad, cache_layout=False)
        y_a_s, gdn_new_s = _gdn_call(proj_sp, ba_sp, hist.reshape(n_db * SUBLANES, 3 * BRANCH_W), state_gdn[l],
                                     conv_w[l], alog, dtb, na, n_b=n_db, t_rows=t_pad, C=t_pad, n_valid=dec_seq,
                                     n_sub=math.gcd(n_db, SAMPLE_SEQS_PER_STEP), carry=False, y_dtype=F32)
        y_c_s, ret_new_s = _ret_call(proj_sp, tab_c_s[0], tab_c_s[1], state_ret[l], nc,
                                     n_b=n_db, t_rows=t_pad, C=t_pad, n_valid=dec_seq,
                                     n_sub=math.gcd(n_db, SAMPLE_SEQS_PER_STEP), carry=False, y_dtype=F32)
        y_b_s = _moba_s_call(page_table, q_rot_s, k_rot_s, proj_sp, cache_k, cache_v, l, n_valid=dec_seq)
        unpad = lambda y: y.reshape(n_db, t_pad, BRANCH_W)[:, :dec_seq].reshape(n_db * dec_seq, BRANCH_W)
        res_s = _merge_call(unpad(y_a_s), unpad(y_b_s), unpad(y_c_s), proj_s, xs, mod_s[2],
                            w_branch_b[l], w_out_b[l], norm_f, tm=n_db * dec_seq, rows_per_mod=dec_seq, final=final)
        xs = res_s[0]
        if final:
            y_s = res_s[1]
        proj_s3 = proj_s.reshape(n_db, dec_seq, MAIN_W)
        outs["ks"].append(k_rot_s.reshape(n_db, t_pad, N_HEADS, HEAD_DIM)[:, :dec_seq])
        outs["vs"].append(proj_s3[:, :, COL_V_B * BRANCH_W:(COL_V_B + 1) * BRANCH_W]
                          .reshape(n_db, dec_seq, N_HEADS, HEAD_DIM))
        outs["gs"].append(gdn_new_s)
        outs["cs"].append(proj_s3[:, dec_seq - (CONV_W - 1):, :3 * BRANCH_W])
        outs["rs"].append(ret_new_s)

    st = {k: jnp.stack(v) for k, v in outs.items()}
    return (y_p.reshape(n_b, seq, d), y_s.reshape(n_db, dec_seq, d),
            st["kp"], st["vp"], st["ks"], st["vs"], st["gp"], st["gs"],
            st["cp"], st["cs"], st["rp"], st["rs"])
```

```python
import functools
import math

import jax
import jax.numpy as jnp
from jax import lax
from jax.experimental import pallas as pl
from jax.experimental.pallas import tpu as pltpu

F32 = jnp.float32
BF16 = jnp.bfloat16
HI = lax.Precision.HIGHEST

HEAD_DIM = 128
N_HEADS = 4
BRANCH_W = N_HEADS * HEAD_DIM
N_BRANCH = 3
CONV_W = 4
GDN_CHUNK = 64
RET_CHUNK = 64
MOBA_BLOCK = 256
MOBA_TOPK = 3
ROPE_THETA = 500000.0
ROPE_DIMS = HEAD_DIM // 4
RET_THETA = 10000.0
EPS = 1e-6
NEG = -1e30
SUBLANES = 8
QK_SCALE = HEAD_DIM ** -0.5
LOG2_E = math.log2(math.e)

COL_QKV_A, COL_Z_A, COL_Q_B, COL_K_B, COL_V_B, COL_Z_B = 0, 3, 4, 5, 6, 7
COL_Q_C, COL_K_C, COL_V_C, COL_Z_C, COL_MERGE = 8, 9, 10, 11, 12
MAIN_W = 18 * BRANCH_W
BA_W = 128

NT = (((1,), (1,)), ((), ()))
TN = (((0,), (0,)), ((), ()))

VMEM_LIMIT = 48 * 1024 * 1024


def _params(*sem):
    return pltpu.CompilerParams(dimension_semantics=sem, vmem_limit_bytes=VMEM_LIMIT)


def _silu(x):
    return x * jax.nn.sigmoid(x)


def _bdot(a, b):
    return jnp.dot(a.astype(BF16), b.astype(BF16), preferred_element_type=F32)


def _bdot_g(a, b, dims):
    return lax.dot_general(a.astype(BF16), b.astype(BF16), dims, preferred_element_type=F32)


def _hdot(a, b):
    return jnp.dot(a, b, precision=HI, preferred_element_type=F32)


def _mod_kernel(c_ref, w_ref, b_ref, o_ref):
    o_ref[...] = _bdot(_silu(c_ref[...]), w_ref[...]) + b_ref[...]


def _mod_call(c_all, w_ada, b_ada):
    depth, d, d3 = w_ada.shape
    rows = c_all.shape[0]
    tn = d
    return pl.pallas_call(
        _mod_kernel,
        grid=(depth, d3 // tn),
        in_specs=[pl.BlockSpec((rows, d), lambda l, j: (0, 0)),
                  pl.BlockSpec((None, d, tn), lambda l, j: (l, 0, j)),
                  pl.BlockSpec((None, 1, tn), lambda l, j: (l, 0, j))],
        out_specs=pl.BlockSpec((None, rows, tn), lambda l, j: (l, 0, j)),
        out_shape=jax.ShapeDtypeStruct((depth, rows, d3), F32),
        compiler_params=_params("arbitrary", "arbitrary"),
        name="mod",
    )(c_all, w_ada, b_ada.reshape(depth, 1, d3))


def _inproj_kernel(x_ref, sc_ref, sh_ref, g_ref, w_ref, wba_ref, o_ref, ba_ref, h_scr):
    @pl.when(pl.program_id(1) == 0)
    def _():
        x = x_ref[...]
        y = x * lax.rsqrt(jnp.mean(x * x, axis=-1, keepdims=True) + EPS) * g_ref[...]
        h = (y * (1.0 + sc_ref[...]) + sh_ref[...]).astype(BF16)
        h_scr[...] = h
        ba_ref[...] = jnp.dot(h, wba_ref[...], preferred_element_type=F32)

    o_ref[...] = jnp.dot(h_scr[...], w_ref[...], preferred_element_type=F32)


def _inproj_call(x2d, scale, shift, norm_g, w_main, w_ba, *, tm, rows_per_mod):
    m, d = x2d.shape
    tn = 1536
    assert m % tm == 0 and MAIN_W % tn == 0
    if scale.ndim == 3:
        assert rows_per_mod % tm == 0
        mod_spec = pl.BlockSpec((None, 1, d), lambda i, j: ((i * tm) // rows_per_mod, 0, 0))
    else:
        mod_spec = pl.BlockSpec((tm, d), lambda i, j: (i, 0))
    return pl.pallas_call(
        _inproj_kernel,
        grid=(m // tm, MAIN_W // tn),
        in_specs=[pl.BlockSpec((tm, d), lambda i, j: (i, 0)),
                  mod_spec, mod_spec,
                  pl.BlockSpec((1, d), lambda i, j: (0, 0)),
                  pl.BlockSpec((d, tn), lambda i, j: (0, j)),
                  pl.BlockSpec((d, BA_W), lambda i, j: (0, 0))],
        out_specs=[pl.BlockSpec((tm, tn), lambda i, j: (i, j)),
                   pl.BlockSpec((tm, BA_W), lambda i, j: (i, 0))],
        out_shape=[jax.ShapeDtypeStruct((m, MAIN_W), F32),
                   jax.ShapeDtypeStruct((m, BA_W), F32)],
        scratch_shapes=[pltpu.VMEM((tm, d), BF16)],
        compiler_params=_params("arbitrary", "arbitrary"),
        name="in_proj",
    )(x2d, scale, shift, norm_g.reshape(1, d), w_main, w_ba)


def _rope_b_kernel(q_ref, k_ref, v_ref, c_ref, s1_ref, s2_ref, qo_ref, ko_ref, *cache_refs):
    c, s1, s2 = c_ref[...], s1_ref[...], s2_ref[...]
    half = ROPE_DIMS // 2
    tr = q_ref.shape[0]
    for h in range(N_HEADS):
        sl = slice(h * HEAD_DIM, (h + 1) * HEAD_DIM)
        q = q_ref[:, sl]
        k = k_ref[:, sl]
        qr = q * c + pltpu.roll(q, half, 1) * s1 + pltpu.roll(q, HEAD_DIM - half, 1) * s2
        kr = k * c + pltpu.roll(k, half, 1) * s1 + pltpu.roll(k, HEAD_DIM - half, 1) * s2
        qo_ref[:, sl] = qr * QK_SCALE
        ko_ref[:, sl] = kr
        if cache_refs:
            kc_ref, vc_ref = cache_refs
            kc_ref[pl.ds(h, tr, stride=N_HEADS), :] = kr
            vc_ref[pl.ds(h, tr, stride=N_HEADS), :] = v_ref[:, sl]


def _rope_b_call(proj, tabs, *, tr, t_rows, cache_layout):
    m = proj.shape[0]
    nt = t_rows // tr
    assert t_rows % tr == 0 and m % t_rows == 0
    tab_spec = pl.BlockSpec((tr, HEAD_DIM), lambda i: (i % nt, 0))
    col_spec = lambda col: pl.BlockSpec((tr, BRANCH_W), lambda i: (i, col))
    out_specs = [col_spec(0), col_spec(0)]
    out_shape = [jax.ShapeDtypeStruct((m, BRANCH_W), F32)] * 2
    if cache_layout:
        out_specs += [pl.BlockSpec((tr * N_HEADS, HEAD_DIM), lambda i: (i, 0))] * 2
        out_shape += [jax.ShapeDtypeStruct((m * N_HEADS, HEAD_DIM), F32)] * 2
    return pl.pallas_call(
        _rope_b_kernel,
        grid=(m // tr,),
        in_specs=[col_spec(COL_Q_B), col_spec(COL_K_B), col_spec(COL_V_B), tab_spec, tab_spec, tab_spec],
        out_specs=out_specs,
        out_shape=out_shape,
        compiler_params=_params("arbitrary"),
        name="rope_b",
    )(proj, proj, proj, *tabs)


INV_BASE = 8
SAMPLE_SEQS_PER_STEP = 4
GDN_CHUNKS_PER_STEP = 8


def _bdot_each(xs, ys):
    return [_bdot(x, y) for x, y in zip(xs, ys)]


def _unit_lower_inverses(mats, ri, ci, n):
    base = min(INV_BASE, n)
    eye = (ri == ci).astype(F32)
    dpows = [jnp.where((ri // base) == (ci // base), a, 0.0) for a in mats]
    invs = [eye + d for d in dpows]
    for _ in range(int(math.log2(base)) - 1):
        dpows = _bdot_each(dpows, dpows)
        invs = [inv + t for inv, t in zip(invs, _bdot_each(invs, dpows))]
    size = base
    while size < n:
        off = ((ri // (2 * size)) == (ci // (2 * size))) & ((ri // size) != (ci // size))
        es = [jnp.where(off, a, 0.0) for a in mats]
        invs = [inv + t for inv, t in zip(invs, _bdot_each(invs, _bdot_each(es, invs)))]
        size *= 2
    return invs


def _gdn_kernel(qkv_ref, hist_ref, ba_ref, z_ref, s0_ref, cw_ref, alog_ref, dtb_ref, na_ref,
                y_ref, sout_ref, s_scr, tail_scr, *, C, n_valid, n_sub, carry):
    c_idx = pl.program_id(1)
    rows = n_sub * C
    u = qkv_ref[...]
    w = cw_ref[...]

    acc = u * w[CONV_W - 1:CONV_W, :]
    if carry:
        @pl.when(c_idx == 0)
        def _():
            s_scr[...] = s0_ref[...]
            tail_scr[...] = hist_ref[...]

        prev = tail_scr[...]
        row8 = lax.broadcasted_iota(jnp.int32, (SUBLANES, u.shape[1]), 0)
        for s in range(1, CONV_W):
            rolled = pltpu.roll(u, s, 0)
            first = jnp.where(row8 < s, pltpu.roll(prev, s, 0), rolled[:SUBLANES])
            shifted = first if rows == SUBLANES else jnp.concatenate([first, rolled[SUBLANES:]], axis=0)
            acc = acc + shifted * w[CONV_W - 1 - s:CONV_W - s, :]
        tail_scr[...] = u[rows - SUBLANES:, :]
    else:
        prev = hist_ref[...]
        row_in_seq = lax.broadcasted_iota(jnp.int32, u.shape, 0) % SUBLANES
        for s in range(1, CONV_W):
            shifted = jnp.where(row_in_seq < s, pltpu.roll(prev, (rows + s - SUBLANES) % rows, 0),
                                pltpu.roll(u, s, 0))
            acc = acc + shifted * w[CONV_W - 1 - s:CONV_W - s, :]
    conv = _silu(acc)

    ba = ba_ref[...]
    beta_t = jax.nn.sigmoid(ba)
    xg = ba + dtb_ref[...]
    g_t = -jnp.exp(alog_ref[...]) * (jnp.maximum(xg, 0.0) + jnp.log1p(jnp.exp(-jnp.abs(xg))))
    row_t = lax.broadcasted_iota(jnp.int32, ba.shape, 0) % C
    if n_valid < C:
        beta_t = jnp.where(row_t < n_valid, beta_t, 0.0)
        g_t = jnp.where(row_t < n_valid, g_t, 0.0)
    gc_t = g_t
    shift = 1
    while shift < C:
        gc_t = gc_t + jnp.where(row_t >= shift, pltpu.roll(gc_t, shift, 0), 0.0)
        shift *= 2

    cs = N_HEADS * C
    subs = range(n_sub)
    stack = lambda f: jnp.concatenate([f(h) for h in range(N_HEADS)], axis=0)
    ri = lax.broadcasted_iota(jnp.int32, (cs, cs), 0)
    ci = lax.broadcasted_iota(jnp.int32, (cs, cs), 1)
    same_head = (ri // C) == (ci // C)
    tri = same_head & (ri >= ci)
    strict = same_head & (ri > ci)

    qn, kn, kb, vb, gc, g_last, decay = [], [], [], [], [], [], []
    for j in subs:
        r0 = j * C
        lanes = lambda t, col: jnp.broadcast_to(t[r0:r0 + C, col:col + 1], (C, HEAD_DIM))
        head_cols = lambda base: stack(lambda h: conv[r0:r0 + C, base + h * HEAD_DIM:base + (h + 1) * HEAD_DIM])
        q, k, v = head_cols(0), head_cols(BRANCH_W), head_cols(2 * BRANCH_W)
        beta = stack(lambda h: lanes(beta_t, h))
        gc_j = stack(lambda h: lanes(gc_t, N_HEADS + h))
        if cs % HEAD_DIM == 0:
            gc_row = jnp.concatenate([gc_j.T] * (cs // HEAD_DIM), axis=0)
            gc_col = jnp.concatenate([gc_j] * (cs // HEAD_DIM), axis=1)
        else:
            g_st = stack(lambda h: lanes(g_t, N_HEADS + h))
            upper = (same_head & (ri <= ci)).astype(F32)
            gc_row = _hdot(jnp.ones((cs, cs), F32), g_st[:, :cs] * upper)
            gc_col = gc_j[:, :cs]
        kn_j = k * lax.rsqrt(jnp.sum(k * k, axis=-1, keepdims=True) + EPS)
        qn.append(q * lax.rsqrt(jnp.sum(q * q, axis=-1, keepdims=True) + EPS) * QK_SCALE)
        kn.append(kn_j)
        kb.append(kn_j * beta)
        vb.append(v * beta)
        gc.append(gc_j)
        g_last.append(stack(lambda h: jnp.broadcast_to(
            gc_t[r0 + C - 1:r0 + C, N_HEADS + h:N_HEADS + h + 1], (C, HEAD_DIM))))
        decay.append(jnp.where(tri, jnp.exp(jnp.where(tri, gc_col - gc_row, 0.0)), 0.0))

    kk = [_bdot_g(kb[j], kn[j], NT) for j in subs]
    qk = [_bdot_g(qn[j], kn[j], NT) for j in subs]
    invs = _unit_lower_inverses([-jnp.where(strict, kk[j] * decay[j], 0.0) for j in subs], ri, ci, C)
    eg = [jnp.exp(gc[j]) for j in subs]
    sols = _bdot_each(invs, [jnp.concatenate([vb[j], kb[j] * eg[j]], axis=1) for j in subs])
    attn = [qk[j] * decay[j] for j in subs]
    q_dec = [qn[j] * eg[j] for j in subs]
    k_dec = [kn[j] * jnp.exp(g_last[j] - gc[j]) for j in subs]

    hrows = [slice(h * C, (h + 1) * C) for h in range(N_HEADS)]
    states = [s_scr[h] for h in range(N_HEADS)] if carry else None
    for j in subs:
        if not carry:
            states = [s0_ref[j, h] for h in range(N_HEADS)]
        u_s, w_s = sols[j][:, :HEAD_DIM], sols[j][:, HEAD_DIM:]
        v_new = jnp.concatenate([u_s[hr] - _bdot(w_s[hr], st) for hr, st in zip(hrows, states)], axis=0)
        o = (jnp.concatenate([_bdot(q_dec[j][hr], st) for hr, st in zip(hrows, states)], axis=0)
             + _bdot(attn[j], v_new))
        states = [st * jnp.exp(g_last[j][h * C:h * C + 1, :]) + _bdot_g(k_dec[j][hr], v_new[hr], TN)
                  for h, (hr, st) in enumerate(zip(hrows, states))]
        on = o * lax.rsqrt(jnp.mean(o * o, axis=-1, keepdims=True) + EPS) * na_ref[...]
        for h, hr in enumerate(hrows):
            sl = slice(h * HEAD_DIM, (h + 1) * HEAD_DIM)
            y_ref[j * C:(j + 1) * C, sl] = (on[hr] * _silu(z_ref[j * C:(j + 1) * C, sl])).astype(y_ref.dtype)
        if not carry:
            for h in range(N_HEADS):
                sout_ref[j, h] = states[h]
    if carry:
        for h in range(N_HEADS):
            s_scr[h] = states[h]

        @pl.when(c_idx == pl.num_programs(1) - 1)
        def _():
            sout_ref[...] = s_scr[...]


def _gdn_call(proj, ba, hist, s0, conv_w, alog, dtb, norm_a, *, n_b, t_rows, C, n_valid, n_sub, carry, y_dtype):
    tile = n_sub * C
    if carry:
        n_c = t_rows // tile
        assert t_rows % tile == 0
        n_groups = n_b
        hist_spec = pl.BlockSpec((None, SUBLANES, 3 * BRANCH_W), lambda b, c: (b, 0, 0))
        state_spec = pl.BlockSpec((None, N_HEADS, HEAD_DIM, HEAD_DIM), lambda b, c: (b, 0, 0, 0))
    else:
        n_c = 1
        assert t_rows == C == SUBLANES and n_b % n_sub == 0
        n_groups = n_b // n_sub
        hist_spec = pl.BlockSpec((tile, 3 * BRANCH_W), lambda b, c: (b, 0))
        state_spec = pl.BlockSpec((n_sub, N_HEADS, HEAD_DIM, HEAD_DIM), lambda b, c: (b, 0, 0, 0))
    row = lambda b, c: b * n_c + c
    vec_spec = pl.BlockSpec((1, HEAD_DIM), lambda b, c: (0, 0))
    return pl.pallas_call(
        functools.partial(_gdn_kernel, C=C, n_valid=n_valid, n_sub=n_sub, carry=carry),
        grid=(n_groups, n_c),
        in_specs=[pl.BlockSpec((tile, 3 * BRANCH_W), lambda b, c: (row(b, c), 0)),
                  hist_spec,
                  pl.BlockSpec((tile, BA_W), lambda b, c: (row(b, c), 0)),
                  pl.BlockSpec((tile, BRANCH_W), lambda b, c: (row(b, c), COL_Z_A)),
                  state_spec,
                  pl.BlockSpec((CONV_W, 3 * BRANCH_W), lambda b, c: (0, 0)),
                  vec_spec, vec_spec, vec_spec],
        out_specs=[pl.BlockSpec((tile, BRANCH_W), lambda b, c: (row(b, c), 0)), state_spec],
        out_shape=[jax.ShapeDtypeStruct((n_b * t_rows, BRANCH_W), y_dtype),
                   jax.ShapeDtypeStruct((n_b, N_HEADS, HEAD_DIM, HEAD_DIM), F32)],
        scratch_shapes=[pltpu.VMEM((N_HEADS, HEAD_DIM, HEAD_DIM), F32),
                        pltpu.VMEM((SUBLANES, 3 * BRANCH_W), F32)],
        compiler_params=_params("arbitrary", "arbitrary"),
        name="gdn",
    )(proj, hist, ba, proj, s0, conv_w, alog, dtb, norm_a)


RET_CHUNKS_PER_STEP = 4

def _ret_kernel(q_ref, k_ref, v_ref, z_ref, cos_ref, sin_ref, s0_ref, dmat_ref, qdec_ref, kdec_ref,
                cdec_ref, nc_ref, y_ref, sout_ref, s_scr, *, C, n_sub, carry):
    c_idx = pl.program_id(1)

    if carry:
        @pl.when(c_idx == 0)
        def _():
            s_scr[...] = s0_ref[...]

    cos, sin = cos_ref[...], sin_ref[...]
    heads = [slice(h * HEAD_DIM, (h + 1) * HEAD_DIM) for h in range(N_HEADS)]
    units = [(j, h) for j in range(n_sub) for h in range(N_HEADS)]
    rows = lambda j: slice(j * C, (j + 1) * C)
    rope = lambda x: x * cos + pltpu.roll(x, HEAD_DIM // 2, 1) * sin
    qr = [rope(q_ref[:, sl]) for sl in heads]
    kr = [rope(k_ref[:, sl]) * QK_SCALE for sl in heads]
    vs = [v_ref[:, sl] for sl in heads]
    sc = [_bdot_g(qr[h][rows(j)], kr[h][rows(j)], NT) * dmat_ref[h] for j, h in units]
    o_intra = [_bdot(s, vs[h][rows(j)]) for s, (j, h) in zip(sc, units)]
    kv = [_bdot_g(kr[h][rows(j)] * kdec_ref[h], vs[h][rows(j)], TN) for j, h in units]

    states = [s_scr[h] for h in range(N_HEADS)] if carry else None
    for j in range(n_sub):
        if not carry:
            states = [s0_ref[j, h] for h in range(N_HEADS)]
        for h, sl in enumerate(heads):
            u = j * N_HEADS + h
            o = o_intra[u] + _bdot(qr[h][rows(j)] * qdec_ref[h], states[h])
            states[h] = states[h] * cdec_ref[h] + kv[u]
            on = o * lax.rsqrt(jnp.mean(o * o, axis=-1, keepdims=True) + EPS) * nc_ref[...]
            y_ref[rows(j), sl] = (on * _silu(z_ref[rows(j), sl])).astype(y_ref.dtype)
            if not carry:
                sout_ref[j, h] = states[h]
    if carry:
        for h in range(N_HEADS):
            s_scr[h] = states[h]

        @pl.when(c_idx == pl.num_programs(1) - 1)
        def _():
            sout_ref[...] = s_scr[...]


def _ret_tables(C, n_valid):
    log_g = jnp.log1p(-(2.0 ** (-5.0 - jnp.arange(N_HEADS, dtype=F32))))
    idx = jnp.arange(C, dtype=F32)
    tri = jnp.tril(jnp.ones((C, C), dtype=bool))
    dmat = jnp.where(tri, jnp.exp(log_g[:, None, None] * jnp.where(tri, idx[:, None] - idx[None, :], 0.0)), 0.0)
    q_dec = jnp.exp(log_g[:, None] * (idx + 1.0))[..., None]
    k_dec = jnp.exp(log_g[:, None] * (n_valid - 1.0 - idx))[..., None]
    c_dec = jnp.exp(log_g * n_valid)[:, None, None]
    bc = lambda t, r: jnp.broadcast_to(t, (N_HEADS, r, HEAD_DIM))
    return dmat, bc(q_dec, C), bc(k_dec, C), bc(c_dec, 1)


def _ret_call(proj, cos2, sin2, s0, norm_c, *, n_b, t_rows, C, n_valid, n_sub, carry, y_dtype):
    tile = n_sub * C
    if carry:
        n_c = t_rows // tile
        assert t_rows % tile == 0
        n_groups = n_b
        state_spec = pl.BlockSpec((None, N_HEADS, HEAD_DIM, HEAD_DIM), lambda b, c: (b, 0, 0, 0))
    else:
        n_c = 1
        assert t_rows == C and n_b % n_sub == 0
        n_groups = n_b // n_sub
        cos2, sin2 = jnp.tile(cos2, (n_sub, 1)), jnp.tile(sin2, (n_sub, 1))
        state_spec = pl.BlockSpec((n_sub, N_HEADS, HEAD_DIM, HEAD_DIM), lambda b, c: (b, 0, 0, 0))
    row = lambda b, c: b * n_c + c
    dmat, q_dec, k_dec, c_dec = _ret_tables(C, n_valid)
    col_spec = lambda col: pl.BlockSpec((tile, BRANCH_W), lambda b, c: (row(b, c), col))
    tab_spec = pl.BlockSpec((tile, HEAD_DIM), lambda b, c: (c, 0))
    const_spec = lambda r, w: pl.BlockSpec((N_HEADS, r, w), lambda b, c: (0, 0, 0))
    return pl.pallas_call(
        functools.partial(_ret_kernel, C=C, n_sub=n_sub, carry=carry),
        grid=(n_groups, n_c),
        in_specs=[col_spec(COL_Q_C), col_spec(COL_K_C), col_spec(COL_V_C), col_spec(COL_Z_C),
                  tab_spec, tab_spec, state_spec,
                  const_spec(C, C), const_spec(C, HEAD_DIM), const_spec(C, HEAD_DIM), const_spec(1, HEAD_DIM),
                  pl.BlockSpec((1, HEAD_DIM), lambda b, c: (0, 0))],
        out_specs=[pl.BlockSpec((tile, BRANCH_W), lambda b, c: (row(b, c), 0)), state_spec],
        out_shape=[jax.ShapeDtypeStruct((n_b * t_rows, BRANCH_W), y_dtype),
                   jax.ShapeDtypeStruct((n_b, N_HEADS, HEAD_DIM, HEAD_DIM), F32)],
        scratch_shapes=[pltpu.VMEM((N_HEADS, HEAD_DIM, HEAD_DIM), F32)],
        compiler_params=_params("arbitrary", "arbitrary"),
        name="ret",
    )(proj, proj, proj, proj, cos2, sin2, s0, dmat, q_dec, k_dec, c_dec, norm_c)


def _topk_mask(gate, valid, axis):
    idx = lax.broadcasted_iota(jnp.int32, gate.shape, axis)
    gm = jnp.where(valid, gate, -jnp.inf)
    rank = jnp.zeros(gate.shape, jnp.int32)
    for m in range(gate.shape[axis]):
        gmm = gm[m:m + 1, :] if axis == 0 else gm[:, m:m + 1]
        rank = rank + jnp.where(gmm > gm, 1, jnp.where(gmm == gm, jnp.where(idx > m, 1, 0), 0))
    return valid & (rank < MOBA_TOPK)


def _moba_p_kernel(q_ref, k_ref, v_ref, z_ref, y_ref, kmean_scr, kb_scr, vt_scr, bias_scr, acc_scr, *, n_blk):
    qi = pl.program_id(1)
    blk = MOBA_BLOCK
    heads = [slice(h * HEAD_DIM, (h + 1) * HEAD_DIM) for h in range(N_HEADS)]

    @pl.when(qi == 0)
    def _():
        def prep(n, carry):
            rows = pl.ds(pl.multiple_of(n * blk, blk), blk)
            for h, sl in enumerate(heads):
                kn = k_ref[rows, sl]
                kmean_scr[h, pl.ds(n, 1), :] = jnp.sum(kn, axis=0, keepdims=True) * (1.0 / blk)
                kb_scr[h * n_blk + n] = kn.astype(BF16)
                vt_scr[h * n_blk + n] = v_ref[rows, sl].T.astype(BF16)
            return carry

        lax.fori_loop(0, n_blk, prep, 0)

    key_i = lax.broadcasted_iota(jnp.int32, (blk, blk), 0)
    qry_i = lax.broadcasted_iota(jnp.int32, (blk, blk), 1)
    blk_id = lax.broadcasted_iota(jnp.int32, (n_blk, blk), 0)
    qs = [q_ref[:, sl] for sl in heads]
    qbs = [(q * LOG2_E).astype(BF16) for q in qs]
    ss = [jnp.where(key_i <= qry_i,
                    lax.dot_general(kb_scr[h * n_blk + qi], qbs[h], NT, preferred_element_type=F32), NEG)
          for h in range(N_HEADS)]
    ms = [jnp.max(s, axis=0, keepdims=True) for s in ss]
    ps = [jnp.exp2(ss[h] - ms[h]) for h in range(N_HEADS)]
    ls = [jnp.sum(p, axis=0, keepdims=True) for p in ps]
    for h in range(N_HEADS):
        acc_scr[h] = jnp.dot(vt_scr[h * n_blk + qi], ps[h].astype(BF16), preferred_element_type=F32)
    for h in range(N_HEADS):
        gate = lax.dot_general(kmean_scr[h], qs[h], NT, precision=HI, preferred_element_type=F32)
        bias_scr[h] = jnp.where(_topk_mask(gate, blk_id < qi, 0), 0.0, NEG)

    hs = range(N_HEADS)
    HEAD_GROUPS = (tuple(hs),)

    def body(i, carry):
        ms, ls = carry
        ns = (2 * i, 2 * i + 1)
        ms_new, ls_new = list(ms), list(ls)
        for grp in HEAD_GROUPS:
            ss = {h: [lax.dot_general(kb_scr[h * n_blk + n], qbs[h], NT, preferred_element_type=F32)
                      + bias_scr[h, pl.ds(n, 1), :] for n in ns] for h in grp}
            for h in grp:
                ms_new[h] = jnp.maximum(ms[h], jnp.max(jnp.maximum(ss[h][0], ss[h][1]), axis=0, keepdims=True))
            ps = {h: [jnp.exp2(s - ms_new[h]) for s in ss[h]] for h in grp}
            alphas = {h: jnp.exp2(ms[h] - ms_new[h]) for h in grp}
            pvs = {h: [jnp.dot(vt_scr[h * n_blk + n], p.astype(BF16), preferred_element_type=F32)
                       for n, p in zip(ns, ps[h])] for h in grp}
            for h in grp:
                ls_new[h] = alphas[h] * ls[h] + jnp.sum(ps[h][0] + ps[h][1], axis=0, keepdims=True)
                acc_scr[h] = alphas[h] * acc_scr[h] + (pvs[h][0] + pvs[h][1])
        return tuple(ms_new), tuple(ls_new)

    _, ls = lax.fori_loop(0, (qi + 1) // 2, body, (tuple(ms), tuple(ls)))
    for h, sl in enumerate(heads):
        y_ref[:, sl] = ((acc_scr[h] / ls[h]).T * _silu(z_ref[:, sl])).astype(y_ref.dtype)


def _moba_p_call(q_rot, k_rot, proj, *, n_b, t_rows):
    n_blk = t_rows // MOBA_BLOCK
    assert t_rows % MOBA_BLOCK == 0 and n_blk >= MOBA_TOPK
    blk = MOBA_BLOCK
    seq_spec = lambda col: pl.BlockSpec((t_rows, BRANCH_W), lambda b, i: (b, col), pipeline_mode=pl.Buffered(1))
    tile_spec = lambda col: pl.BlockSpec((blk, BRANCH_W), lambda b, i: (b * n_blk + i, col))
    return pl.pallas_call(
        functools.partial(_moba_p_kernel, n_blk=n_blk),
        grid=(n_b, n_blk),
        in_specs=[tile_spec(0), seq_spec(0), seq_spec(COL_V_B), tile_spec(COL_Z_B)],
        out_specs=tile_spec(0),
        out_shape=jax.ShapeDtypeStruct((n_b * t_rows, BRANCH_W), BF16),
        scratch_shapes=[pltpu.VMEM((N_HEADS, n_blk, HEAD_DIM), F32),
                        pltpu.VMEM((N_HEADS * n_blk, blk, HEAD_DIM), BF16),
                        pltpu.VMEM((N_HEADS * n_blk, HEAD_DIM, blk), BF16),
                        pltpu.VMEM((N_HEADS, n_blk, blk), F32),
                        pltpu.VMEM((N_HEADS, HEAD_DIM, blk), F32)],
        compiler_params=_params("arbitrary", "arbitrary"),
        name="moba_p",
    )(q_rot, k_rot, proj, proj)


PAGES_PER_STEP = 16


def _moba_s_kernel(pt_ref, q_ref, kn_ref, vn_ref, z_ref, *refs, n_valid, pages_per_blk, pps):
    del pt_ref
    k_refs, v_refs = refs[:pps], refs[pps:2 * pps]
    y_ref, q_scr, ksum_scr, m_scr, l_scr, o_scr = refs[2 * pps:]
    step = pl.program_id(1)
    rows = N_HEADS * SUBLANES
    page_rows = k_refs[0].shape[0]
    heads = [slice(h * HEAD_DIM, (h + 1) * HEAD_DIM) for h in range(N_HEADS)]
    hrows = [slice(h * SUBLANES, (h + 1) * SUBLANES) for h in range(N_HEADS)]

    @pl.when(step == 0)
    def _():
        q = q_ref[...]
        q_scr[...] = jnp.concatenate([q[:, sl] for sl in heads], axis=0)

    q32 = q_scr[...]
    qb = q32.astype(BF16)
    row_head = lax.broadcasted_iota(jnp.int32, (rows, page_rows), 0) // SUBLANES
    col_head = lax.broadcasted_iota(jnp.int32, (rows, page_rows), 1) % N_HEADS
    head_bias = jnp.where(row_head == col_head, 0.0, NEG)

    blks_per_step = pps // pages_per_blk
    kps = [k_refs[j][...] for j in range(pps)]
    ss = [_bdot_g(qb, kp, NT) + head_bias for kp in kps]
    ms = [jnp.max(s, axis=1, keepdims=True) for s in ss]
    es = [jnp.exp(s - m) for s, m in zip(ss, ms)]
    os_ = [_bdot(e, v_refs[j][...]) for j, e in enumerate(es)]
    for j in range(pps):
        pg = step * pps + j
        m_scr[pg] = jnp.broadcast_to(ms[j], (rows, HEAD_DIM))
        l_scr[pg] = jnp.broadcast_to(jnp.sum(es[j], axis=1, keepdims=True), (rows, HEAD_DIM))
        o_scr[pg] = os_[j]
    parts = [jnp.sum(kp.reshape(page_rows // SUBLANES, SUBLANES, HEAD_DIM), axis=0) for kp in kps]
    for bl in range(blks_per_step):
        ksum = parts[bl * pages_per_blk]
        for j in range(bl * pages_per_blk + 1, (bl + 1) * pages_per_blk):
            ksum = ksum + parts[j]
        ksum_scr[step * blks_per_step + bl] = ksum

    @pl.when(step == pl.num_programs(1) - 1)
    def _():
        n_pg = m_scr.shape[0]
        n_blk = ksum_scr.shape[0]
        kflat = ksum_scr[...].reshape(n_blk * SUBLANES, HEAD_DIM) * (1.0 / MOBA_BLOCK)
        g_all = lax.dot_general(q32, kflat, NT, precision=HI, preferred_element_type=F32)
        rh = lax.broadcasted_iota(jnp.int32, g_all.shape, 0) // SUBLANES
        ch = lax.broadcasted_iota(jnp.int32, g_all.shape, 1) % N_HEADS
        pool = (lax.broadcasted_iota(jnp.int32, (n_blk * SUBLANES, n_blk), 0) // SUBLANES
                == lax.broadcasted_iota(jnp.int32, (n_blk * SUBLANES, n_blk), 1)).astype(F32)
        gate = _hdot(jnp.where(rh == ch, g_all, 0.0), pool)
        sel_f = jnp.where(_topk_mask(gate, jnp.ones(gate.shape, jnp.bool_), 1), 1.0, 0.0)
        selw = [jnp.broadcast_to(sel_f[:, n:n + 1], (rows, HEAD_DIM)) > 0.5 for n in range(n_blk)]

        kn, vn = kn_ref[...], vn_ref[...]
        s_own = jnp.concatenate([_bdot_g(q32[hr], kn[:, sl], NT) for hr, sl in zip(hrows, heads)], axis=0)
        rq = lax.broadcasted_iota(jnp.int32, s_own.shape, 0) % SUBLANES
        cj = lax.broadcasted_iota(jnp.int32, s_own.shape, 1)
        own_ok = (cj <= rq) & (cj < n_valid)
        s_own = jnp.where(own_ok, s_own, NEG)
        mx = jnp.broadcast_to(jnp.max(s_own, axis=1, keepdims=True), (rows, HEAD_DIM))
        for pg in range(n_pg):
            mx = jnp.maximum(mx, jnp.where(selw[pg // pages_per_blk], m_scr[pg], NEG))
        e_own = jnp.where(own_ok, jnp.exp(s_own - mx[:, :SUBLANES]), 0.0)
        l_tot = jnp.broadcast_to(jnp.sum(e_own, axis=1, keepdims=True), (rows, HEAD_DIM))
        o_tot = jnp.concatenate([_bdot(e_own[hr], vn[:, sl]) for hr, sl in zip(hrows, heads)], axis=0)
        for pg in range(n_pg):
            wgt = jnp.where(selw[pg // pages_per_blk], jnp.exp(jnp.minimum(m_scr[pg] - mx, 0.0)), 0.0)
            l_tot = l_tot + wgt * l_scr[pg]
            o_tot = o_tot + wgt * o_scr[pg]
        o = o_tot / l_tot
        y_ref[...] = jnp.concatenate([o[hr] for hr in hrows], axis=1) * _silu(z_ref[...])


def _moba_s_call(page_table, q_rot, k_rot, proj, cache_k, cache_v, layer, *, n_valid):
    n_b, n_pages = page_table.shape
    depth, n_pool, page = cache_k.shape[:3]
    assert MOBA_BLOCK % page == 0
    pages_per_blk = MOBA_BLOCK // page
    pps = PAGES_PER_STEP
    assert pps % pages_per_blk == 0 and n_pages % pps == 0 and n_pages // pages_per_blk >= MOBA_TOPK
    assert (page * N_HEADS) % SUBLANES == 0 and SUBLANES % N_HEADS == 0
    ck = cache_k.reshape(depth, n_pool, page * N_HEADS, HEAD_DIM)
    cv = cache_v.reshape(depth, n_pool, page * N_HEADS, HEAD_DIM)
    rows = N_HEADS * SUBLANES
    row_spec = lambda col: pl.BlockSpec((SUBLANES, BRANCH_W), lambda b, p, pt: (b, col))

    def page_spec(j):
        return pl.BlockSpec((None, None, page * N_HEADS, HEAD_DIM),
                            lambda b, p, pt: (layer, pt[b, p * pps + j], 0, 0))

    page_specs = [page_spec(j) for j in range(pps)]
    grid_spec = pltpu.PrefetchScalarGridSpec(
        num_scalar_prefetch=1,
        grid=(n_b, n_pages // pps),
        in_specs=[row_spec(0), row_spec(0), row_spec(COL_V_B), row_spec(COL_Z_B)] + page_specs + page_specs,
        out_specs=pl.BlockSpec((SUBLANES, BRANCH_W), lambda b, p, pt: (b, 0)),
        scratch_shapes=[pltpu.VMEM((rows, HEAD_DIM), F32),
                        pltpu.VMEM((n_pages // pages_per_blk, SUBLANES, HEAD_DIM), F32),
                        pltpu.VMEM((n_pages, rows, HEAD_DIM), F32),
                        pltpu.VMEM((n_pages, rows, HEAD_DIM), F32),
                        pltpu.VMEM((n_pages, rows, HEAD_DIM), F32)])
    return pl.pallas_call(
        functools.partial(_moba_s_kernel, n_valid=n_valid, pages_per_blk=pages_per_blk, pps=pps),
        grid_spec=grid_spec,
        out_shape=jax.ShapeDtypeStruct((n_b * SUBLANES, BRANCH_W), F32),
        compiler_params=_params("arbitrary", "arbitrary"),
        name="moba_s",
    )(page_table, q_rot, k_rot, proj, proj, *([ck] * pps), *([cv] * pps))


def _merge_kernel(ya_ref, yb_ref, yc_ref, mg_ref, x_ref, gate_ref, wb_ref, wo_ref, nf_ref, *out_refs, final):
    d = x_ref.shape[1]
    mixed = None
    for n, y_ref in enumerate((ya_ref, yb_ref, yc_ref)):
        per_branch = _bdot(y_ref[...], wb_ref[n])
        term = jax.nn.sigmoid(mg_ref[:, n * d:(n + 1) * d]) * per_branch
        mixed = term if mixed is None else mixed + term
    x_out = x_ref[...] + gate_ref[...] * _bdot(mixed, wo_ref[...])
    out_refs[0][...] = x_out
    if final:
        out_refs[1][...] = (x_out * lax.rsqrt(jnp.mean(x_out * x_out, axis=-1, keepdims=True) + EPS)
                            * nf_ref[...])


def _merge_call(y_a, y_b, y_c, proj, x2d, gate, w_branch, w_out, norm_f, *, tm, rows_per_mod, final):
    m, d = x2d.shape
    assert m % tm == 0
    if gate.ndim == 3:
        assert rows_per_mod % tm == 0
        gate_spec = pl.BlockSpec((None, 1, d), lambda i: ((i * tm) // rows_per_mod, 0, 0))
    else:
        gate_spec = pl.BlockSpec((tm, d), lambda i: (i, 0))
    y_spec = pl.BlockSpec((tm, BRANCH_W), lambda i: (i, 0))
    x_spec = pl.BlockSpec((tm, d), lambda i: (i, 0))
    n_out = 2 if final else 1
    outs = pl.pallas_call(
        functools.partial(_merge_kernel, final=final),
        grid=(m // tm,),
        in_specs=[y_spec, y_spec, y_spec,
                  pl.BlockSpec((tm, N_BRANCH * d), lambda i: (i, (COL_MERGE * BRANCH_W) // (N_BRANCH * d))),
                  x_spec, gate_spec,
                  pl.BlockSpec((N_BRANCH, BRANCH_W, d), lambda i: (0, 0, 0)),
                  pl.BlockSpec((d, d), lambda i: (0, 0)),
                  pl.BlockSpec((1, d), lambda i: (0, 0))],
        out_specs=[x_spec] * n_out,
        out_shape=[jax.ShapeDtypeStruct((m, d), F32)] * n_out,
        compiler_params=_params("arbitrary"),
        name="merge",
    )(y_a, y_b, y_c, proj, x2d, gate, w_branch, w_out, norm_f.reshape(1, d))
    return outs


def _rope_tables(pos):
    t = pos.shape[0]
    posf = pos.astype(F32)

    def cos_sin(n_rot, theta):
        half = n_rot // 2
        inv = theta ** (-jnp.arange(half, dtype=F32) / half)
        ang = posf[:, None] * inv[None, :]
        return jnp.cos(ang), jnp.sin(ang)

    cb, sb = cos_sin(ROPE_DIMS, ROPE_THETA)
    hb = ROPE_DIMS // 2
    tab_b = (jnp.concatenate([cb, cb, jnp.ones((t, HEAD_DIM - ROPE_DIMS), F32)], axis=1),
             jnp.concatenate([jnp.zeros((t, hb), F32), sb, jnp.zeros((t, HEAD_DIM - ROPE_DIMS), F32)], axis=1),
             jnp.concatenate([-sb, jnp.zeros((t, HEAD_DIM - hb), F32)], axis=1))
    cc, sc = cos_sin(HEAD_DIM, RET_THETA)
    tab_c = (jnp.concatenate([cc, cc], axis=1), jnp.concatenate([-sc, sc], axis=1))
    return tab_b, tab_c


def _pad_rows(a2d, n_b, t, t_pad):
    w = a2d.shape[1]
    return jnp.pad(a2d.reshape(n_b, t, w), ((0, 0), (0, t_pad - t), (0, 0))).reshape(n_b * t_pad, w)


def _alpha_lanes(v):
    return jnp.pad(v.astype(F32), (N_HEADS, BA_W - 2 * N_HEADS)).reshape(1, BA_W)


def kernel(x_prompt, x_sample, cache_k, cache_v, state_gdn, state_conv, state_ret, page_table, c_prompt, c_sample,
           norm_in, w_ada, b_ada, w_in, conv_w, a_log, dt_bias, norm_a, norm_c, w_branch, w_out, norm_f):
    n_b, seq, d = x_prompt.shape
    n_db, dec_seq, _ = x_sample.shape
    depth = w_in.shape[0]
    n_pages = page_table.shape[1]
    past_len = n_pages * cache_k.shape[2]
    assert d == 2 * BRANCH_W and dec_seq <= SUBLANES and dec_seq >= CONV_W - 1
    assert seq % GDN_CHUNK == 0 and seq % RET_CHUNK == 0
    t_pad = SUBLANES

    ba0 = 4 * BRANCH_W
    w_main = jnp.concatenate([w_in[:, :, :ba0], w_in[:, :, ba0 + 2 * N_HEADS:]], axis=2).astype(BF16)
    w_ba = jnp.pad(w_in[:, :, ba0:ba0 + 2 * N_HEADS], ((0, 0), (0, 0), (0, BA_W - 2 * N_HEADS))).astype(BF16)
    w_ada_b = w_ada.astype(BF16)
    w_branch_b = w_branch.astype(BF16)
    w_out_b = w_out.astype(BF16)

    n_c = n_b + n_db
    c_rows = -(-n_c // SUBLANES) * SUBLANES
    c_all = jnp.pad(jnp.concatenate([c_prompt, c_sample], axis=0), ((0, c_rows - n_c), (0, 0)))
    mods = _mod_call(c_all, w_ada_b, b_ada)

    tab_b_p, tab_c_p = _rope_tables(jnp.arange(seq, dtype=jnp.int32))
    tab_b_s, tab_c_s = _rope_tables(past_len + jnp.arange(t_pad, dtype=jnp.int32))

    zeros_state = jnp.zeros((n_b, N_HEADS, HEAD_DIM, HEAD_DIM), F32)
    zeros_hist = jnp.zeros((n_b, SUBLANES, 3 * BRANCH_W), F32)

    xp = x_prompt.reshape(n_b * seq, d)
    xs = x_sample.reshape(n_db * dec_seq, d)
    outs = {k: [] for k in ("kp", "vp", "ks", "vs", "gp", "gs", "cp", "cs", "rp", "rs")}
    y_p = y_s = None
    for l in range(depth):
        final = l == depth - 1
        alog, dtb = _alpha_lanes(a_log[l]), _alpha_lanes(dt_bias[l])
        na, nc = norm_a[l].reshape(1, HEAD_DIM), norm_c[l].reshape(1, HEAD_DIM)
        shift, scale, gate = jnp.split(mods[l], 3, axis=-1)

        mod_p = [t[:n_b].reshape(n_b, 1, d) for t in (scale, shift, gate)]
        proj, ba = _inproj_call(xp, mod_p[0], mod_p[1], norm_in[l], w_main[l], w_ba[l],
                                tm=min(seq, 1024), rows_per_mod=seq)
        q_rot, k_rot, k_out, v_out = _rope_b_call(proj, tab_b_p, tr=min(seq, 512), t_rows=seq, cache_layout=True)
        y_a, gdn_new = _gdn_call(proj, ba, zeros_hist, zeros_state, conv_w[l], alog, dtb, na,
                                 n_b=n_b, t_rows=seq, C=GDN_CHUNK, n_valid=GDN_CHUNK, n_sub=GDN_CHUNKS_PER_STEP,
                                 carry=True, y_dtype=BF16)
        y_c, ret_new = _ret_call(proj, tab_c_p[0], tab_c_p[1], zeros_state, nc,
                                 n_b=n_b, t_rows=seq, C=RET_CHUNK, n_valid=RET_CHUNK, n_sub=RET_CHUNKS_PER_STEP,
                                 carry=True, y_dtype=BF16)
        y_b = _moba_p_call(q_rot, k_rot, proj, n_b=n_b, t_rows=seq)
        res = _merge_call(y_a, y_b, y_c, proj, xp, mod_p[2], w_branch_b[l], w_out_b[l], norm_f,
                          tm=min(seq, 256), rows_per_mod=seq, final=final)
        xp = res[0]
        if final:
            y_p = res[1]
        proj3 = proj.reshape(n_b, seq, MAIN_W)
        outs["kp"].append(k_out.reshape(n_b, seq, N_HEADS, HEAD_DIM))
        outs["vp"].append(v_out.reshape(n_b, seq, N_HEADS, HEAD_DIM))
        outs["gp"].append(gdn_new)
        outs["cp"].append(proj3[:, seq - (CONV_W - 1):, :3 * BRANCH_W])
        outs["rp"].append(ret_new)

        mod_s = [jnp.repeat(t[n_b:n_c], dec_seq, axis=0) for t in (scale, shift, gate)]
        proj_s, ba_s = _inproj_call(xs, mod_s[0], mod_s[1], norm_in[l], w_main[l], w_ba[l],
                                    tm=n_db * dec_seq, rows_per_mod=dec_seq)
        proj_sp = _pad_rows(proj_s, n_db, dec_seq, t_pad)
        ba_sp = _pad_rows(ba_s, n_db, dec_seq, t_pad)
        hist = jnp.pad(state_conv[l], ((0, 0), (SUBLANES - (CONV_W - 1), 0), (0, 0)))
        q_rot_s, k_rot_s = _rope_b_call(proj_sp, tab_b_s, tr=t_pad, t_rows=t_pad, cache_layout=False)
        y_a_s, gdn_new_s = _gdn_call(proj_sp, ba_sp, hist.reshape(n_db * SUBLANES, 3 * BRANCH_W), state_gdn[l],
                                     conv_w[l], alog, dtb, na, n_b=n_db, t_rows=t_pad, C=t_pad, n_valid=dec_seq,
                                     n_sub=math.gcd(n_db, SAMPLE_SEQS_PER_STEP), carry=False, y_dtype=F32)
        y_c_s, ret_new_s = _ret_call(proj_sp, tab_c_s[0], tab_c_s[1], state_ret[l], nc,
                                     n_b=n_db, t_rows=t_pad, C=t_pad, n_valid=dec_seq,
                                     n_sub=math.gcd(n_db, SAMPLE_SEQS_PER_STEP), carry=False, y_dtype=F32)
        y_b_s = _moba_s_call(page_table, q_rot_s, k_rot_s, proj_sp, cache_k, cache_v, l, n_valid=dec_seq)
        unpad = lambda y: y.reshape(n_db, t_pad, BRANCH_W)[:, :dec_seq].reshape(n_db * dec_seq, BRANCH_W)
        res_s = _merge_call(unpad(y_a_s), unpad(y_b_s), unpad(y_c_s), proj_s, xs, mod_s[2],
                            w_branch_b[l], w_out_b[l], norm_f, tm=n_db * dec_seq, rows_per_mod=dec_seq, final=final)
        xs = res_s[0]
        if final:
            y_s = res_s[1]
        proj_s3 = proj_s.reshape(n_db, dec_seq, MAIN_W)
        outs["ks"].append(k_rot_s.reshape(n_db, t_pad, N_HEADS, HEAD_DIM)[:, :dec_seq])
        outs["vs"].append(proj_s3[:, :, COL_V_B * BRANCH_W:(COL_V_B + 1) * BRANCH_W]
                          .reshape(n_db, dec_seq, N_HEADS, HEAD_DIM))
        outs["gs"].append(gdn_new_s)
        outs["cs"].append(proj_s3[:, dec_seq - (CONV_W - 1):, :3 * BRANCH_W])
        outs["rs"].append(ret_new_s)

    st = {k: jnp.stack(v) for k, v in outs.items()}
    return (y_p.reshape(n_b, seq, d), y_s.reshape(n_db, dec_seq, d),
            st["kp"], st["vp"], st["ks"], st["vs"], st["gp"], st["gs"],
            st["cp"], st["cs"], st["rp"], st["rs"])
```
